```python
import jax, jax.numpy as jnp
from jax import lax
import numpy as np

D_MODEL = 2048
BATCH = 16
SEQ = 256
DEPTH = 1
DEC_BATCH = 4
DEC_SEQ = 2048
PAST_LEN = 512

GRID_W = 64
N_HEADS_A = 16
HEAD_DIM_A = 64
WIDTH_A = N_HEADS_A * HEAD_DIM_A
NA_ROWS = 8
NA_COLS = 16
N_GROUPS_B = 8
GROUP_DIM_B = 128
WIDTH_B = N_GROUPS_B * GROUP_DIM_B
CHUNK = 128
MIX_WIDTH = WIDTH_A + WIDTH_B
IN_WIDTH = 3 * WIDTH_A + 2 * WIDTH_B
D_FF = 5632
N_MOD = 9
EPS = 1e-6
Q_BLOCK = 128
NEG_INF = -1e30

kernel_name = "hybrid_natten_gmlp_macaron_prefix_dit"


def rms_norm(x, g):
    xf = x.astype(jnp.float32)
    y = xf * lax.rsqrt(jnp.mean(xf * xf, axis=-1, keepdims=True) + EPS)
    return (y * g.astype(jnp.float32)).astype(x.dtype)


def adaln(cvec, w_ada, b_ada):
    m = jax.nn.silu(cvec) @ w_ada + b_ada
    return jnp.split(m, N_MOD, axis=-1)


def swiglu_sublayer(x, g, w_gate, w_up, w_down, shift, scale, gate):
    h = rms_norm(x, g) * (1 + scale) + shift
    f = (jax.nn.silu(h @ w_gate) * (h @ w_up)) @ w_down
    return x + 0.5 * gate * f


def mixer_inputs(x, g, w_in, shift, scale):
    B, L, _ = x.shape
    h = rms_norm(x, g) * (1 + scale) + shift
    p = h @ w_in
    q, k, v, u, vg = jnp.split(p, [WIDTH_A, 2 * WIDTH_A, 3 * WIDTH_A, 3 * WIDTH_A + WIDTH_B], axis=-1)
    shp = (B, L, N_HEADS_A, HEAD_DIM_A)
    return q.reshape(shp), k.reshape(shp), v.reshape(shp), u, vg


def spatial_gating(u, vg, norm_g, w_s, b_s):
    B, L, _ = u.shape
    u = jax.nn.gelu(u)
    vg = jax.nn.gelu(vg).reshape(B, L, N_GROUPS_B, GROUP_DIM_B)
    vg = rms_norm(vg, norm_g.reshape(N_GROUPS_B, GROUP_DIM_B))
    vc = vg.reshape(B, L // CHUNK, CHUNK, N_GROUPS_B, GROUP_DIM_B)
    s = jnp.einsum('gij,bnjgc->bnigc', w_s, vc) + b_s.T[None, None, :, :, None]
    return u * s.reshape(B, L, WIDTH_B)


def mixer_output(x, a, gm, na_g, nb_g, w_out, gate):
    y = jnp.concatenate([rms_norm(a, na_g), rms_norm(gm, nb_g)], axis=-1) @ w_out
    return x + gate * y


def context_attention(q, k, v):
    B, S, H, dh = q.shape
    qb = (q * dh ** -0.5).reshape(B, S // Q_BLOCK, Q_BLOCK, H, dh).transpose(1, 0, 2, 3, 4)

    def block(qi):
        s = jnp.einsum('bqhd,bkhd->bhqk', qi, k).astype(jnp.float32)
        p = jax.nn.softmax(s, axis=-1).astype(v.dtype)
        return jnp.einsum('bhqk,bkhd->bqhd', p, v)

    o = lax.map(block, qb)
    return o.transpose(1, 0, 2, 3, 4).reshape(B, S, H * dh)


def neighbourhood_attention(q, k, v, ctx_k, ctx_v, rpb):
    B, N, H, dh = q.shape
    rows = N // GRID_W
    kr = min(NA_ROWS, rows)
    r = jnp.arange(rows)
    row_start = jnp.clip(r - kr // 2, 0, rows - kr)
    row_idx = row_start[:, None] + jnp.arange(kr)[None, :]
    col = jnp.arange(GRID_W)
    col_start = jnp.clip(col - NA_COLS // 2, 0, GRID_W - NA_COLS)
    col_mask = (col[None, :] >= col_start[:, None]) & (col[None, :] < col_start[:, None] + NA_COLS)
    dr = row_idx - r[:, None]
    dc_idx = jnp.clip(col[None, :] - col[:, None] + NA_COLS - 1, 0, 2 * NA_COLS - 2)
    bias = rpb[:, dr + NA_ROWS - 1][..., dc_idx]
    bias = bias.transpose(0, 1, 3, 2, 4).astype(jnp.float32)
    bias = jnp.where(col_mask[None, None, :, None, :], bias, NEG_INF)

    qs = (q * dh ** -0.5).reshape(B, rows, GRID_W, H, dh)
    kg = k.reshape(B, rows, GRID_W, H, dh)[:, row_idx]
    vg = v.reshape(B, rows, GRID_W, H, dh)[:, row_idx]
    s_loc = jnp.einsum('brqhd,brkwhd->bhrqkw', qs, kg).astype(jnp.float32) + bias[None]
    s_ctx = jnp.einsum('brqhd,bphd->bhrqp', qs, ctx_k).astype(jnp.float32)
    n_loc = kr * GRID_W
    s = jnp.concatenate([s_loc.reshape(B, H, rows, GRID_W, n_loc), s_ctx], axis=-1)
    p = jax.nn.softmax(s, axis=-1).astype(v.dtype)
    p_loc = p[..., :n_loc].reshape(B, H, rows, GRID_W, kr, GRID_W)
    p_ctx = p[..., n_loc:]
    o = (jnp.einsum('bhrqkw,brkwhd->brqhd', p_loc, vg)
         + jnp.einsum('bhrqp,bphd->brqhd', p_ctx, ctx_v))
    return o.reshape(B, N, H * dh)


def setup_inputs(seed: int = 0) -> dict:
    key = jax.random.key(seed)
    ks = jax.random.split(key, 26)
    f32 = jnp.float32
    D, L = D_MODEL, DEPTH

    def nrm(k, shape, s):
        return jax.random.normal(k, shape, f32) * s

    def gain(k, shape):
        return 1.0 + 0.01 * jax.random.normal(k, shape, f32)

    return {
        "x_prompt": nrm(ks[0], (BATCH, SEQ, D), 1.0),
        "x_sample": nrm(ks[1], (DEC_BATCH, DEC_SEQ, D), 1.0),
        "cache_k": nrm(ks[2], (DEC_BATCH, L, PAST_LEN, N_HEADS_A, HEAD_DIM_A), 1.0),
        "cache_v": nrm(ks[3], (DEC_BATCH, L, PAST_LEN, N_HEADS_A, HEAD_DIM_A), 1.0),
        "c": nrm(ks[4], (DEC_BATCH, D), 1.0),
        "c_ctx": nrm(ks[5], (D,), 1.0),
        "w_ada": nrm(ks[6], (L, D, N_MOD * D), 0.1 * D ** -0.5),
        "b_ada": nrm(ks[7], (L, N_MOD * D), 0.01),
        "ffn1_norm": gain(ks[8], (L, D)),
        "ffn1_w_gate": nrm(ks[9], (L, D, D_FF), D ** -0.5),
        "ffn1_w_up": nrm(ks[10], (L, D, D_FF), D ** -0.5),
        "ffn1_w_down": nrm(ks[11], (L, D_FF, D), D_FF ** -0.5),
        "mix_norm": gain(ks[12], (L, D)),
        "w_in": nrm(ks[13], (L, D, IN_WIDTH), D ** -0.5),
        "rpb": nrm(ks[14], (L, N_HEADS_A, 2 * NA_ROWS - 1, 2 * NA_COLS - 1), 0.1),
        "gmlp_norm": gain(ks[15], (L, WIDTH_B)),
        "w_s": nrm(ks[16], (L, N_GROUPS_B, CHUNK, CHUNK), CHUNK ** -0.5),
        "b_s": gain(ks[17], (L, N_GROUPS_B, CHUNK)),
        "out_norm_a": gain(ks[18], (L, WIDTH_A)),
        "out_norm_b": gain(ks[19], (L, WIDTH_B)),
        "w_out": nrm(ks[20], (L, MIX_WIDTH, D), MIX_WIDTH ** -0.5),
        "ffn2_norm": gain(ks[21], (L, D)),
        "ffn2_w_gate": nrm(ks[22], (L, D, D_FF), D ** -0.5),
        "ffn2_w_up": nrm(ks[23], (L, D, D_FF), D ** -0.5),
        "ffn2_w_down": nrm(ks[24], (L, D_FF, D), D_FF ** -0.5),
        "final_norm": gain(ks[25], (D,)),
    }


def reference(x_prompt, x_sample, cache_k, cache_v, c, c_ctx, w_ada, b_ada,
              ffn1_norm, ffn1_w_gate, ffn1_w_up, ffn1_w_down,
              mix_norm, w_in, rpb, gmlp_norm, w_s, b_s, out_norm_a, out_norm_b, w_out,
              ffn2_norm, ffn2_w_gate, ffn2_w_up, ffn2_w_down, final_norm):
    x_ctx = x_prompt
    x_lat = x_sample
    new_k = []
    new_v = []
    for l in range(DEPTH):
        m_ctx = adaln(c_ctx, w_ada[l], b_ada[l])
        m_lat = [m[:, None, :] for m in adaln(c, w_ada[l], b_ada[l])]

        x_ctx = swiglu_sublayer(x_ctx, ffn1_norm[l], ffn1_w_gate[l], ffn1_w_up[l], ffn1_w_down[l],
                                m_ctx[0], m_ctx[1], m_ctx[2])
        q, k, v, u, vg = mixer_inputs(x_ctx, mix_norm[l], w_in[l], m_ctx[3], m_ctx[4])
        a = context_attention(q, k, v)
        gm = spatial_gating(u, vg, gmlp_norm[l], w_s[l], b_s[l])
        x_ctx = mixer_output(x_ctx, a, gm, out_norm_a[l], out_norm_b[l], w_out[l], m_ctx[5])
        x_ctx = swiglu_sublayer(x_ctx, ffn2_norm[l], ffn2_w_gate[l], ffn2_w_up[l], ffn2_w_down[l],
                                m_ctx[6], m_ctx[7], m_ctx[8])
        new_k.append(k)
        new_v.append(v)

        x_lat = swiglu_sublayer(x_lat, ffn1_norm[l], ffn1_w_gate[l], ffn1_w_up[l], ffn1_w_down[l],
                                m_lat[0], m_lat[1], m_lat[2])
        q, k, v, u, vg = mixer_inputs(x_lat, mix_norm[l], w_in[l], m_lat[3], m_lat[4])
        a = neighbourhood_attention(q, k, v, cache_k[:, l], cache_v[:, l], rpb[l])
        gm = spatial_gating(u, vg, gmlp_norm[l], w_s[l], b_s[l])
        x_lat = mixer_output(x_lat, a, gm, out_norm_a[l], out_norm_b[l], w_out[l], m_lat[5])
        x_lat = swiglu_sublayer(x_lat, ffn2_norm[l], ffn2_w_gate[l], ffn2_w_up[l], ffn2_w_down[l],
                                m_lat[6], m_lat[7], m_lat[8])

    y_prompt = rms_norm(x_ctx, final_norm)
    y_sample = rms_norm(x_lat, final_norm)
    state_k = jnp.stack(new_k, axis=1)
    state_v = jnp.stack(new_v, axis=1)
    return (y_prompt, y_sample, state_k, state_v)
```

```python
import functools

import jax
import jax.numpy as jnp
from jax import lax
from jax.experimental import pallas as pl
from jax.experimental.pallas import tpu as pltpu

D_MODEL = 2048
N_HEADS_A = 16
HEAD_DIM_A = 64
WIDTH_A = N_HEADS_A * HEAD_DIM_A
GRID_W = 64
NA_ROWS = 8
NA_COLS = 16
N_GROUPS_B = 8
GROUP_DIM_B = 128
WIDTH_B = N_GROUPS_B * GROUP_DIM_B
CHUNK = 128
MIX_WIDTH = WIDTH_A + WIDTH_B
IN_WIDTH = 3 * WIDTH_A + 2 * WIDTH_B
D_FF = 5632
N_MOD = 9
EPS = 1e-6
NEG_INF = -1e30

LANES = 128
HEADS_PER_BLOCK = LANES // HEAD_DIM_A
N_HEAD_BLOCKS = N_HEADS_A // HEADS_PER_BLOCK
Q_ROWS = 4
Q_TOK = Q_ROWS * GRID_W
WIN_BLOCKS = 3
VMEM_LIMIT = 56 * 1024 * 1024

F32 = jnp.float32
BF16 = jnp.bfloat16


def _dot(a, b):
    return jnp.dot(a, b, preferred_element_type=F32)


def _dot_nt(a, b):
    return lax.dot_general(a, b, (((1,), (1,)), ((), ())), preferred_element_type=F32)


def _rms(x, g):
    ms = jnp.mean(x * x, axis=-1, keepdims=True)
    return x * lax.rsqrt(ms + EPS) * g


def _silu(x):
    return x * jax.nn.sigmoid(x)


def _gelu_tanh(x):
    c = 0.7978845608028654
    return 0.5 * x * (1.0 + jnp.tanh(c * (x + 0.044715 * (x * x * x))))


def _adaln_kernel(c_ref, w_ref, b_ref, o_ref):
    s = _silu(c_ref[...]).astype(BF16)
    o_ref[...] = _dot(s, w_ref[...].astype(BF16)) + b_ref[...]


def _adaln(cvec, w_ada, b_ada, tn=1024):
    rows = cvec.shape[0]
    n = w_ada.shape[1]
    return pl.pallas_call(
        _adaln_kernel,
        out_shape=jax.ShapeDtypeStruct((rows, n), F32),
        grid=(n // tn,),
        in_specs=[
            pl.BlockSpec((rows, D_MODEL), lambda j: (0, 0)),
            pl.BlockSpec((D_MODEL, tn), lambda j: (0, j)),
            pl.BlockSpec((1, tn), lambda j: (0, j)),
        ],
        out_specs=pl.BlockSpec((rows, tn), lambda j: (0, j)),
        compiler_params=pltpu.CompilerParams(
            dimension_semantics=("arbitrary",), vmem_limit_bytes=VMEM_LIMIT),
        name="adaln",
    )(cvec, w_ada, b_ada)


def _ffn_kernel(*refs, mod_base, final):
    if final:
        x_ref, mod_ref, g_ref, wg_ref, wu_ref, wd_ref, fg_ref, o_ref, h_ref, acc_ref = refs
    else:
        x_ref, mod_ref, g_ref, wg_ref, wu_ref, wd_ref, o_ref, h_ref, acc_ref = refs
    j = pl.program_id(1)

    @pl.when(j == 0)
    def _():
        shift = mod_ref[0, mod_base:mod_base + 1, :]
        scale = mod_ref[0, mod_base + 1:mod_base + 2, :]
        h = _rms(x_ref[...], g_ref[...]) * (1.0 + scale) + shift
        h_ref[...] = h.astype(BF16)
        acc_ref[...] = jnp.zeros_like(acc_ref)

    h = h_ref[...]
    a = (_silu(_dot(h, wg_ref[...])) * _dot(h, wu_ref[...])).astype(BF16)
    acc_ref[...] += _dot(a, wd_ref[...])

    @pl.when(j == pl.num_programs(1) - 1)
    def _():
        gate = mod_ref[0, mod_base + 2:mod_base + 3, :]
        out = x_ref[...] + 0.5 * gate * acc_ref[...]
        if final:
            out = _rms(out, fg_ref[...])
        o_ref[...] = out


def _ffn(x, mod, norm_g, wg, wu, wd, *, mod_base, rows_per_mod, final_g=None, tm=512, tf=512):
    t = x.shape[0]
    final = final_g is not None
    in_specs = [
        pl.BlockSpec((tm, D_MODEL), lambda i, j: (i, 0)),
        pl.BlockSpec((1, N_MOD, D_MODEL), lambda i, j: (i * tm // rows_per_mod, 0, 0)),
        pl.BlockSpec((1, D_MODEL), lambda i, j: (0, 0)),
        pl.BlockSpec((D_MODEL, tf), lambda i, j: (0, j)),
        pl.BlockSpec((D_MODEL, tf), lambda i, j: (0, j)),
        pl.BlockSpec((tf, D_MODEL), lambda i, j: (j, 0)),
    ]
    args = [x, mod, norm_g, wg, wu, wd]
    if final:
        in_specs.append(pl.BlockSpec((1, D_MODEL), lambda i, j: (0, 0)))
        args.append(final_g)
    return pl.pallas_call(
        functools.partial(_ffn_kernel, mod_base=mod_base, final=final),
        out_shape=jax.ShapeDtypeStruct((t, D_MODEL), F32),
        grid=(t // tm, D_FF // tf),
        in_specs=in_specs,
        out_specs=pl.BlockSpec((tm, D_MODEL), lambda i, j: (i, 0)),
        scratch_shapes=[pltpu.VMEM((tm, D_MODEL), BF16), pltpu.VMEM((tm, D_MODEL), F32)],
        compiler_params=pltpu.CompilerParams(
            dimension_semantics=("parallel", "arbitrary"), vmem_limit_bytes=VMEM_LIMIT),
        name="ffn_final" if final else "ffn",
    )(*args)


def _mixin_kernel(x_ref, mod_ref, g_ref, w_ref, q_ref, k_ref, v_ref, u_ref, vg_ref):
    shift = mod_ref[0, 3:4, :]
    scale = mod_ref[0, 4:5, :]
    h = (_rms(x_ref[...], g_ref[...]) * (1.0 + scale) + shift).astype(BF16)
    for idx, o_ref in enumerate((q_ref, k_ref, v_ref, u_ref, vg_ref)):
        w = w_ref[:, idx * WIDTH_A:(idx + 1) * WIDTH_A]
        o_ref[...] = _dot(h, w).astype(o_ref.dtype)


def _mixin(x, mod, norm_g, w_in, *, rows_per_mod, kv_dtype, tm=512):
    t = x.shape[0]
    out_dtypes = (BF16, kv_dtype, kv_dtype, BF16, BF16)
    return pl.pallas_call(
        _mixin_kernel,
        out_shape=[jax.ShapeDtypeStruct((t, WIDTH_A), dt) for dt in out_dtypes],
        grid=(t // tm,),
        in_specs=[
            pl.BlockSpec((tm, D_MODEL), lambda i: (i, 0)),
            pl.BlockSpec((1, N_MOD, D_MODEL), lambda i: (i * tm // rows_per_mod, 0, 0)),
            pl.BlockSpec((1, D_MODEL), lambda i: (0, 0)),
            pl.BlockSpec((D_MODEL, IN_WIDTH), lambda i: (0, 0), pipeline_mode=pl.Buffered(1)),
        ],
        out_specs=[pl.BlockSpec((tm, WIDTH_A), lambda i: (i, 0)) for _ in out_dtypes],
        compiler_params=pltpu.CompilerParams(
            dimension_semantics=("parallel",), vmem_limit_bytes=VMEM_LIMIT),
        name="mixin",
    )(x, mod, norm_g, w_in)


def _attend(q, key_segments, value_segments, biases):
    lane = lax.broadcasted_iota(jnp.int32, (1, LANES), 1)
    out = jnp.zeros(q.shape, F32)
    for hh in range(HEADS_PER_BLOCK):
        head = (lane // HEAD_DIM_A) == hh
        qh = jnp.where(head, q, jnp.zeros_like(q)) * (HEAD_DIM_A ** -0.5)
        scores = []
        for k, b in zip(key_segments, biases):
            s = _dot_nt(qh, k)
            if b is not None:
                s = s + b[hh]
            scores.append(s)
        m = scores[0].max(axis=-1, keepdims=True)
        for s in scores[1:]:
            m = jnp.maximum(m, s.max(axis=-1, keepdims=True))
        den = jnp.zeros_like(m)
        acc = jnp.zeros(q.shape, F32)
        for s, v in zip(scores, value_segments):
            e = jnp.exp(s - m)
            den = den + e.sum(axis=-1, keepdims=True)
            acc = acc + _dot(e.astype(BF16), jnp.where(head, v, jnp.zeros_like(v)))
        out = out + acc / den
    return out


def _ctx_attn_kernel(q_ref, k_ref, v_ref, o_ref):
    k = k_ref[...].astype(BF16)
    v = v_ref[...].astype(BF16)
    o_ref[...] = _attend(q_ref[...], [k], [v], [None]).astype(o_ref.dtype)


def _ctx_attention(q, k, v, seq):
    t = q.shape[0]
    blk = lambda: pl.BlockSpec((seq, LANES), lambda b, p: (b, p))
    return pl.pallas_call(
        _ctx_attn_kernel,
        out_shape=jax.ShapeDtypeStruct((t, WIDTH_A), BF16),
        grid=(t // seq, N_HEAD_BLOCKS),
        in_specs=[blk(), blk(), blk()],
        out_specs=blk(),
        compiler_params=pltpu.CompilerParams(
            dimension_semantics=("parallel", "parallel"), vmem_limit_bytes=VMEM_LIMIT),
        name="ctx_attn",
    )(q, k, v)


def _nbr_attn_kernel(q_ref, k0_ref, k1_ref, k2_ref, v0_ref, v1_ref, v2_ref,
                     ck_ref, cv_ref, bias_ref, o_ref):
    keys = [k0_ref[...], k1_ref[...], k2_ref[...], ck_ref[0].astype(BF16)]
    values = [v0_ref[...], v1_ref[...], v2_ref[...], cv_ref[0].astype(BF16)]
    biases = [bias_ref.at[0, :, :, w * Q_TOK:(w + 1) * Q_TOK] for w in range(WIN_BLOCKS)] + [None]
    o_ref[...] = _attend(q_ref[...], keys, values, biases).astype(o_ref.dtype)


def _nbr_attention(q, k, v, ck, cv, bias, n_tok):
    t = q.shape[0]
    batch = t // n_tok
    past = ck.shape[1]
    qb = n_tok // Q_TOK
    def win(m):
        return jnp.clip(m - 1, 0, qb - WIN_BLOCKS)
    def variant(m):
        return jnp.minimum(m, 1) + jnp.maximum(m - (qb - 2), 0)
    qspec = pl.BlockSpec((Q_TOK, LANES), lambda p, m, b: (b * qb + m, p))
    def kvspec(w):
        return pl.BlockSpec((Q_TOK, LANES), lambda p, m, b: (b * qb + win(m) + w, p))
    cspec = pl.BlockSpec((1, past, LANES), lambda p, m, b: (b, 0, p))
    bspec = pl.BlockSpec((1, HEADS_PER_BLOCK, Q_TOK, WIN_BLOCKS * Q_TOK),
                         lambda p, m, b: (variant(m), p, 0, 0))
    return pl.pallas_call(
        _nbr_attn_kernel,
        out_shape=jax.ShapeDtypeStruct((t, WIDTH_A), BF16),
        grid=(N_HEAD_BLOCKS, qb, batch),
        in_specs=[qspec] + [kvspec(w) for w in range(WIN_BLOCKS)]
                 + [kvspec(w) for w in range(WIN_BLOCKS)] + [cspec, cspec, bspec],
        out_specs=qspec,
        compiler_params=pltpu.CompilerParams(
            dimension_semantics=("parallel", "parallel", "arbitrary"),
            vmem_limit_bytes=VMEM_LIMIT),
        name="nbr_attn",
    )(q, k, k, k, v, v, v, ck, cv, bias)


def _bias_kernel(rpb_ref, o_ref, *, rows):
    h = pl.program_id(0)
    n_dr = 2 * NA_ROWS - 1
    n_dc = 2 * NA_COLS - 1
    qc = lax.broadcasted_iota(jnp.int32, (GRID_W, LANES), 0)
    ln = lax.broadcasted_iota(jnp.int32, (GRID_W, LANES), 1)
    kc = ln % GRID_W
    dc_idx = jnp.clip(kc - qc + NA_COLS - 1, 0, n_dc - 1)
    col_start = jnp.clip(qc - NA_COLS // 2, 0, GRID_W - NA_COLS)
    col_ok = (kc >= col_start) & (kc < col_start + NA_COLS)
    neg = jnp.full((GRID_W, LANES), NEG_INF, F32)
    toeplitz = []
    for dr in range(n_dr):
        acc = jnp.zeros((GRID_W, LANES), F32)
        for dc in range(n_dc):
            acc = jnp.where(dc_idx == dc, rpb_ref[h * (n_dr * n_dc) + dr * n_dc + dc], acc)
        toeplitz.append(jnp.where(col_ok, acc, neg))
    qb = rows // Q_ROWS
    kr = min(NA_ROWS, rows)
    for var, m in enumerate((0, 1, qb - 1)):
        wb = min(max(m - 1, 0), qb - WIN_BLOCKS)
        for a in range(Q_ROWS):
            r = Q_ROWS * m + a
            row_start = min(max(r - kr // 2, 0), rows - kr)
            for tp in range(WIN_BLOCKS * Q_ROWS // 2):
                halves = []
                for t in (2 * tp, 2 * tp + 1):
                    key_row = Q_ROWS * wb + t
                    ok = row_start <= key_row < row_start + kr
                    halves.append(toeplitz[key_row - r + NA_ROWS - 1] if ok else neg)
                tile = jnp.where(ln < GRID_W, halves[0], halves[1])
                o_ref[var, 0, a * GRID_W:(a + 1) * GRID_W, tp * LANES:(tp + 1) * LANES] = tile


def _bias_table(rpb, rows):
    return pl.pallas_call(
        functools.partial(_bias_kernel, rows=rows),
        out_shape=jax.ShapeDtypeStruct((3, N_HEADS_A, Q_TOK, WIN_BLOCKS * Q_TOK), F32),
        grid=(N_HEADS_A,),
        in_specs=[pl.BlockSpec(memory_space=pltpu.SMEM)],
        out_specs=pl.BlockSpec((3, 1, Q_TOK, WIN_BLOCKS * Q_TOK), lambda h: (0, h, 0, 0)),
        compiler_params=pltpu.CompilerParams(
            dimension_semantics=("arbitrary",), vmem_limit_bytes=VMEM_LIMIT),
        name="nbr_bias",
    )(rpb.reshape(-1))


def _mixout_kernel(x_ref, a_ref, u_ref, vg_ref, mod_ref, gn_ref, ws_ref, bs_ref,
                   na_ref, nb_ref, wo_ref, o_ref, gm_ref):
    tm = x_ref.shape[0]
    for g in range(N_GROUPS_B):
        cols = slice(g * GROUP_DIM_B, (g + 1) * GROUP_DIM_B)
        vn = _rms(_gelu_tanh(vg_ref[:, cols].astype(F32)), gn_ref[:, cols]).astype(BF16)
        ug = _gelu_tanh(u_ref[:, cols].astype(F32))
        w = ws_ref[g]
        b = bs_ref[:, g:g + 1]
        for c in range(tm // CHUNK):
            rows = slice(c * CHUNK, (c + 1) * CHUNK)
            gm_ref[rows, cols] = ug[rows] * (_dot(w, vn[rows]) + b)
    ya = _rms(a_ref[...].astype(F32), na_ref[...]).astype(BF16)
    yb = _rms(gm_ref[...], nb_ref[...]).astype(BF16)
    y = _dot(ya, wo_ref[:WIDTH_A, :]) + _dot(yb, wo_ref[WIDTH_A:, :])
    o_ref[...] = x_ref[...] + mod_ref[0, 5:6, :] * y


def _mixout(x, a, u, vg, mod, gmlp_norm, w_s, b_s_t, na_g, nb_g, w_out, *, rows_per_mod, tm=512):
    t = x.shape[0]
    tok = lambda w: pl.BlockSpec((tm, w), lambda i: (i, 0))
    full = lambda shape: pl.BlockSpec(shape, lambda i: (0,) * len(shape))
    return pl.pallas_call(
        _mixout_kernel,
        out_shape=jax.ShapeDtypeStruct((t, D_MODEL), F32),
        grid=(t // tm,),
        in_specs=[
            tok(D_MODEL), tok(WIDTH_A), tok(WIDTH_B), tok(WIDTH_B),
            pl.BlockSpec((1, N_MOD, D_MODEL), lambda i: (i * tm // rows_per_mod, 0, 0)),
            full((1, WIDTH_B)),
            full((N_GROUPS_B, CHUNK, CHUNK)),
            full((CHUNK, N_GROUPS_B)),
            full((1, WIDTH_A)),
            full((1, WIDTH_B)),
            pl.BlockSpec((MIX_WIDTH, D_MODEL), lambda i: (0, 0), pipeline_mode=pl.Buffered(1)),
        ],
        out_specs=tok(D_MODEL),
        scratch_shapes=[pltpu.VMEM((tm, WIDTH_B), F32)],
        compiler_params=pltpu.CompilerParams(
            dimension_semantics=("parallel",), vmem_limit_bytes=VMEM_LIMIT),
        name="mixout",
    )(x, a, u, vg, mod, gmlp_norm, w_s, b_s_t, na_g, nb_g, w_out)


def kernel(x_prompt, x_sample, cache_k, cache_v, c, c_ctx, w_ada, b_ada, ffn1_norm, ffn1_w_gate, ffn1_w_up, ffn1_w_down, mix_norm, w_in, rpb, gmlp_norm, w_s, b_s, out_norm_a, out_norm_b, w_out, ffn2_norm, ffn2_w_gate, ffn2_w_up, ffn2_w_down, final_norm):
    batch, seq, _ = x_prompt.shape
    dec_batch, dec_seq, _ = x_sample.shape
    depth = w_ada.shape[0]
    x_ctx = x_prompt.reshape(batch * seq, D_MODEL)
    x_lat = x_sample.reshape(dec_batch * dec_seq, D_MODEL)
    final_g = final_norm.reshape(1, D_MODEL)

    cvec = jnp.concatenate([c_ctx[None, :], c], axis=0)
    mod_rows = -(-cvec.shape[0] // 8) * 8
    cvec = jnp.pad(cvec, ((0, mod_rows - cvec.shape[0]), (0, 0)))

    new_k, new_v = [], []
    for l in range(depth):
        last = l == depth - 1
        mod = _adaln(cvec, w_ada[l], b_ada[l][None, :]).reshape(mod_rows, N_MOD, D_MODEL)
        mods = (mod[0:1], mod[1:1 + dec_batch])
        rows_per_mod = (batch * seq, dec_seq)

        f1 = (ffn1_norm[l][None, :], ffn1_w_gate[l].astype(BF16), ffn1_w_up[l].astype(BF16),
              ffn1_w_down[l].astype(BF16))
        f2 = (ffn2_norm[l][None, :], ffn2_w_gate[l].astype(BF16), ffn2_w_up[l].astype(BF16),
              ffn2_w_down[l].astype(BF16))
        mix_g = mix_norm[l][None, :]
        w_in_l = w_in[l].astype(BF16)
        w_out_l = w_out[l].astype(BF16)
        w_s_l = w_s[l].astype(BF16)
        b_s_t = b_s[l].T
        gn = gmlp_norm[l][None, :]
        na_g = out_norm_a[l][None, :]
        nb_g = out_norm_b[l][None, :]
        bias = _bias_table(rpb[l], dec_seq // GRID_W)
        ck = cache_k[:, l].reshape(dec_batch, -1, WIDTH_A)
        cv = cache_v[:, l].reshape(dec_batch, -1, WIDTH_A)

        xs = []
        for path, (x, m, rpm) in enumerate(zip((x_ctx, x_lat), mods, rows_per_mod)):
            x = _ffn(x, m, *f1, mod_base=0, rows_per_mod=rpm)
            q, k, v, u, vg = _mixin(x, m, mix_g, w_in_l, rows_per_mod=rpm,
                                    kv_dtype=F32 if path == 0 else BF16)
            if path == 0:
                a = _ctx_attention(q, k, v, seq)
                new_k.append(k.reshape(batch, seq, N_HEADS_A, HEAD_DIM_A))
                new_v.append(v.reshape(batch, seq, N_HEADS_A, HEAD_DIM_A))
            else:
                a = _nbr_attention(q, k, v, ck, cv, bias, dec_seq)
            x = _mixout(x, a, u, vg, m, gn, w_s_l, b_s_t, na_g, nb_g, w_out_l, rows_per_mod=rpm)
            x = _ffn(x, m, *f2, mod_base=6, rows_per_mod=rpm,
                     final_g=final_g if last else None)
            xs.append(x)
        x_ctx, x_lat = xs

    if depth == 0:
        raise ValueError("depth must be positive")
    y_prompt = x_ctx.reshape(batch, seq, D_MODEL)
    y_sample = x_lat.reshape(dec_batch, dec_seq, D_MODEL)
    return (y_prompt, y_sample, jnp.stack(new_k, axis=1), jnp.stack(new_v, axis=1))
```

```python
import functools
import math

import jax
import jax.numpy as jnp
from jax import lax
from jax.experimental import pallas as pl
from jax.experimental.pallas import tpu as pltpu

D_MODEL = 2048
N_HEADS_A = 16
HEAD_DIM_A = 64
WIDTH_A = N_HEADS_A * HEAD_DIM_A
GRID_W = 64
NA_ROWS = 8
NA_COLS = 16
N_GROUPS_B = 8
GROUP_DIM_B = 128
WIDTH_B = N_GROUPS_B * GROUP_DIM_B
CHUNK = 128
MIX_WIDTH = WIDTH_A + WIDTH_B
IN_WIDTH = 3 * WIDTH_A + 2 * WIDTH_B
D_FF = 5632
N_MOD = 9
EPS = 1e-6
NEG_INF = -1e30

LANES = 128
HEADS_PER_BLOCK = LANES // HEAD_DIM_A
N_HEAD_BLOCKS = N_HEADS_A // HEADS_PER_BLOCK
Q_ROWS = 4
Q_TOK = Q_ROWS * GRID_W
WIN_BLOCKS = 3
VMEM_LIMIT = 56 * 1024 * 1024
ROW_CHUNK = 16
NORM_UNROLL = 8
FFN_OUT_CHUNK = 512
PAIRS_PER_ITER = 4
LOG2E = math.log2(math.e)
Q_SCALE = HEAD_DIM_A ** -0.5 * LOG2E

F32 = jnp.float32
BF16 = jnp.bfloat16


def _dot(a, b):
    return jnp.dot(a, b, preferred_element_type=F32)


def _dot_nt(a, b):
    return lax.dot_general(a, b, (((1,), (1,)), ((), ())), preferred_element_type=F32)


def _rms(x, g):
    ms = jnp.mean(x * x, axis=-1, keepdims=True)
    return x * lax.rsqrt(ms + EPS) * g


def _silu(x):
    return x * jax.nn.sigmoid(x)


def _gelu_tanh(x):
    c = 0.7978845608028654
    return 0.5 * x * (1.0 + jnp.tanh(c * (x + 0.044715 * (x * x * x))))


def _row_chunk(c):
    return pl.ds(pl.multiple_of(c * ROW_CHUNK, ROW_CHUNK), ROW_CHUNK)


def _lane_tile(v, width):
    return jnp.concatenate([v] * (width // LANES), axis=1)


def _row_inv_rms(x_ref, rs_ref):
    def body(c, carry):
        r = _row_chunk(c)
        x = x_ref[r, :]
        ms = jnp.mean(x * x, axis=-1, keepdims=True)
        rs_ref[r, :] = jnp.broadcast_to(lax.rsqrt(ms + EPS), (ROW_CHUNK, LANES))
        return carry

    lax.fori_loop(0, x_ref.shape[0] // ROW_CHUNK, body, 0, unroll=NORM_UNROLL)


def _adaln_kernel(c_ref, w_ref, b_ref, o_ref):
    s = _silu(c_ref[...]).astype(BF16)
    o_ref[...] = _dot(s, w_ref[...].astype(BF16)) + b_ref[...]


def _adaln(cvec, w_ada, b_ada, tn=1024):
    rows = cvec.shape[0]
    n = w_ada.shape[1]
    return pl.pallas_call(
        _adaln_kernel,
        out_shape=jax.ShapeDtypeStruct((rows, n), F32),
        grid=(n // tn,),
        in_specs=[
            pl.BlockSpec((rows, D_MODEL), lambda j: (0, 0)),
            pl.BlockSpec((D_MODEL, tn), lambda j: (0, j)),
            pl.BlockSpec((1, tn), lambda j: (0, j)),
        ],
        out_specs=pl.BlockSpec((rows, tn), lambda j: (0, j)),
        compiler_params=pltpu.CompilerParams(
            dimension_semantics=("arbitrary",), vmem_limit_bytes=VMEM_LIMIT),
        name="adaln",
    )(cvec, w_ada, b_ada)


def _ffn_kernel(*refs, mod_base, final):
    if final:
        x_ref, mod_ref, g_ref, wg_ref, wu_ref, wd_ref, fg_ref, o_ref, h_ref, rs_ref = refs
    else:
        x_ref, mod_ref, g_ref, wg_ref, wu_ref, wd_ref, o_ref, h_ref, rs_ref = refs
    j = pl.program_id(1)
    n_row_chunks = x_ref.shape[0] // ROW_CHUNK

    @pl.when(j == 0)
    def _():
        _row_inv_rms(x_ref, rs_ref)
        shift = mod_ref[0, mod_base:mod_base + 1, :]
        gain = g_ref[...] * (1.0 + mod_ref[0, mod_base + 1:mod_base + 2, :])

        def body(c, carry):
            r = _row_chunk(c)
            x = x_ref[r, :]
            h_ref[r, :] = (x * _lane_tile(rs_ref[r, :], D_MODEL) * gain + shift).astype(BF16)
            o_ref[r, :] = x
            return carry

        lax.fori_loop(0, n_row_chunks, body, 0, unroll=NORM_UNROLL)

    h = h_ref[...]
    a = (_silu(_dot(h, wg_ref[...])) * _dot(h, wu_ref[...])).astype(BF16)
    half_gate = 0.5 * mod_ref[0, mod_base + 2:mod_base + 3, :]
    for n in range(D_MODEL // FFN_OUT_CHUNK):
        cols = slice(n * FFN_OUT_CHUNK, (n + 1) * FFN_OUT_CHUNK)
        o_ref[:, cols] += half_gate[:, cols] * _dot(a, wd_ref[:, cols])

    if final:
        @pl.when(j == pl.num_programs(1) - 1)
        def _():
            _row_inv_rms(o_ref, rs_ref)
            fg = fg_ref[...]

            def body(c, carry):
                r = _row_chunk(c)
                o_ref[r, :] = o_ref[r, :] * _lane_tile(rs_ref[r, :], D_MODEL) * fg
                return carry

            lax.fori_loop(0, n_row_chunks, body, 0, unroll=NORM_UNROLL)


def _ffn(x, mod, norm_g, wg, wu, wd, *, mod_base, rows_per_mod, final_g=None, tm=1024, tf=512):
    t = x.shape[0]
    final = final_g is not None
    in_specs = [
        pl.BlockSpec((tm, D_MODEL), lambda i, j: (i, 0), pipeline_mode=pl.Buffered(1)),
        pl.BlockSpec((1, N_MOD, D_MODEL), lambda i, j: (i * tm // rows_per_mod, 0, 0)),
        pl.BlockSpec((1, D_MODEL), lambda i, j: (0, 0)),
        pl.BlockSpec((D_MODEL, tf), lambda i, j: (0, j)),
        pl.BlockSpec((D_MODEL, tf), lambda i, j: (0, j)),
        pl.BlockSpec((tf, D_MODEL), lambda i, j: (j, 0)),
    ]
    args = [x, mod, norm_g, wg, wu, wd]
    if final:
        in_specs.append(pl.BlockSpec((1, D_MODEL), lambda i, j: (0, 0)))
        args.append(final_g)
    return pl.pallas_call(
        functools.partial(_ffn_kernel, mod_base=mod_base, final=final),
        out_shape=jax.ShapeDtypeStruct((t, D_MODEL), F32),
        grid=(t // tm, D_FF // tf),
        in_specs=in_specs,
        out_specs=pl.BlockSpec((tm, D_MODEL), lambda i, j: (i, 0)),
        scratch_shapes=[pltpu.VMEM((tm, D_MODEL), BF16), pltpu.VMEM((tm, LANES), F32)],
        compiler_params=pltpu.CompilerParams(
            dimension_semantics=("parallel", "arbitrary"), vmem_limit_bytes=VMEM_LIMIT),
        name="ffn_final" if final else "ffn",
    )(*args)


def _mixin_kernel(x_ref, mod_ref, g_ref, w_ref, q_ref, k_ref, v_ref, u_ref, vg_ref):
    shift = mod_ref[0, 3:4, :]
    scale = mod_ref[0, 4:5, :]
    h = (_rms(x_ref[...], g_ref[...]) * (1.0 + scale) + shift).astype(BF16)
    for idx, o_ref in enumerate((q_ref, k_ref, v_ref, u_ref, vg_ref)):
        p = _dot(h, w_ref[:, idx * WIDTH_A:(idx + 1) * WIDTH_A])
        if idx == 0:
            p = p * Q_SCALE
        o_ref[...] = p.astype(o_ref.dtype)


def _mixin(x, mod, norm_g, w_in, *, rows_per_mod, kv_dtype, tm=512):
    t = x.shape[0]
    out_dtypes = (BF16, kv_dtype, kv_dtype, BF16, BF16)
    return pl.pallas_call(
        _mixin_kernel,
        out_shape=[jax.ShapeDtypeStruct((t, WIDTH_A), dt) for dt in out_dtypes],
        grid=(t // tm,),
        in_specs=[
            pl.BlockSpec((tm, D_MODEL), lambda i: (i, 0)),
            pl.BlockSpec((1, N_MOD, D_MODEL), lambda i: (i * tm // rows_per_mod, 0, 0)),
            pl.BlockSpec((1, D_MODEL), lambda i: (0, 0)),
            pl.BlockSpec((D_MODEL, IN_WIDTH), lambda i: (0, 0), pipeline_mode=pl.Buffered(1)),
        ],
        out_specs=[pl.BlockSpec((tm, WIDTH_A), lambda i: (i, 0)) for _ in out_dtypes],
        compiler_params=pltpu.CompilerParams(
            dimension_semantics=("parallel",), vmem_limit_bytes=VMEM_LIMIT),
        name="mixin",
    )(x, mod, norm_g, w_in)


def _attend_pair(q, keys, values, bias_fn):
    lane = lax.broadcasted_iota(jnp.int32, (1, LANES), 1)
    out = jnp.zeros(q.shape, F32)
    for hh in range(HEADS_PER_BLOCK):
        head = (lane // HEAD_DIM_A) == hh
        ones_lane = lane == ((hh + 1) % HEADS_PER_BLOCK) * HEAD_DIM_A
        ones_row = jnp.where(ones_lane, 1.0, 0.0).astype(BF16)
        qh = jnp.where(head, q, jnp.zeros_like(q))
        scores = []
        for i, k in enumerate(keys):
            s = _dot_nt(qh, k)
            b = bias_fn(hh, i)
            if b is not None:
                s = s + b
            scores.append(s)
        m = scores[0].max(axis=-1, keepdims=True)
        for s in scores[1:]:
            m = jnp.maximum(m, s.max(axis=-1, keepdims=True))
        acc = jnp.zeros(q.shape, F32)
        for s, v in zip(scores, values):
            v_aug = jnp.where(head, v, jnp.broadcast_to(ones_row, v.shape))
            acc = acc + _dot(jnp.exp2(s - m).astype(BF16), v_aug)
        den = jnp.sum(jnp.where(ones_lane, acc, 0.0), axis=-1, keepdims=True)
        out = out + jnp.where(head, acc / den, 0.0)
    return out


def _pair_loop(pair_fn, pairs_per_iter):
    def body(it, carry):
        for sub in range(pairs_per_iter):
            p = it * pairs_per_iter + sub
            pair_fn(p, pl.ds(pl.multiple_of(p * LANES, LANES), LANES))
        return carry

    lax.fori_loop(0, N_HEAD_BLOCKS // pairs_per_iter, body, 0)


def _ctx_attn_kernel(q_ref, k_ref, v_ref, o_ref):
    def pair(p, cols):
        k = k_ref[:, cols].astype(BF16)
        v = v_ref[:, cols].astype(BF16)
        o = _attend_pair(q_ref[:, cols], [k], [v], lambda hh, i: None)
        o_ref[:, cols] = o.astype(o_ref.dtype)

    _pair_loop(pair, N_HEAD_BLOCKS)


def _ctx_attention(q, k, v, seq):
    t = q.shape[0]
    blk = lambda: pl.BlockSpec((seq, WIDTH_A), lambda b: (b, 0))
    return pl.pallas_call(
        _ctx_attn_kernel,
        out_shape=jax.ShapeDtypeStruct((t, WIDTH_A), BF16),
        grid=(t // seq,),
        in_specs=[blk(), blk(), blk()],
        out_specs=blk(),
        compiler_params=pltpu.CompilerParams(
            dimension_semantics=("parallel",), vmem_limit_bytes=VMEM_LIMIT),
        name="ctx_attn",
    )(q, k, v)


def _nbr_attn_kernel(q_ref, k0_ref, k1_ref, k2_ref, v0_ref, v1_ref, v2_ref,
                     ck_ref, cv_ref, bias_ref, o_ref):
    def pair(p, cols):
        keys = [k0_ref[:, cols], k1_ref[:, cols], k2_ref[:, cols], ck_ref[0, :, cols].astype(BF16)]
        values = [v0_ref[:, cols], v1_ref[:, cols], v2_ref[:, cols], cv_ref[0, :, cols].astype(BF16)]

        def bias_fn(hh, i):
            if i >= WIN_BLOCKS:
                return None
            return bias_ref[0, HEADS_PER_BLOCK * p + hh, :, i * Q_TOK:(i + 1) * Q_TOK]

        o = _attend_pair(q_ref[:, cols], keys, values, bias_fn)
        o_ref[:, cols] = o.astype(o_ref.dtype)

    _pair_loop(pair, PAIRS_PER_ITER)


def _nbr_attention(q, k, v, ck, cv, bias, n_tok):
    t = q.shape[0]
    batch = t // n_tok
    past = ck.shape[1]
    qb = n_tok // Q_TOK
    def win(m):
        return jnp.clip(m - 1, 0, qb - WIN_BLOCKS)
    def variant(m):
        return jnp.minimum(m, 1) + jnp.maximum(m - (qb - 2), 0)
    qspec = pl.BlockSpec((Q_TOK, WIDTH_A), lambda m, b: (b * qb + m, 0))
    def kvspec(w):
        return pl.BlockSpec((Q_TOK, WIDTH_A), lambda m, b: (b * qb + win(m) + w, 0))
    cspec = pl.BlockSpec((1, past, WIDTH_A), lambda m, b: (b, 0, 0))
    bspec = pl.BlockSpec((1, N_HEADS_A, Q_TOK, WIN_BLOCKS * Q_TOK),
                         lambda m, b: (variant(m), 0, 0, 0))
    return pl.pallas_call(
        _nbr_attn_kernel,
        out_shape=jax.ShapeDtypeStruct((t, WIDTH_A), BF16),
        grid=(qb, batch),
        in_specs=[qspec] + [kvspec(w) for w in range(WIN_BLOCKS)]
                 + [kvspec(w) for w in range(WIN_BLOCKS)] + [cspec, cspec, bspec],
        out_specs=qspec,
        compiler_params=pltpu.CompilerParams(
            dimension_semantics=("parallel", "arbitrary"), vmem_limit_bytes=VMEM_LIMIT),
        name="nbr_attn",
    )(q, k, k, k, v, v, v, ck, cv, bias)


def _bias_kernel(rpb_ref, o_ref, *, rows):
    h = pl.program_id(0)
    n_dr = 2 * NA_ROWS - 1
    n_dc = 2 * NA_COLS - 1
    qc = lax.broadcasted_iota(jnp.int32, (GRID_W, LANES), 0)
    ln = lax.broadcasted_iota(jnp.int32, (GRID_W, LANES), 1)
    kc = ln % GRID_W
    dc_idx = jnp.clip(kc - qc + NA_COLS - 1, 0, n_dc - 1)
    col_start = jnp.clip(qc - NA_COLS // 2, 0, GRID_W - NA_COLS)
    col_ok = (kc >= col_start) & (kc < col_start + NA_COLS)
    neg = jnp.full((GRID_W, LANES), NEG_INF, F32)
    toeplitz = []
    for dr in range(n_dr):
        acc = jnp.zeros((GRID_W, LANES), F32)
        for dc in range(n_dc):
            acc = jnp.where(dc_idx == dc, rpb_ref[h * (n_dr * n_dc) + dr * n_dc + dc], acc)
        toeplitz.append(jnp.where(col_ok, acc * LOG2E, neg))
    qb = rows // Q_ROWS
    kr = min(NA_ROWS, rows)
    for var, m in enumerate((0, 1, qb - 1)):
        wb = min(max(m - 1, 0), qb - WIN_BLOCKS)
        for a in range(Q_ROWS):
            r = Q_ROWS * m + a
            row_start = min(max(r - kr // 2, 0), rows - kr)
            for tp in range(WIN_BLOCKS * Q_ROWS // 2):
                halves = []
                for t in (2 * tp, 2 * tp + 1):
                    key_row = Q_ROWS * wb + t
                    ok = row_start <= key_row < row_start + kr
                    halves.append(toeplitz[key_row - r + NA_ROWS - 1] if ok else neg)
                tile = jnp.where(ln < GRID_W, halves[0], halves[1])
                o_ref[var, 0, a * GRID_W:(a + 1) * GRID_W, tp * LANES:(tp + 1) * LANES] = tile


def _bias_table(rpb, rows):
    return pl.pallas_call(
        functools.partial(_bias_kernel, rows=rows),
        out_shape=jax.ShapeDtypeStruct((3, N_HEADS_A, Q_TOK, WIN_BLOCKS * Q_TOK), F32),
        grid=(N_HEADS_A,),
        in_specs=[pl.BlockSpec(memory_space=pltpu.SMEM)],
        out_specs=pl.BlockSpec((3, 1, Q_TOK, WIN_BLOCKS * Q_TOK), lambda h: (0, h, 0, 0)),
        compiler_params=pltpu.CompilerParams(
            dimension_semantics=("arbitrary",), vmem_limit_bytes=VMEM_LIMIT),
        name="nbr_bias",
    )(rpb.reshape(-1))


def _mixout_kernel(x_ref, a_ref, u_ref, vg_ref, mod_ref, gn_ref, ws_ref, bs_ref,
                   na_ref, nb_ref, wo_ref, o_ref, gm_ref):
    tm = x_ref.shape[0]
    for g in range(N_GROUPS_B):
        cols = slice(g * GROUP_DIM_B, (g + 1) * GROUP_DIM_B)
        vn = _rms(_gelu_tanh(vg_ref[:, cols].astype(F32)), gn_ref[:, cols]).astype(BF16)
        ug = _gelu_tanh(u_ref[:, cols].astype(F32))
        w = ws_ref[g]
        b = bs_ref[:, g:g + 1]
        for c in range(tm // CHUNK):
            rows = slice(c * CHUNK, (c + 1) * CHUNK)
            gm_ref[rows, cols] = ug[rows] * (_dot(w, vn[rows]) + b)
    ya = _rms(a_ref[...].astype(F32), na_ref[...]).astype(BF16)
    yb = _rms(gm_ref[...], nb_ref[...]).astype(BF16)
    y = _dot(ya, wo_ref[:WIDTH_A, :]) + _dot(yb, wo_ref[WIDTH_A:, :])
    o_ref[...] = x_ref[...] + mod_ref[0, 5:6, :] * y


def _mixout(x, a, u, vg, mod, gmlp_norm, w_s, b_s_t, na_g, nb_g, w_out, *, rows_per_mod, tm=512):
    t = x.shape[0]
    tok = lambda w: pl.BlockSpec((tm, w), lambda i: (i, 0))
    full = lambda shape: pl.BlockSpec(shape, lambda i: (0,) * len(shape))
    return pl.pallas_call(
        _mixout_kernel,
        out_shape=jax.ShapeDtypeStruct((t, D_MODEL), F32),
        grid=(t // tm,),
        in_specs=[
            tok(D_MODEL), tok(WIDTH_A), tok(WIDTH_B), tok(WIDTH_B),
            pl.BlockSpec((1, N_MOD, D_MODEL), lambda i: (i * tm // rows_per_mod, 0, 0)),
            full((1, WIDTH_B)),
            full((N_GROUPS_B, CHUNK, CHUNK)),
            full((CHUNK, N_GROUPS_B)),
            full((1, WIDTH_A)),
            full((1, WIDTH_B)),
            pl.BlockSpec((MIX_WIDTH, D_MODEL), lambda i: (0, 0), pipeline_mode=pl.Buffered(1)),
        ],
        out_specs=tok(D_MODEL),
        scratch_shapes=[pltpu.VMEM((tm, WIDTH_B), F32)],
        compiler_params=pltpu.CompilerParams(
            dimension_semantics=("parallel",), vmem_limit_bytes=VMEM_LIMIT),
        name="mixout",
    )(x, a, u, vg, mod, gmlp_norm, w_s, b_s_t, na_g, nb_g, w_out)


def kernel(x_prompt, x_sample, cache_k, cache_v, c, c_ctx, w_ada, b_ada, ffn1_norm, ffn1_w_gate, ffn1_w_up, ffn1_w_down, mix_norm, w_in, rpb, gmlp_norm, w_s, b_s, out_norm_a, out_norm_b, w_out, ffn2_norm, ffn2_w_gate, ffn2_w_up, ffn2_w_down, final_norm):
    batch, seq, _ = x_prompt.shape
    dec_batch, dec_seq, _ = x_sample.shape
    depth = w_ada.shape[0]
    x_ctx = x_prompt.reshape(batch * seq, D_MODEL)
    x_lat = x_sample.reshape(dec_batch * dec_seq, D_MODEL)
    final_g = final_norm.reshape(1, D_MODEL)

    cvec = jnp.concatenate([c_ctx[None, :], c], axis=0)
    mod_rows = -(-cvec.shape[0] // 8) * 8
    cvec = jnp.pad(cvec, ((0, mod_rows - cvec.shape[0]), (0, 0)))

    new_k, new_v = [], []
    for l in range(depth):
        last = l == depth - 1
        mod = _adaln(cvec, w_ada[l], b_ada[l][None, :]).reshape(mod_rows, N_MOD, D_MODEL)
        mods = (mod[0:1], mod[1:1 + dec_batch])
        rows_per_mod = (batch * seq, dec_seq)

        f1 = (ffn1_norm[l][None, :], ffn1_w_gate[l].astype(BF16), ffn1_w_up[l].astype(BF16),
              ffn1_w_down[l].astype(BF16))
        f2 = (ffn2_norm[l][None, :], ffn2_w_gate[l].astype(BF16), ffn2_w_up[l].astype(BF16),
              ffn2_w_down[l].astype(BF16))
        mix_g = mix_norm[l][None, :]
        w_in_l = w_in[l].astype(BF16)
        w_out_l = w_out[l].astype(BF16)
        w_s_l = w_s[l].astype(BF16)
        b_s_t = b_s[l].T
        gn = gmlp_norm[l][None, :]
        na_g = out_norm_a[l][None, :]
        nb_g = out_norm_b[l][None, :]
        bias = _bias_table(rpb[l], dec_seq // GRID_W)
        ck = cache_k[:, l].reshape(dec_batch, -1, WIDTH_A)
        cv = cache_v[:, l].reshape(dec_batch, -1, WIDTH_A)

        xs = []
        for path, (x, m, rpm) in enumerate(zip((x_ctx, x_lat), mods, rows_per_mod)):
            x = _ffn(x, m, *f1, mod_base=0, rows_per_mod=rpm)
            q, k, v, u, vg = _mixin(x, m, mix_g, w_in_l, rows_per_mod=rpm,
                                    kv_dtype=F32 if path == 0 else BF16)
            if path == 0:
                a = _ctx_attention(q, k, v, seq)
                new_k.append(k.reshape(batch, seq, N_HEADS_A, HEAD_DIM_A))
                new_v.append(v.reshape(batch, seq, N_HEADS_A, HEAD_DIM_A))
            else:
                a = _nbr_attention(q, k, v, ck, cv, bias, dec_seq)
            x = _mixout(x, a, u, vg, m, gn, w_s_l, b_s_t, na_g, nb_g, w_out_l, rows_per_mod=rpm)
            x = _ffn(x, m, *f2, mod_base=6, rows_per_mod=rpm,
                     final_g=final_g if last else None)
            xs.append(x)
        x_ctx, x_lat = xs

    if depth == 0:
        raise ValueError("depth must be positive")
    y_prompt = x_ctx.reshape(batch, seq, D_MODEL)
    y_sample = x_lat.reshape(dec_batch, dec_seq, D_MODEL)
    return (y_prompt, y_sample, jnp.stack(new_k, axis=1), jnp.stack(new_v, axis=1))
```

```python
import functools
import math

import jax
import jax.numpy as jnp
from jax import lax
from jax.experimental import pallas as pl
from jax.experimental.pallas import tpu as pltpu

D_MODEL = 2048
N_HEADS_A = 16
HEAD_DIM_A = 64
WIDTH_A = N_HEADS_A * HEAD_DIM_A
GRID_W = 64
NA_ROWS = 8
NA_COLS = 16
N_GROUPS_B = 8
GROUP_DIM_B = 128
WIDTH_B = N_GROUPS_B * GROUP_DIM_B
CHUNK = 128
MIX_WIDTH = WIDTH_A + WIDTH_B
IN_WIDTH = 3 * WIDTH_A + 2 * WIDTH_B
D_FF = 5632
N_MOD = 9
EPS = 1e-6
NEG_INF = -1e30

LANES = 128
HEADS_PER_BLOCK = LANES // HEAD_DIM_A
N_HEAD_BLOCKS = N_HEADS_A // HEADS_PER_BLOCK
Q_ROWS = 4
Q_TOK = Q_ROWS * GRID_W
WIN_BLOCKS = 3
VMEM_LIMIT = 60 * 1024 * 1024
ROW_CHUNK = 16
NORM_UNROLL = 8
FFN_OUT_CHUNK = 512
PAIRS_PER_ITER = 4
LOG2E = math.log2(math.e)
Q_SCALE = HEAD_DIM_A ** -0.5 * LOG2E

F32 = jnp.float32
BF16 = jnp.bfloat16


def _dot(a, b):
    return jnp.dot(a, b, preferred_element_type=F32)


def _dot_nt(a, b):
    return lax.dot_general(a, b, (((1,), (1,)), ((), ())), preferred_element_type=F32)


def _rms(x, g):
    ms = jnp.mean(x * x, axis=-1, keepdims=True)
    return x * lax.rsqrt(ms + EPS) * g


def _silu(x):
    return x * jax.nn.sigmoid(x)


def _gelu_tanh(x):
    c = 0.7978845608028654
    return 0.5 * x * (1.0 + jnp.tanh(c * (x + 0.044715 * (x * x * x))))


def _row_chunk(c):
    return pl.ds(pl.multiple_of(c * ROW_CHUNK, ROW_CHUNK), ROW_CHUNK)


def _lane_tile(v, width):
    return jnp.concatenate([v] * (width // LANES), axis=1)


def _row_inv_rms(x_ref, rs_ref):
    def body(c, carry):
        r = _row_chunk(c)
        x = x_ref[r, :]
        ms = jnp.mean(x * x, axis=-1, keepdims=True)
        rs_ref[r, :] = jnp.broadcast_to(lax.rsqrt(ms + EPS), (ROW_CHUNK, LANES))
        return carry

    lax.fori_loop(0, x_ref.shape[0] // ROW_CHUNK, body, 0, unroll=NORM_UNROLL)


def _adaln_kernel(c_ref, w_ref, b_ref, o_ref):
    s = _silu(c_ref[...]).astype(BF16)
    o_ref[...] = _dot(s, w_ref[...].astype(BF16)) + b_ref[...]


def _adaln(cvec, w_ada, b_ada, tn=1024):
    rows = cvec.shape[0]
    n = w_ada.shape[1]
    return pl.pallas_call(
        _adaln_kernel,
        out_shape=jax.ShapeDtypeStruct((rows, n), F32),
        grid=(n // tn,),
        in_specs=[
            pl.BlockSpec((rows, D_MODEL), lambda j: (0, 0)),
            pl.BlockSpec((D_MODEL, tn), lambda j: (0, j)),
            pl.BlockSpec((1, tn), lambda j: (0, j)),
        ],
        out_specs=pl.BlockSpec((rows, tn), lambda j: (0, j)),
        compiler_params=pltpu.CompilerParams(
            dimension_semantics=("arbitrary",), vmem_limit_bytes=VMEM_LIMIT),
        name="adaln",
    )(cvec, w_ada, b_ada)


def _ffn_kernel(*refs, mod_base, final):
    if final:
        x_ref, mod_ref, g_ref, wg_ref, wu_ref, wd_ref, fg_ref, o_ref, h_ref, rs_ref = refs
    else:
        x_ref, mod_ref, g_ref, wg_ref, wu_ref, wd_ref, o_ref, h_ref, rs_ref = refs
    j = pl.program_id(1)
    n_row_chunks = x_ref.shape[0] // ROW_CHUNK

    @pl.when(j == 0)
    def _():
        _row_inv_rms(x_ref, rs_ref)
        shift = mod_ref[0, mod_base:mod_base + 1, :]
        gain = g_ref[...] * (1.0 + mod_ref[0, mod_base + 1:mod_base + 2, :])

        def body(c, carry):
            r = _row_chunk(c)
            x = x_ref[r, :]
            h_ref[r, :] = (x * _lane_tile(rs_ref[r, :], D_MODEL) * gain + shift).astype(BF16)
            o_ref[r, :] = x
            return carry

        lax.fori_loop(0, n_row_chunks, body, 0, unroll=NORM_UNROLL)

    h = h_ref[...]
    a = (_silu(_dot(h, wg_ref[...])) * _dot(h, wu_ref[...])).astype(BF16)
    half_gate = 0.5 * mod_ref[0, mod_base + 2:mod_base + 3, :]
    for n in range(D_MODEL // FFN_OUT_CHUNK):
        cols = slice(n * FFN_OUT_CHUNK, (n + 1) * FFN_OUT_CHUNK)
        o_ref[:, cols] += half_gate[:, cols] * _dot(a, wd_ref[:, cols])

    if final:
        @pl.when(j == pl.num_programs(1) - 1)
        def _():
            _row_inv_rms(o_ref, rs_ref)
            fg = fg_ref[...]

            def body(c, carry):
                r = _row_chunk(c)
                o_ref[r, :] = o_ref[r, :] * _lane_tile(rs_ref[r, :], D_MODEL) * fg
                return carry

            lax.fori_loop(0, n_row_chunks, body, 0, unroll=NORM_UNROLL)


def _ffn(x, mod, norm_g, wg, wu, wd, *, mod_base, rows_per_mod, final_g=None, tm=1024, tf=512):
    t = x.shape[0]
    final = final_g is not None
    in_specs = [
        pl.BlockSpec((tm, D_MODEL), lambda i, j: (i, 0)),
        pl.BlockSpec((1, N_MOD, D_MODEL), lambda i, j: (i * tm // rows_per_mod, 0, 0)),
        pl.BlockSpec((1, D_MODEL), lambda i, j: (0, 0)),
        pl.BlockSpec((D_MODEL, tf), lambda i, j: (0, j)),
        pl.BlockSpec((D_MODEL, tf), lambda i, j: (0, j)),
        pl.BlockSpec((tf, D_MODEL), lambda i, j: (j, 0)),
    ]
    args = [x, mod, norm_g, wg, wu, wd]
    if final:
        in_specs.append(pl.BlockSpec((1, D_MODEL), lambda i, j: (0, 0)))
        args.append(final_g)
    return pl.pallas_call(
        functools.partial(_ffn_kernel, mod_base=mod_base, final=final),
        out_shape=jax.ShapeDtypeStruct((t, D_MODEL), F32),
        grid=(t // tm, D_FF // tf),
        in_specs=in_specs,
        out_specs=pl.BlockSpec((tm, D_MODEL), lambda i, j: (i, 0)),
        scratch_shapes=[pltpu.VMEM((tm, D_MODEL), BF16), pltpu.VMEM((tm, LANES), F32)],
        compiler_params=pltpu.CompilerParams(
            dimension_semantics=("parallel", "arbitrary"), vmem_limit_bytes=VMEM_LIMIT),
        name="ffn_final" if final else "ffn",
    )(*args)


def _mixin_kernel(x_ref, mod_ref, g_ref, w_ref, q_ref, k_ref, v_ref, u_ref, vg_ref):
    shift = mod_ref[0, 3:4, :]
    scale = mod_ref[0, 4:5, :]
    h = (_rms(x_ref[...], g_ref[...]) * (1.0 + scale) + shift).astype(BF16)
    for idx, o_ref in enumerate((q_ref, k_ref, v_ref, u_ref, vg_ref)):
        p = _dot(h, w_ref[:, idx * WIDTH_A:(idx + 1) * WIDTH_A])
        if idx == 0:
            p = p * Q_SCALE
        o_ref[...] = p.astype(o_ref.dtype)


def _mixin(x, mod, norm_g, w_in, *, rows_per_mod, kv_dtype, tm=512):
    t = x.shape[0]
    out_dtypes = (BF16, kv_dtype, kv_dtype, BF16, BF16)
    return pl.pallas_call(
        _mixin_kernel,
        out_shape=[jax.ShapeDtypeStruct((t, WIDTH_A), dt) for dt in out_dtypes],
        grid=(t // tm,),
        in_specs=[
            pl.BlockSpec((tm, D_MODEL), lambda i: (i, 0)),
            pl.BlockSpec((1, N_MOD, D_MODEL), lambda i: (i * tm // rows_per_mod, 0, 0)),
            pl.BlockSpec((1, D_MODEL), lambda i: (0, 0)),
            pl.BlockSpec((D_MODEL, IN_WIDTH), lambda i: (0, 0), pipeline_mode=pl.Buffered(1)),
        ],
        out_specs=[pl.BlockSpec((tm, WIDTH_A), lambda i: (i, 0)) for _ in out_dtypes],
        compiler_params=pltpu.CompilerParams(
            dimension_semantics=("parallel",), vmem_limit_bytes=VMEM_LIMIT),
        name="mixin",
    )(x, mod, norm_g, w_in)


def _attend_pair(q, keys, values, bias_fn):
    lane = lax.broadcasted_iota(jnp.int32, (1, LANES), 1)
    out = jnp.zeros(q.shape, F32)
    for hh in range(HEADS_PER_BLOCK):
        head = (lane // HEAD_DIM_A) == hh
        ones_lane = lane == ((hh + 1) % HEADS_PER_BLOCK) * HEAD_DIM_A
        ones_row = jnp.where(ones_lane, 1.0, 0.0).astype(BF16)
        qh = jnp.where(head, q, jnp.zeros_like(q))
        scores = []
        for i, k in enumerate(keys):
            s = _dot_nt(qh, k)
            b = bias_fn(hh, i)
            if b is not None:
                s = s + b
            scores.append(s)
        m = scores[0].max(axis=-1, keepdims=True)
        for s in scores[1:]:
            m = jnp.maximum(m, s.max(axis=-1, keepdims=True))
        acc = jnp.zeros(q.shape, F32)
        for s, v in zip(scores, values):
            v_aug = jnp.where(head, v, jnp.broadcast_to(ones_row, v.shape))
            acc = acc + _dot(jnp.exp2(s - m).astype(BF16), v_aug)
        den = jnp.sum(jnp.where(ones_lane, acc, 0.0), axis=-1, keepdims=True)
        out = out + jnp.where(head, acc / den, 0.0)
    return out


def _pair_loop(pair_fn, pairs_per_iter):
    def body(it, carry):
        for sub in range(pairs_per_iter):
            p = it * pairs_per_iter + sub
            pair_fn(p, pl.ds(pl.multiple_of(p * LANES, LANES), LANES))
        return carry

    lax.fori_loop(0, N_HEAD_BLOCKS // pairs_per_iter, body, 0)


def _ctx_attn_kernel(q_ref, k_ref, v_ref, o_ref):
    def pair(p, cols):
        k = k_ref[:, cols].astype(BF16)
        v = v_ref[:, cols].astype(BF16)
        o = _attend_pair(q_ref[:, cols], [k], [v], lambda hh, i: None)
        o_ref[:, cols] = o.astype(o_ref.dtype)

    _pair_loop(pair, N_HEAD_BLOCKS)


def _ctx_attention(q, k, v, seq):
    t = q.shape[0]
    blk = lambda: pl.BlockSpec((seq, WIDTH_A), lambda b: (b, 0))
    return pl.pallas_call(
        _ctx_attn_kernel,
        out_shape=jax.ShapeDtypeStruct((t, WIDTH_A), BF16),
        grid=(t // seq,),
        in_specs=[blk(), blk(), blk()],
        out_specs=blk(),
        compiler_params=pltpu.CompilerParams(
            dimension_semantics=("parallel",), vmem_limit_bytes=VMEM_LIMIT),
        name="ctx_attn",
    )(q, k, v)


def _nbr_attn_kernel(q_ref, k0_ref, k1_ref, k2_ref, v0_ref, v1_ref, v2_ref,
                     ck_ref, cv_ref, bias_ref, o_ref):
    def pair(p, cols):
        keys = [k0_ref[:, cols], k1_ref[:, cols], k2_ref[:, cols], ck_ref[0, :, cols].astype(BF16)]
        values = [v0_ref[:, cols], v1_ref[:, cols], v2_ref[:, cols], cv_ref[0, :, cols].astype(BF16)]

        def bias_fn(hh, i):
            if i >= WIN_BLOCKS:
                return None
            return bias_ref[0, HEADS_PER_BLOCK * p + hh, :, i * Q_TOK:(i + 1) * Q_TOK]

        o = _attend_pair(q_ref[:, cols], keys, values, bias_fn)
        o_ref[:, cols] = o.astype(o_ref.dtype)

    _pair_loop(pair, PAIRS_PER_ITER)


def _nbr_attention(q, k, v, ck, cv, bias, n_tok):
    t = q.shape[0]
    batch = t // n_tok
    past = ck.shape[1]
    qb = n_tok // Q_TOK
    def win(m):
        return jnp.clip(m - 1, 0, qb - WIN_BLOCKS)
    def variant(m):
        return jnp.minimum(m, 1) + jnp.maximum(m - (qb - 2), 0)
    qspec = pl.BlockSpec((Q_TOK, WIDTH_A), lambda m, b: (b * qb + m, 0))
    def kvspec(w):
        return pl.BlockSpec((Q_TOK, WIDTH_A), lambda m, b: (b * qb + win(m) + w, 0))
    cspec = pl.BlockSpec((1, past, WIDTH_A), lambda m, b: (b, 0, 0))
    bspec = pl.BlockSpec((1, N_HEADS_A, Q_TOK, WIN_BLOCKS * Q_TOK),
                         lambda m, b: (variant(m), 0, 0, 0))
    return pl.pallas_call(
        _nbr_attn_kernel,
        out_shape=jax.ShapeDtypeStruct((t, WIDTH_A), BF16),
        grid=(qb, batch),
        in_specs=[qspec] + [kvspec(w) for w in range(WIN_BLOCKS)]
                 + [kvspec(w) for w in range(WIN_BLOCKS)] + [cspec, cspec, bspec],
        out_specs=qspec,
        compiler_params=pltpu.CompilerParams(
            dimension_semantics=("parallel", "arbitrary"), vmem_limit_bytes=VMEM_LIMIT),
        name="nbr_attn",
    )(q, k, k, k, v, v, v, ck, cv, bias)


def _bias_kernel(rpb_ref, o_ref, *, rows):
    h = pl.program_id(0)
    n_dr = 2 * NA_ROWS - 1
    n_dc = 2 * NA_COLS - 1
    qc = lax.broadcasted_iota(jnp.int32, (GRID_W, LANES), 0)
    ln = lax.broadcasted_iota(jnp.int32, (GRID_W, LANES), 1)
    kc = ln % GRID_W
    dc_idx = jnp.clip(kc - qc + NA_COLS - 1, 0, n_dc - 1)
    col_start = jnp.clip(qc - NA_COLS // 2, 0, GRID_W - NA_COLS)
    col_ok = (kc >= col_start) & (kc < col_start + NA_COLS)
    neg = jnp.full((GRID_W, LANES), NEG_INF, F32)
    toeplitz = []
    for dr in range(n_dr):
        acc = jnp.zeros((GRID_W, LANES), F32)
        for dc in range(n_dc):
            acc = jnp.where(dc_idx == dc, rpb_ref[h * (n_dr * n_dc) + dr * n_dc + dc], acc)
        toeplitz.append(jnp.where(col_ok, acc * LOG2E, neg))
    qb = rows // Q_ROWS
    kr = min(NA_ROWS, rows)
    for var, m in enumerate((0, 1, qb - 1)):
        wb = min(max(m - 1, 0), qb - WIN_BLOCKS)
        for a in range(Q_ROWS):
            r = Q_ROWS * m + a
            row_start = min(max(r - kr // 2, 0), rows - kr)
            for tp in range(WIN_BLOCKS * Q_ROWS // 2):
                halves = []
                for t in (2 * tp, 2 * tp + 1):
                    key_row = Q_ROWS * wb + t
                    ok = row_start <= key_row < row_start + kr
                    halves.append(toeplitz[key_row - r + NA_ROWS - 1] if ok else neg)
                tile = jnp.where(ln < GRID_W, halves[0], halves[1])
                o_ref[var, 0, a * GRID_W:(a + 1) * GRID_W, tp * LANES:(tp + 1) * LANES] = tile


def _bias_table(rpb, rows):
    return pl.pallas_call(
        functools.partial(_bias_kernel, rows=rows),
        out_shape=jax.ShapeDtypeStruct((3, N_HEADS_A, Q_TOK, WIN_BLOCKS * Q_TOK), F32),
        grid=(N_HEADS_A,),
        in_specs=[pl.BlockSpec(memory_space=pltpu.SMEM)],
        out_specs=pl.BlockSpec((3, 1, Q_TOK, WIN_BLOCKS * Q_TOK), lambda h: (0, h, 0, 0)),
        compiler_params=pltpu.CompilerParams(
            dimension_semantics=("arbitrary",), vmem_limit_bytes=VMEM_LIMIT),
        name="nbr_bias",
    )(rpb.reshape(-1))


def _mixout_kernel(x_ref, a_ref, u_ref, vg_ref, mod_ref, gn_ref, ws_ref, bs_ref,
                   na_ref, nb_ref, wo_ref, o_ref, gm_ref):
    tm = x_ref.shape[0]
    for g in range(N_GROUPS_B):
        cols = slice(g * GROUP_DIM_B, (g + 1) * GROUP_DIM_B)
        vn = _rms(_gelu_tanh(vg_ref[:, cols].astype(F32)), gn_ref[:, cols]).astype(BF16)
        ug = _gelu_tanh(u_ref[:, cols].astype(F32))
        w = ws_ref[g]
        b = bs_ref[:, g:g + 1]
        for c in range(tm // CHUNK):
            rows = slice(c * CHUNK, (c + 1) * CHUNK)
            gm_ref[rows, cols] = ug[rows] * (_dot(w, vn[rows]) + b)
    ya = _rms(a_ref[...].astype(F32), na_ref[...]).astype(BF16)
    yb = _rms(gm_ref[...], nb_ref[...]).astype(BF16)
    y = _dot(ya, wo_ref[:WIDTH_A, :]) + _dot(yb, wo_ref[WIDTH_A:, :])
    o_ref[...] = x_ref[...] + mod_ref[0, 5:6, :] * y


def _mixout(x, a, u, vg, mod, gmlp_norm, w_s, b_s_t, na_g, nb_g, w_out, *, rows_per_mod, tm=512):
    t = x.shape[0]
    tok = lambda w: pl.BlockSpec((tm, w), lambda i: (i, 0))
    full = lambda shape: pl.BlockSpec(shape, lambda i: (0,) * len(shape))
    return pl.pallas_call(
        _mixout_kernel,
        out_shape=jax.ShapeDtypeStruct((t, D_MODEL), F32),
        grid=(t // tm,),
        in_specs=[
            tok(D_MODEL), tok(WIDTH_A), tok(WIDTH_B), tok(WIDTH_B),
            pl.BlockSpec((1, N_MOD, D_MODEL), lambda i: (i * tm // rows_per_mod, 0, 0)),
            full((1, WIDTH_B)),
            full((N_GROUPS_B, CHUNK, CHUNK)),
            full((CHUNK, N_GROUPS_B)),
            full((1, WIDTH_A)),
            full((1, WIDTH_B)),
            pl.BlockSpec((MIX_WIDTH, D_MODEL), lambda i: (0, 0), pipeline_mode=pl.Buffered(1)),
        ],
        out_specs=tok(D_MODEL),
        scratch_shapes=[pltpu.VMEM((tm, WIDTH_B), F32)],
        compiler_params=pltpu.CompilerParams(
            dimension_semantics=("parallel",), vmem_limit_bytes=VMEM_LIMIT),
        name="mixout",
    )(x, a, u, vg, mod, gmlp_norm, w_s, b_s_t, na_g, nb_g, w_out)


def kernel(x_prompt, x_sample, cache_k, cache_v, c, c_ctx, w_ada, b_ada, ffn1_norm, ffn1_w_gate, ffn1_w_up, ffn1_w_down, mix_norm, w_in, rpb, gmlp_norm, w_s, b_s, out_norm_a, out_norm_b, w_out, ffn2_norm, ffn2_w_gate, ffn2_w_up, ffn2_w_down, final_norm):
    batch, seq, _ = x_prompt.shape
    dec_batch, dec_seq, _ = x_sample.shape
    depth = w_ada.shape[0]
    x_ctx = x_prompt.reshape(batch * seq, D_MODEL)
    x_lat = x_sample.reshape(dec_batch * dec_seq, D_MODEL)
    final_g = final_norm.reshape(1, D_MODEL)

    cvec = jnp.concatenate([c_ctx[None, :], c], axis=0)
    mod_rows = -(-cvec.shape[0] // 8) * 8
    cvec = jnp.pad(cvec, ((0, mod_rows - cvec.shape[0]), (0, 0)))

    new_k, new_v = [], []
    for l in range(depth):
        last = l == depth - 1
        mod = _adaln(cvec, w_ada[l], b_ada[l][None, :]).reshape(mod_rows, N_MOD, D_MODEL)
        mods = (mod[0:1], mod[1:1 + dec_batch])
        rows_per_mod = (batch * seq, dec_seq)

        f1 = (ffn1_norm[l][None, :], ffn1_w_gate[l].astype(BF16), ffn1_w_up[l].astype(BF16),
              ffn1_w_down[l].astype(BF16))
        f2 = (ffn2_norm[l][None, :], ffn2_w_gate[l].astype(BF16), ffn2_w_up[l].astype(BF16),
              ffn2_w_down[l].astype(BF16))
        mix_g = mix_norm[l][None, :]
        w_in_l = w_in[l].astype(BF16)
        w_out_l = w_out[l].astype(BF16)
        w_s_l = w_s[l].astype(BF16)
        b_s_t = b_s[l].T
        gn = gmlp_norm[l][None, :]
        na_g = out_norm_a[l][None, :]
        nb_g = out_norm_b[l][None, :]
        bias = _bias_table(rpb[l], dec_seq // GRID_W)
        ck = cache_k[:, l].reshape(dec_batch, -1, WIDTH_A)
        cv = cache_v[:, l].reshape(dec_batch, -1, WIDTH_A)

        xs = []
        for path, (x, m, rpm) in enumerate(zip((x_ctx, x_lat), mods, rows_per_mod)):
            x = _ffn(x, m, *f1, mod_base=0, rows_per_mod=rpm)
            q, k, v, u, vg = _mixin(x, m, mix_g, w_in_l, rows_per_mod=rpm,
                                    kv_dtype=F32 if path == 0 else BF16)
            if path == 0:
                a = _ctx_attention(q, k, v, seq)
                new_k.append(k.reshape(batch, seq, N_HEADS_A, HEAD_DIM_A))
                new_v.append(v.reshape(batch, seq, N_HEADS_A, HEAD_DIM_A))
            else:
                a = _nbr_attention(q, k, v, ck, cv, bias, dec_seq)
            x = _mixout(x, a, u, vg, m, gn, w_s_l, b_s_t, na_g, nb_g, w_out_l, rows_per_mod=rpm)
            x = _ffn(x, m, *f2, mod_base=6, rows_per_mod=rpm,
                     final_g=final_g if last else None)
            xs.append(x)
        x_ctx, x_lat = xs

    if depth == 0:
        raise ValueError("depth must be positive")
    y_prompt = x_ctx.reshape(batch, seq, D_MODEL)
    y_sample = x_lat.reshape(dec_batch, dec_seq, D_MODEL)
    return (y_prompt, y_sample, jnp.stack(new_k, axis=1), jnp.stack(new_v, axis=1))
```

```python
import functools
import math

import jax
import jax.numpy as jnp
from jax import lax
from jax.experimental import pallas as pl
from jax.experimental.pallas import tpu as pltpu

D_MODEL = 2048
N_HEADS_A = 16
HEAD_DIM_A = 64
WIDTH_A = N_HEADS_A * HEAD_DIM_A
GRID_W = 64
NA_ROWS = 8
NA_COLS = 16
N_GROUPS_B = 8
GROUP_DIM_B = 128
WIDTH_B = N_GROUPS_B * GROUP_DIM_B
CHUNK = 128
MIX_WIDTH = WIDTH_A + WIDTH_B
IN_WIDTH = 3 * WIDTH_A + 2 * WIDTH_B
D_FF = 5632
N_MOD = 9
EPS = 1e-6
NEG_INF = -1e30

LANES = 128
HEADS_PER_BLOCK = LANES // HEAD_DIM_A
N_HEAD_BLOCKS = N_HEADS_A // HEADS_PER_BLOCK
Q_ROWS = 4
Q_TOK = Q_ROWS * GRID_W
WIN_BLOCKS = 3
VMEM_LIMIT = 62 * 1024 * 1024
ROW_CHUNK = 16
NORM_UNROLL = 8
FFN_OUT_CHUNK = 512
FFN_TM = 1024
FFN_TF = 512
PAIRS_PER_ITER = 4
LOG2E = math.log2(math.e)
Q_SCALE = HEAD_DIM_A ** -0.5 * LOG2E

F32 = jnp.float32
BF16 = jnp.bfloat16


def _dot(a, b):
    return jnp.dot(a, b, preferred_element_type=F32)


def _dot_nt(a, b):
    return lax.dot_general(a, b, (((1,), (1,)), ((), ())), preferred_element_type=F32)


def _rms(x, g):
    ms = jnp.mean(x * x, axis=-1, keepdims=True)
    return x * lax.rsqrt(ms + EPS) * g


def _silu(x):
    return x * jax.nn.sigmoid(x)


def _gelu_tanh(x):
    c = 0.7978845608028654
    return 0.5 * x * (1.0 + jnp.tanh(c * (x + 0.044715 * (x * x * x))))


def _row_chunk(c):
    return pl.ds(pl.multiple_of(c * ROW_CHUNK, ROW_CHUNK), ROW_CHUNK)


def _lane_tile(v, width):
    return jnp.concatenate([v] * (width // LANES), axis=1)


def _row_inv_rms(x_ref, rs_ref):
    def body(c, carry):
        r = _row_chunk(c)
        x = x_ref[r, :]
        ms = jnp.mean(x * x, axis=-1, keepdims=True)
        rs_ref[r, :] = jnp.broadcast_to(lax.rsqrt(ms + EPS), (ROW_CHUNK, LANES))
        return carry

    lax.fori_loop(0, x_ref.shape[0] // ROW_CHUNK, body, 0, unroll=NORM_UNROLL)


def _adaln_kernel(c_ref, w_ref, b_ref, o_ref):
    s = _silu(c_ref[...]).astype(BF16)
    o_ref[...] = _dot(s, w_ref[...].astype(BF16)) + b_ref[...]


def _adaln(cvec, w_ada, b_ada, tn=1024):
    rows = cvec.shape[0]
    n = w_ada.shape[1]
    return pl.pallas_call(
        _adaln_kernel,
        out_shape=jax.ShapeDtypeStruct((rows, n), F32),
        grid=(n // tn,),
        in_specs=[
            pl.BlockSpec((rows, D_MODEL), lambda j: (0, 0)),
            pl.BlockSpec((D_MODEL, tn), lambda j: (0, j)),
            pl.BlockSpec((1, tn), lambda j: (0, j)),
        ],
        out_specs=pl.BlockSpec((rows, tn), lambda j: (0, j)),
        compiler_params=pltpu.CompilerParams(
            dimension_semantics=("arbitrary",), vmem_limit_bytes=VMEM_LIMIT),
        name="adaln",
    )(cvec, w_ada, b_ada)


def _ffn_kernel(*refs, mod_base, final, n_side):
    x_ref, mod_ref, g_ref, wg_ref, wu_ref, wd_ref = refs[:6]
    n_in = 7 if final else 6
    fg_ref = refs[6] if final else None
    side_in = refs[n_in:n_in + n_side]
    o_ref = refs[n_in + n_side]
    side_out = refs[n_in + n_side + 1:n_in + 2 * n_side + 1]
    h_ref, rs_ref = refs[n_in + 2 * n_side + 1:]
    j = pl.program_id(1)
    n_row_chunks = x_ref.shape[0] // ROW_CHUNK

    @pl.when(j == 0)
    def _():
        _row_inv_rms(x_ref, rs_ref)
        shift = mod_ref[0, mod_base:mod_base + 1, :]
        gain = g_ref[...] * (1.0 + mod_ref[0, mod_base + 1:mod_base + 2, :])

        def body(c, carry):
            r = _row_chunk(c)
            x = x_ref[r, :]
            h_ref[r, :] = (x * _lane_tile(rs_ref[r, :], D_MODEL) * gain + shift).astype(BF16)
            o_ref[r, :] = x
            return carry

        lax.fori_loop(0, n_row_chunks, body, 0, unroll=NORM_UNROLL)

    for src_ref, dst_ref in zip(side_in, side_out):
        dst_ref[...] = src_ref[...].astype(BF16)

    h = h_ref[...]
    a = (_silu(_dot(h, wg_ref[...])) * _dot(h, wu_ref[...])).astype(BF16)
    half_gate = 0.5 * mod_ref[0, mod_base + 2:mod_base + 3, :]
    for n in range(D_MODEL // FFN_OUT_CHUNK):
        cols = slice(n * FFN_OUT_CHUNK, (n + 1) * FFN_OUT_CHUNK)
        o_ref[:, cols] += half_gate[:, cols] * _dot(a, wd_ref[:, cols])

    if final:
        @pl.when(j == pl.num_programs(1) - 1)
        def _():
            _row_inv_rms(o_ref, rs_ref)
            fg = fg_ref[...]

            def body(c, carry):
                r = _row_chunk(c)
                o_ref[r, :] = o_ref[r, :] * _lane_tile(rs_ref[r, :], D_MODEL) * fg
                return carry

            lax.fori_loop(0, n_row_chunks, body, 0, unroll=NORM_UNROLL)


def _cast_plan(w, n_i, n_j):
    rows, cols = w.shape
    col_blocks = max(d for d in range(1, n_j + 1) if cols % d == 0 and (cols // d) % LANES == 0)
    assert rows % n_i == 0 and (rows // n_i) % 16 == 0
    block = (rows // n_i, cols // col_blocks)
    return w, block, lambda i, j: (i, jnp.minimum(j, col_blocks - 1))


def _cast_plan_t(w, n_i, n_j):
    rows, cols = w.shape
    assert rows % n_j == 0 and (rows // n_j) % 16 == 0 and cols % n_i == 0 and (cols // n_i) % LANES == 0
    return w, (rows // n_j, cols // n_i), lambda i, j: (j, i)


def _ffn(x, mod, norm_g, wg, wu, wd, *, mod_base, rows_per_mod, final_g=None, side=(),
         tm=FFN_TM, tf=FFN_TF):
    t = x.shape[0]
    final = final_g is not None
    in_specs = [
        pl.BlockSpec((tm, D_MODEL), lambda i, j: (i, 0)),
        pl.BlockSpec((1, N_MOD, D_MODEL), lambda i, j: (i * tm // rows_per_mod, 0, 0)),
        pl.BlockSpec((1, D_MODEL), lambda i, j: (0, 0)),
        pl.BlockSpec((D_MODEL, tf), lambda i, j: (0, j)),
        pl.BlockSpec((D_MODEL, tf), lambda i, j: (0, j)),
        pl.BlockSpec((tf, D_MODEL), lambda i, j: (j, 0)),
    ]
    args = [x, mod, norm_g, wg, wu, wd]
    if final:
        in_specs.append(pl.BlockSpec((1, D_MODEL), lambda i, j: (0, 0)))
        args.append(final_g)
    out_shape = [jax.ShapeDtypeStruct((t, D_MODEL), F32)]
    out_specs = [pl.BlockSpec((tm, D_MODEL), lambda i, j: (i, 0))]
    for w, block, index_map in side:
        in_specs.append(pl.BlockSpec(block, index_map))
        args.append(w)
        out_shape.append(jax.ShapeDtypeStruct(w.shape, BF16))
        out_specs.append(pl.BlockSpec(block, index_map))
    return pl.pallas_call(
        functools.partial(_ffn_kernel, mod_base=mod_base, final=final, n_side=len(side)),
        out_shape=out_shape,
        grid=(t // tm, D_FF // tf),
        in_specs=in_specs,
        out_specs=out_specs,
        scratch_shapes=[pltpu.VMEM((tm, D_MODEL), BF16), pltpu.VMEM((tm, LANES), F32)],
        compiler_params=pltpu.CompilerParams(
            dimension_semantics=("arbitrary", "arbitrary"), vmem_limit_bytes=VMEM_LIMIT),
        name="ffn_final" if final else "ffn",
    )(*args)


def _mixin_kernel(x_ref, mod_ref, g_ref, w_ref, q_ref, k_ref, v_ref, u_ref, vg_ref):
    shift = mod_ref[0, 3:4, :]
    scale = mod_ref[0, 4:5, :]
    h = (_rms(x_ref[...], g_ref[...]) * (1.0 + scale) + shift).astype(BF16)
    for idx, o_ref in enumerate((q_ref, k_ref, v_ref, u_ref, vg_ref)):
        p = _dot(h, w_ref[:, idx * WIDTH_A:(idx + 1) * WIDTH_A])
        if idx == 0:
            p = p * Q_SCALE
        o_ref[...] = p.astype(o_ref.dtype)


def _mixin(x, mod, norm_g, w_in, *, rows_per_mod, kv_dtype, tm=512):
    t = x.shape[0]
    out_dtypes = (BF16, kv_dtype, kv_dtype, BF16, BF16)
    return pl.pallas_call(
        _mixin_kernel,
        out_shape=[jax.ShapeDtypeStruct((t, WIDTH_A), dt) for dt in out_dtypes],
        grid=(t // tm,),
        in_specs=[
            pl.BlockSpec((tm, D_MODEL), lambda i: (i, 0)),
            pl.BlockSpec((1, N_MOD, D_MODEL), lambda i: (i * tm // rows_per_mod, 0, 0)),
            pl.BlockSpec((1, D_MODEL), lambda i: (0, 0)),
            pl.BlockSpec((D_MODEL, IN_WIDTH), lambda i: (0, 0), pipeline_mode=pl.Buffered(1)),
        ],
        out_specs=[pl.BlockSpec((tm, WIDTH_A), lambda i: (i, 0)) for _ in out_dtypes],
        compiler_params=pltpu.CompilerParams(
            dimension_semantics=("parallel",), vmem_limit_bytes=VMEM_LIMIT),
        name="mixin",
    )(x, mod, norm_g, w_in)


def _attend_pair(q, keys, values, bias_fn):
    lane = lax.broadcasted_iota(jnp.int32, (1, LANES), 1)
    out = jnp.zeros(q.shape, F32)
    for hh in range(HEADS_PER_BLOCK):
        head = (lane // HEAD_DIM_A) == hh
        ones_lane = lane == ((hh + 1) % HEADS_PER_BLOCK) * HEAD_DIM_A
        ones_row = jnp.where(ones_lane, 1.0, 0.0).astype(BF16)
        qh = jnp.where(head, q, jnp.zeros_like(q))
        scores = []
        for i, k in enumerate(keys):
            s = _dot_nt(qh, k)
            b = bias_fn(hh, i)
            if b is not None:
                s = s + b
            scores.append(s)
        m = scores[0].max(axis=-1, keepdims=True)
        for s in scores[1:]:
            m = jnp.maximum(m, s.max(axis=-1, keepdims=True))
        acc = jnp.zeros(q.shape, F32)
        for s, v in zip(scores, values):
            v_aug = jnp.where(head, v, jnp.broadcast_to(ones_row, v.shape))
            acc = acc + _dot(jnp.exp2(s - m).astype(BF16), v_aug)
        den = jnp.sum(jnp.where(ones_lane, acc, 0.0), axis=-1, keepdims=True)
        out = out + jnp.where(head, acc / den, 0.0)
    return out


def _pair_loop(pair_fn, pairs_per_iter):
    def body(it, carry):
        for sub in range(pairs_per_iter):
            p = it * pairs_per_iter + sub
            pair_fn(p, pl.ds(pl.multiple_of(p * LANES, LANES), LANES))
        return carry

    lax.fori_loop(0, N_HEAD_BLOCKS // pairs_per_iter, body, 0)


def _ctx_attn_kernel(q_ref, k_ref, v_ref, o_ref):
    def pair(p, cols):
        k = k_ref[:, cols].astype(BF16)
        v = v_ref[:, cols].astype(BF16)
        o = _attend_pair(q_ref[:, cols], [k], [v], lambda hh, i: None)
        o_ref[:, cols] = o.astype(o_ref.dtype)

    _pair_loop(pair, N_HEAD_BLOCKS)


def _ctx_attention(q, k, v, seq):
    t = q.shape[0]
    blk = lambda: pl.BlockSpec((seq, WIDTH_A), lambda b: (b, 0))
    return pl.pallas_call(
        _ctx_attn_kernel,
        out_shape=jax.ShapeDtypeStruct((t, WIDTH_A), BF16),
        grid=(t // seq,),
        in_specs=[blk(), blk(), blk()],
        out_specs=blk(),
        compiler_params=pltpu.CompilerParams(
            dimension_semantics=("parallel",), vmem_limit_bytes=VMEM_LIMIT),
        name="ctx_attn",
    )(q, k, v)


def _nbr_attn_kernel(q_ref, k0_ref, k1_ref, k2_ref, v0_ref, v1_ref, v2_ref,
                     ck_ref, cv_ref, bias_ref, o_ref):
    def pair(p, cols):
        keys = [k0_ref[:, cols], k1_ref[:, cols], k2_ref[:, cols], ck_ref[0, :, cols].astype(BF16)]
        values = [v0_ref[:, cols], v1_ref[:, cols], v2_ref[:, cols], cv_ref[0, :, cols].astype(BF16)]

        def bias_fn(hh, i):
            if i >= WIN_BLOCKS:
                return None
            return bias_ref[0, HEADS_PER_BLOCK * p + hh, :, i * Q_TOK:(i + 1) * Q_TOK]

        o = _attend_pair(q_ref[:, cols], keys, values, bias_fn)
        o_ref[:, cols] = o.astype(o_ref.dtype)

    _pair_loop(pair, PAIRS_PER_ITER)


def _nbr_attention(q, k, v, ck, cv, bias, n_tok):
    t = q.shape[0]
    batch = t // n_tok
    past = ck.shape[1]
    qb = n_tok // Q_TOK
    def win(m):
        return jnp.clip(m - 1, 0, qb - WIN_BLOCKS)
    def variant(m):
        return jnp.minimum(m, 1) + jnp.maximum(m - (qb - 2), 0)
    qspec = pl.BlockSpec((Q_TOK, WIDTH_A), lambda m, b: (b * qb + m, 0))
    def kvspec(w):
        return pl.BlockSpec((Q_TOK, WIDTH_A), lambda m, b: (b * qb + win(m) + w, 0))
    cspec = pl.BlockSpec((1, past, WIDTH_A), lambda m, b: (b, 0, 0))
    bspec = pl.BlockSpec((1, N_HEADS_A, Q_TOK, WIN_BLOCKS * Q_TOK),
                         lambda m, b: (variant(m), 0, 0, 0))
    return pl.pallas_call(
        _nbr_attn_kernel,
        out_shape=jax.ShapeDtypeStruct((t, WIDTH_A), BF16),
        grid=(qb, batch),
        in_specs=[qspec] + [kvspec(w) for w in range(WIN_BLOCKS)]
                 + [kvspec(w) for w in range(WIN_BLOCKS)] + [cspec, cspec, bspec],
        out_specs=qspec,
        compiler_params=pltpu.CompilerParams(
            dimension_semantics=("parallel", "arbitrary"), vmem_limit_bytes=VMEM_LIMIT),
        name="nbr_attn",
    )(q, k, k, k, v, v, v, ck, cv, bias)


def _bias_kernel(rpb_ref, o_ref, *, rows):
    h = pl.program_id(0)
    n_dr = 2 * NA_ROWS - 1
    n_dc = 2 * NA_COLS - 1
    qc = lax.broadcasted_iota(jnp.int32, (GRID_W, LANES), 0)
    ln = lax.broadcasted_iota(jnp.int32, (GRID_W, LANES), 1)
    kc = ln % GRID_W
    dc_idx = jnp.clip(kc - qc + NA_COLS - 1, 0, n_dc - 1)
    col_start = jnp.clip(qc - NA_COLS // 2, 0, GRID_W - NA_COLS)
    col_ok = (kc >= col_start) & (kc < col_start + NA_COLS)
    neg = jnp.full((GRID_W, LANES), NEG_INF, F32)
    toeplitz = []
    for dr in range(n_dr):
        acc = jnp.zeros((GRID_W, LANES), F32)
        for dc in range(n_dc):
            acc = jnp.where(dc_idx == dc, rpb_ref[h * (n_dr * n_dc) + dr * n_dc + dc], acc)
        toeplitz.append(jnp.where(col_ok, acc * LOG2E, neg))
    qb = rows // Q_ROWS
    kr = min(NA_ROWS, rows)
    for var, m in enumerate((0, 1, qb - 1)):
        wb = min(max(m - 1, 0), qb - WIN_BLOCKS)
        for a in range(Q_ROWS):
            r = Q_ROWS * m + a
            row_start = min(max(r - kr // 2, 0), rows - kr)
            for tp in range(WIN_BLOCKS * Q_ROWS // 2):
                halves = []
                for t in (2 * tp, 2 * tp + 1):
                    key_row = Q_ROWS * wb + t
                    ok = row_start <= key_row < row_start + kr
                    halves.append(toeplitz[key_row - r + NA_ROWS - 1] if ok else neg)
                tile = jnp.where(ln < GRID_W, halves[0], halves[1])
                o_ref[var, 0, a * GRID_W:(a + 1) * GRID_W, tp * LANES:(tp + 1) * LANES] = tile


def _bias_table(rpb, rows):
    return pl.pallas_call(
        functools.partial(_bias_kernel, rows=rows),
        out_shape=jax.ShapeDtypeStruct((3, N_HEADS_A, Q_TOK, WIN_BLOCKS * Q_TOK), F32),
        grid=(N_HEADS_A,),
        in_specs=[pl.BlockSpec(memory_space=pltpu.SMEM)],
        out_specs=pl.BlockSpec((3, 1, Q_TOK, WIN_BLOCKS * Q_TOK), lambda h: (0, h, 0, 0)),
        compiler_params=pltpu.CompilerParams(
            dimension_semantics=("arbitrary",), vmem_limit_bytes=VMEM_LIMIT),
        name="nbr_bias",
    )(rpb.reshape(-1))


def _mixout_kernel(x_ref, a_ref, u_ref, vg_ref, mod_ref, gn_ref, ws_ref, bs_ref,
                   na_ref, nb_ref, wo_ref, o_ref, gm_ref):
    tm = x_ref.shape[0]
    for g in range(N_GROUPS_B):
        cols = slice(g * GROUP_DIM_B, (g + 1) * GROUP_DIM_B)
        vn = _rms(_gelu_tanh(vg_ref[:, cols].astype(F32)), gn_ref[:, cols]).astype(BF16)
        ug = _gelu_tanh(u_ref[:, cols].astype(F32))
        w = ws_ref[g]
        b = bs_ref[:, g:g + 1]
        for c in range(tm // CHUNK):
            rows = slice(c * CHUNK, (c + 1) * CHUNK)
            gm_ref[rows, cols] = ug[rows] * (_dot(w, vn[rows]) + b)
    ya = _rms(a_ref[...].astype(F32), na_ref[...]).astype(BF16)
    yb = _rms(gm_ref[...], nb_ref[...]).astype(BF16)
    y = _dot(ya, wo_ref[:WIDTH_A, :]) + _dot(yb, wo_ref[WIDTH_A:, :])
    o_ref[...] = x_ref[...] + mod_ref[0, 5:6, :] * y


def _mixout(x, a, u, vg, mod, gmlp_norm, w_s, b_s_t, na_g, nb_g, w_out, *, rows_per_mod, tm=512):
    t = x.shape[0]
    tok = lambda w: pl.BlockSpec((tm, w), lambda i: (i, 0))
    full = lambda shape: pl.BlockSpec(shape, lambda i: (0,) * len(shape))
    return pl.pallas_call(
        _mixout_kernel,
        out_shape=jax.ShapeDtypeStruct((t, D_MODEL), F32),
        grid=(t // tm,),
        in_specs=[
            tok(D_MODEL), tok(WIDTH_A), tok(WIDTH_B), tok(WIDTH_B),
            pl.BlockSpec((1, N_MOD, D_MODEL), lambda i: (i * tm // rows_per_mod, 0, 0)),
            full((1, WIDTH_B)),
            full((N_GROUPS_B, CHUNK, CHUNK)),
            full((CHUNK, N_GROUPS_B)),
            full((1, WIDTH_A)),
            full((1, WIDTH_B)),
            pl.BlockSpec((MIX_WIDTH, D_MODEL), lambda i: (0, 0), pipeline_mode=pl.Buffered(1)),
        ],
        out_specs=tok(D_MODEL),
        scratch_shapes=[pltpu.VMEM((tm, WIDTH_B), F32)],
        compiler_params=pltpu.CompilerParams(
            dimension_semantics=("parallel",), vmem_limit_bytes=VMEM_LIMIT),
        name="mixout",
    )(x, a, u, vg, mod, gmlp_norm, w_s, b_s_t, na_g, nb_g, w_out)


def kernel(x_prompt, x_sample, cache_k, cache_v, c, c_ctx, w_ada, b_ada, ffn1_norm, ffn1_w_gate, ffn1_w_up, ffn1_w_down, mix_norm, w_in, rpb, gmlp_norm, w_s, b_s, out_norm_a, out_norm_b, w_out, ffn2_norm, ffn2_w_gate, ffn2_w_up, ffn2_w_down, final_norm):
    batch, seq, _ = x_prompt.shape
    dec_batch, dec_seq, _ = x_sample.shape
    depth = w_ada.shape[0]
    x_ctx = x_prompt.reshape(batch * seq, D_MODEL)
    x_lat = x_sample.reshape(dec_batch * dec_seq, D_MODEL)
    final_g = final_norm.reshape(1, D_MODEL)

    cvec = jnp.concatenate([c_ctx[None, :], c], axis=0)
    mod_rows = -(-cvec.shape[0] // 8) * 8
    cvec = jnp.pad(cvec, ((0, mod_rows - cvec.shape[0]), (0, 0)))

    new_k, new_v = [], []
    for l in range(depth):
        last = l == depth - 1
        mod = _adaln(cvec, w_ada[l], b_ada[l][None, :]).reshape(mod_rows, N_MOD, D_MODEL)
        mods = (mod[0:1], mod[1:1 + dec_batch])
        rows_per_mod = (batch * seq, dec_seq)

        f1 = (ffn1_norm[l][None, :], ffn1_w_gate[l].astype(BF16), ffn1_w_up[l].astype(BF16),
              ffn1_w_down[l].astype(BF16))
        mix_g = mix_norm[l][None, :]
        w_s_l = w_s[l].astype(BF16)
        b_s_t = b_s[l].T
        gn = gmlp_norm[l][None, :]
        na_g = out_norm_a[l][None, :]
        nb_g = out_norm_b[l][None, :]
        bias = _bias_table(rpb[l], dec_seq // GRID_W)
        ck = cache_k[:, l].reshape(dec_batch, -1, WIDTH_A)
        cv = cache_v[:, l].reshape(dec_batch, -1, WIDTH_A)

        n_j = D_FF // FFN_TF
        n_ctx, n_lat = x_ctx.shape[0] // FFN_TM, x_lat.shape[0] // FFN_TM
        x_ctx, w_in_l, w_out_l = _ffn(
            x_ctx, mods[0], *f1, mod_base=0, rows_per_mod=rows_per_mod[0],
            side=(_cast_plan(w_in[l], n_ctx, n_j), _cast_plan(w_out[l], n_ctx, n_j)))
        x_lat, *f2_w = _ffn(
            x_lat, mods[1], *f1, mod_base=0, rows_per_mod=rows_per_mod[1],
            side=(_cast_plan(ffn2_w_gate[l], n_lat, n_j), _cast_plan(ffn2_w_up[l], n_lat, n_j),
                  _cast_plan_t(ffn2_w_down[l], n_lat, n_j)))
        f2 = (ffn2_norm[l][None, :], *f2_w)

        xs = []
        for path, (x, m, rpm) in enumerate(zip((x_ctx, x_lat), mods, rows_per_mod)):
            q, k, v, u, vg = _mixin(x, m, mix_g, w_in_l, rows_per_mod=rpm,
                                    kv_dtype=F32 if path == 0 else BF16)
            if path == 0:
                a = _ctx_attention(q, k, v, seq)
                new_k.append(k.reshape(batch, seq, N_HEADS_A, HEAD_DIM_A))
                new_v.append(v.reshape(batch, seq, N_HEADS_A, HEAD_DIM_A))
            else:
                a = _nbr_attention(q, k, v, ck, cv, bias, dec_seq)
            x = _mixout(x, a, u, vg, m, gn, w_s_l, b_s_t, na_g, nb_g, w_out_l, rows_per_mod=rpm)
            x, = _ffn(x, m, *f2, mod_base=6, rows_per_mod=rpm,
                      final_g=final_g if last else None)
            xs.append(x)
        x_ctx, x_lat = xs

    if depth == 0:
        raise ValueError("depth must be positive")
    y_prompt = x_ctx.reshape(batch, seq, D_MODEL)
    y_sample = x_lat.reshape(dec_batch, dec_seq, D_MODEL)
    return (y_prompt, y_sample, jnp.stack(new_k, axis=1), jnp.stack(new_v, axis=1))
```

```python
import functools
import math

import jax
import jax.numpy as jnp
from jax import lax
from jax.experimental import pallas as pl
from jax.experimental.pallas import tpu as pltpu

D_MODEL = 2048
N_HEADS_A = 16
HEAD_DIM_A = 64
WIDTH_A = N_HEADS_A * HEAD_DIM_A
GRID_W = 64
NA_ROWS = 8
NA_COLS = 16
N_GROUPS_B = 8
GROUP_DIM_B = 128
WIDTH_B = N_GROUPS_B * GROUP_DIM_B
CHUNK = 128
MIX_WIDTH = WIDTH_A + WIDTH_B
IN_WIDTH = 3 * WIDTH_A + 2 * WIDTH_B
D_FF = 5632
N_MOD = 9
EPS = 1e-6
NEG_INF = -1e30

LANES = 128
HEADS_PER_BLOCK = LANES // HEAD_DIM_A
N_HEAD_BLOCKS = N_HEADS_A // HEADS_PER_BLOCK
Q_ROWS = 4
Q_TOK = Q_ROWS * GRID_W
WIN_BLOCKS = 3
VMEM_LIMIT = 62 * 1024 * 1024
ROW_CHUNK = 16
NORM_UNROLL = 8
FFN_OUT_CHUNK = 512
FFN_TM = 1024
FFN_TF = 512
PAIRS_PER_ITER = 4
LOG2E = math.log2(math.e)
Q_SCALE = HEAD_DIM_A ** -0.5 * LOG2E

F32 = jnp.float32
BF16 = jnp.bfloat16


def _dot(a, b):
    return jnp.dot(a, b, preferred_element_type=F32)


def _dot_nt(a, b):
    return lax.dot_general(a, b, (((1,), (1,)), ((), ())), preferred_element_type=F32)


def _rms(x, g):
    ms = jnp.mean(x * x, axis=-1, keepdims=True)
    return x * lax.rsqrt(ms + EPS) * g


def _silu(x):
    return x * jax.nn.sigmoid(x)


def _gelu_tanh(x):
    c = 0.7978845608028654
    return 0.5 * x * (1.0 + jnp.tanh(c * (x + 0.044715 * (x * x * x))))


def _row_chunk(c):
    return pl.ds(pl.multiple_of(c * ROW_CHUNK, ROW_CHUNK), ROW_CHUNK)


def _lane_tile(v, width):
    return jnp.concatenate([v] * (width // LANES), axis=1)


def _row_inv_rms(x_ref, rs_ref):
    def body(c, carry):
        r = _row_chunk(c)
        x = x_ref[r, :]
        ms = jnp.mean(x * x, axis=-1, keepdims=True)
        rs_ref[r, :] = jnp.broadcast_to(lax.rsqrt(ms + EPS), (ROW_CHUNK, LANES))
        return carry

    lax.fori_loop(0, x_ref.shape[0] // ROW_CHUNK, body, 0, unroll=NORM_UNROLL)


def _adaln_kernel(c_ref, w_ref, b_ref, o_ref):
    s = _silu(c_ref[...]).astype(BF16)
    o_ref[...] = _dot(s, w_ref[...].astype(BF16)) + b_ref[...]


def _adaln(cvec, w_ada, b_ada, tn=1024):
    rows = cvec.shape[0]
    n = w_ada.shape[1]
    return pl.pallas_call(
        _adaln_kernel,
        out_shape=jax.ShapeDtypeStruct((rows, n), F32),
        grid=(n // tn,),
        in_specs=[
            pl.BlockSpec((rows, D_MODEL), lambda j: (0, 0)),
            pl.BlockSpec((D_MODEL, tn), lambda j: (0, j)),
            pl.BlockSpec((1, tn), lambda j: (0, j)),
        ],
        out_specs=pl.BlockSpec((rows, tn), lambda j: (0, j)),
        compiler_params=pltpu.CompilerParams(
            dimension_semantics=("arbitrary",), vmem_limit_bytes=VMEM_LIMIT),
        name="adaln",
    )(cvec, w_ada, b_ada)


def _ffn_kernel(*refs, mod_base, final, n_side):
    x_ref, mod_ref, g_ref, wg_ref, wu_ref, wd_ref = refs[:6]
    n_in = 7 if final else 6
    fg_ref = refs[6] if final else None
    side_in = refs[n_in:n_in + n_side]
    o_ref = refs[n_in + n_side]
    side_out = refs[n_in + n_side + 1:n_in + 2 * n_side + 1]
    h_ref, rs_ref = refs[n_in + 2 * n_side + 1:]
    j = pl.program_id(1)
    n_row_chunks = x_ref.shape[0] // ROW_CHUNK

    def step(acc_ref):
        for src_ref, dst_ref in zip(side_in, side_out):
            dst_ref[...] = src_ref[...].astype(BF16)
        h = h_ref[...]
        a = (_silu(_dot(h, wg_ref[...])) * _dot(h, wu_ref[...])).astype(BF16)
        half_gate = 0.5 * mod_ref[0, mod_base + 2:mod_base + 3, :]
        for n in range(D_MODEL // FFN_OUT_CHUNK):
            cols = slice(n * FFN_OUT_CHUNK, (n + 1) * FFN_OUT_CHUNK)
            o_ref[:, cols] = acc_ref[:, cols] + half_gate[:, cols] * _dot(a, wd_ref[:, cols])

    @pl.when(j == 0)
    def _():
        _row_inv_rms(x_ref, rs_ref)
        shift = mod_ref[0, mod_base:mod_base + 1, :]
        gain = g_ref[...] * (1.0 + mod_ref[0, mod_base + 1:mod_base + 2, :])

        def body(c, carry):
            r = _row_chunk(c)
            h_ref[r, :] = (x_ref[r, :] * _lane_tile(rs_ref[r, :], D_MODEL) * gain + shift).astype(BF16)
            return carry

        lax.fori_loop(0, n_row_chunks, body, 0, unroll=NORM_UNROLL)
        step(x_ref)

    @pl.when(j > 0)
    def _():
        step(o_ref)

    if final:
        @pl.when(j == pl.num_programs(1) - 1)
        def _():
            _row_inv_rms(o_ref, rs_ref)
            fg = fg_ref[...]

            def body(c, carry):
                r = _row_chunk(c)
                o_ref[r, :] = o_ref[r, :] * _lane_tile(rs_ref[r, :], D_MODEL) * fg
                return carry

            lax.fori_loop(0, n_row_chunks, body, 0, unroll=NORM_UNROLL)


def _cast_plan(w, n_i, n_j):
    rows, cols = w.shape
    col_blocks = max(d for d in range(1, n_j + 1) if cols % d == 0 and (cols // d) % LANES == 0)
    assert rows % n_i == 0 and (rows // n_i) % 16 == 0
    block = (rows // n_i, cols // col_blocks)
    return w, block, lambda i, j: (i, jnp.minimum(j, col_blocks - 1))


def _cast_plan_t(w, n_i, n_j):
    rows, cols = w.shape
    assert rows % n_j == 0 and (rows // n_j) % 16 == 0 and cols % n_i == 0 and (cols // n_i) % LANES == 0
    return w, (rows // n_j, cols // n_i), lambda i, j: (j, i)


def _ffn(x, mod, norm_g, wg, wu, wd, *, mod_base, rows_per_mod, final_g=None, side=(),
         tm=FFN_TM, tf=FFN_TF):
    t = x.shape[0]
    final = final_g is not None
    in_specs = [
        pl.BlockSpec((tm, D_MODEL), lambda i, j: (i, 0)),
        pl.BlockSpec((1, N_MOD, D_MODEL), lambda i, j: (i * tm // rows_per_mod, 0, 0)),
        pl.BlockSpec((1, D_MODEL), lambda i, j: (0, 0)),
        pl.BlockSpec((D_MODEL, tf), lambda i, j: (0, j)),
        pl.BlockSpec((D_MODEL, tf), lambda i, j: (0, j)),
        pl.BlockSpec((tf, D_MODEL), lambda i, j: (j, 0)),
    ]
    args = [x, mod, norm_g, wg, wu, wd]
    if final:
        in_specs.append(pl.BlockSpec((1, D_MODEL), lambda i, j: (0, 0)))
        args.append(final_g)
    out_shape = [jax.ShapeDtypeStruct((t, D_MODEL), F32)]
    out_specs = [pl.BlockSpec((tm, D_MODEL), lambda i, j: (i, 0))]
    for w, block, index_map in side:
        in_specs.append(pl.BlockSpec(block, index_map))
        args.append(w)
        out_shape.append(jax.ShapeDtypeStruct(w.shape, BF16))
        out_specs.append(pl.BlockSpec(block, index_map))
    return pl.pallas_call(
        functools.partial(_ffn_kernel, mod_base=mod_base, final=final, n_side=len(side)),
        out_shape=out_shape,
        grid=(t // tm, D_FF // tf),
        in_specs=in_specs,
        out_specs=out_specs,
        scratch_shapes=[pltpu.VMEM((tm, D_MODEL), BF16), pltpu.VMEM((tm, LANES), F32)],
        compiler_params=pltpu.CompilerParams(
            dimension_semantics=("arbitrary", "arbitrary"), vmem_limit_bytes=VMEM_LIMIT),
        name="ffn_final" if final else "ffn",
    )(*args)


def _mixin_kernel(x_ref, mod_ref, g_ref, w_ref, q_ref, k_ref, v_ref, u_ref, vg_ref):
    shift = mod_ref[0, 3:4, :]
    scale = mod_ref[0, 4:5, :]
    h = (_rms(x_ref[...], g_ref[...]) * (1.0 + scale) + shift).astype(BF16)
    for idx, o_ref in enumerate((q_ref, k_ref, v_ref, u_ref, vg_ref)):
        p = _dot(h, w_ref[:, idx * WIDTH_A:(idx + 1) * WIDTH_A])
        if idx == 0:
            p = p * Q_SCALE
        o_ref[...] = p.astype(o_ref.dtype)


def _mixin(x, mod, norm_g, w_in, *, rows_per_mod, kv_dtype, tm=512):
    t = x.shape[0]
    out_dtypes = (BF16, kv_dtype, kv_dtype, BF16, BF16)
    return pl.pallas_call(
        _mixin_kernel,
        out_shape=[jax.ShapeDtypeStruct((t, WIDTH_A), dt) for dt in out_dtypes],
        grid=(t // tm,),
        in_specs=[
            pl.BlockSpec((tm, D_MODEL), lambda i: (i, 0)),
            pl.BlockSpec((1, N_MOD, D_MODEL), lambda i: (i * tm // rows_per_mod, 0, 0)),
            pl.BlockSpec((1, D_MODEL), lambda i: (0, 0)),
            pl.BlockSpec((D_MODEL, IN_WIDTH), lambda i: (0, 0), pipeline_mode=pl.Buffered(1)),
        ],
        out_specs=[pl.BlockSpec((tm, WIDTH_A), lambda i: (i, 0)) for _ in out_dtypes],
        compiler_params=pltpu.CompilerParams(
            dimension_semantics=("parallel",), vmem_limit_bytes=VMEM_LIMIT),
        name="mixin",
    )(x, mod, norm_g, w_in)


def _attend_pair(q, keys, values, bias_fn):
    lane = lax.broadcasted_iota(jnp.int32, (1, LANES), 1)
    out = jnp.zeros(q.shape, F32)
    for hh in range(HEADS_PER_BLOCK):
        head = (lane // HEAD_DIM_A) == hh
        ones_lane = lane == ((hh + 1) % HEADS_PER_BLOCK) * HEAD_DIM_A
        ones_row = jnp.where(ones_lane, 1.0, 0.0).astype(BF16)
        qh = jnp.where(head, q, jnp.zeros_like(q))
        scores = []
        for i, k in enumerate(keys):
            s = _dot_nt(qh, k)
            b = bias_fn(hh, i)
            if b is not None:
                s = s + b
            scores.append(s)
        m = scores[0].max(axis=-1, keepdims=True)
        for s in scores[1:]:
            m = jnp.maximum(m, s.max(axis=-1, keepdims=True))
        acc = jnp.zeros(q.shape, F32)
        for s, v in zip(scores, values):
            v_aug = jnp.where(head, v, jnp.broadcast_to(ones_row, v.shape))
            acc = acc + _dot(jnp.exp2(s - m).astype(BF16), v_aug)
        den = jnp.sum(jnp.where(ones_lane, acc, 0.0), axis=-1, keepdims=True)
        out = out + jnp.where(head, acc / den, 0.0)
    return out


def _pair_loop(pair_fn, pairs_per_iter):
    def body(it, carry):
        for sub in range(pairs_per_iter):
            p = it * pairs_per_iter + sub
            pair_fn(p, pl.ds(pl.multiple_of(p * LANES, LANES), LANES))
        return carry

    lax.fori_loop(0, N_HEAD_BLOCKS // pairs_per_iter, body, 0)


def _ctx_attn_kernel(q_ref, k_ref, v_ref, o_ref):
    def pair(p, cols):
        k = k_ref[:, cols].astype(BF16)
        v = v_ref[:, cols].astype(BF16)
        o = _attend_pair(q_ref[:, cols], [k], [v], lambda hh, i: None)
        o_ref[:, cols] = o.astype(o_ref.dtype)

    _pair_loop(pair, N_HEAD_BLOCKS)


def _ctx_attention(q, k, v, seq):
    t = q.shape[0]
    blk = lambda: pl.BlockSpec((seq, WIDTH_A), lambda b: (b, 0))
    return pl.pallas_call(
        _ctx_attn_kernel,
        out_shape=jax.ShapeDtypeStruct((t, WIDTH_A), BF16),
        grid=(t // seq,),
        in_specs=[blk(), blk(), blk()],
        out_specs=blk(),
        compiler_params=pltpu.CompilerParams(
            dimension_semantics=("parallel",), vmem_limit_bytes=VMEM_LIMIT),
        name="ctx_attn",
    )(q, k, v)


def _nbr_attn_kernel(q_ref, k0_ref, k1_ref, k2_ref, v0_ref, v1_ref, v2_ref,
                     ck_ref, cv_ref, bias_ref, o_ref):
    def pair(p, cols):
        keys = [k0_ref[:, cols], k1_ref[:, cols], k2_ref[:, cols], ck_ref[0, :, cols].astype(BF16)]
        values = [v0_ref[:, cols], v1_ref[:, cols], v2_ref[:, cols], cv_ref[0, :, cols].astype(BF16)]

        def bias_fn(hh, i):
            if i >= WIN_BLOCKS:
                return None
            return bias_ref[0, HEADS_PER_BLOCK * p + hh, :, i * Q_TOK:(i + 1) * Q_TOK]

        o = _attend_pair(q_ref[:, cols], keys, values, bias_fn)
        o_ref[:, cols] = o.astype(o_ref.dtype)

    _pair_loop(pair, PAIRS_PER_ITER)


def _nbr_attention(q, k, v, ck, cv, bias, n_tok):
    t = q.shape[0]
    batch = t // n_tok
    past = ck.shape[1]
    qb = n_tok // Q_TOK
    def win(m):
        return jnp.clip(m - 1, 0, qb - WIN_BLOCKS)
    def variant(m):
        return jnp.minimum(m, 1) + jnp.maximum(m - (qb - 2), 0)
    qspec = pl.BlockSpec((Q_TOK, WIDTH_A), lambda m, b: (b * qb + m, 0))
    def kvspec(w):
        return pl.BlockSpec((Q_TOK, WIDTH_A), lambda m, b: (b * qb + win(m) + w, 0))
    cspec = pl.BlockSpec((1, past, WIDTH_A), lambda m, b: (b, 0, 0))
    bspec = pl.BlockSpec((1, N_HEADS_A, Q_TOK, WIN_BLOCKS * Q_TOK),
                         lambda m, b: (variant(m), 0, 0, 0))
    return pl.pallas_call(
        _nbr_attn_kernel,
        out_shape=jax.ShapeDtypeStruct((t, WIDTH_A), BF16),
        grid=(qb, batch),
        in_specs=[qspec] + [kvspec(w) for w in range(WIN_BLOCKS)]
                 + [kvspec(w) for w in range(WIN_BLOCKS)] + [cspec, cspec, bspec],
        out_specs=qspec,
        compiler_params=pltpu.CompilerParams(
            dimension_semantics=("parallel", "arbitrary"), vmem_limit_bytes=VMEM_LIMIT),
        name="nbr_attn",
    )(q, k, k, k, v, v, v, ck, cv, bias)


def _bias_kernel(rpb_ref, o_ref, *, rows):
    h = pl.program_id(0)
    n_dr = 2 * NA_ROWS - 1
    n_dc = 2 * NA_COLS - 1
    qc = lax.broadcasted_iota(jnp.int32, (GRID_W, LANES), 0)
    ln = lax.broadcasted_iota(jnp.int32, (GRID_W, LANES), 1)
    kc = ln % GRID_W
    dc_idx = jnp.clip(kc - qc + NA_COLS - 1, 0, n_dc - 1)
    col_start = jnp.clip(qc - NA_COLS // 2, 0, GRID_W - NA_COLS)
    col_ok = (kc >= col_start) & (kc < col_start + NA_COLS)
    neg = jnp.full((GRID_W, LANES), NEG_INF, F32)
    toeplitz = []
    for dr in range(n_dr):
        acc = jnp.zeros((GRID_W, LANES), F32)
        for dc in range(n_dc):
            acc = jnp.where(dc_idx == dc, rpb_ref[h * (n_dr * n_dc) + dr * n_dc + dc], acc)
        toeplitz.append(jnp.where(col_ok, acc * LOG2E, neg))
    qb = rows // Q_ROWS
    kr = min(NA_ROWS, rows)
    for var, m in enumerate((0, 1, qb - 1)):
        wb = min(max(m - 1, 0), qb - WIN_BLOCKS)
        for a in range(Q_ROWS):
            r = Q_ROWS * m + a
            row_start = min(max(r - kr // 2, 0), rows - kr)
            for tp in range(WIN_BLOCKS * Q_ROWS // 2):
                halves = []
                for t in (2 * tp, 2 * tp + 1):
                    key_row = Q_ROWS * wb + t
                    ok = row_start <= key_row < row_start + kr
                    halves.append(toeplitz[key_row - r + NA_ROWS - 1] if ok else neg)
                tile = jnp.where(ln < GRID_W, halves[0], halves[1])
                o_ref[var, 0, a * GRID_W:(a + 1) * GRID_W, tp * LANES:(tp + 1) * LANES] = tile


def _bias_table(rpb, rows):
    return pl.pallas_call(
        functools.partial(_bias_kernel, rows=rows),
        out_shape=jax.ShapeDtypeStruct((3, N_HEADS_A, Q_TOK, WIN_BLOCKS * Q_TOK), F32),
        grid=(N_HEADS_A,),
        in_specs=[pl.BlockSpec(memory_space=pltpu.SMEM)],
        out_specs=pl.BlockSpec((3, 1, Q_TOK, WIN_BLOCKS * Q_TOK), lambda h: (0, h, 0, 0)),
        compiler_params=pltpu.CompilerParams(
            dimension_semantics=("arbitrary",), vmem_limit_bytes=VMEM_LIMIT),
        name="nbr_bias",
    )(rpb.reshape(-1))


def _mixout_kernel(x_ref, a_ref, u_ref, vg_ref, mod_ref, gn_ref, ws_ref, bs_ref,
                   na_ref, nb_ref, wo_ref, o_ref, gm_ref):
    tm = x_ref.shape[0]
    for g in range(N_GROUPS_B):
        cols = slice(g * GROUP_DIM_B, (g + 1) * GROUP_DIM_B)
        vn = _rms(_gelu_tanh(vg_ref[:, cols].astype(F32)), gn_ref[:, cols]).astype(BF16)
        ug = _gelu_tanh(u_ref[:, cols].astype(F32))
        w = ws_ref[g]
        b = bs_ref[:, g:g + 1]
        for c in range(tm // CHUNK):
            rows = slice(c * CHUNK, (c + 1) * CHUNK)
            gm_ref[rows, cols] = ug[rows] * (_dot(w, vn[rows]) + b)
    ya = _rms(a_ref[...].astype(F32), na_ref[...]).astype(BF16)
    yb = _rms(gm_ref[...], nb_ref[...]).astype(BF16)
    y = _dot(ya, wo_ref[:WIDTH_A, :]) + _dot(yb, wo_ref[WIDTH_A:, :])
    o_ref[...] = x_ref[...] + mod_ref[0, 5:6, :] * y


def _mixout(x, a, u, vg, mod, gmlp_norm, w_s, b_s_t, na_g, nb_g, w_out, *, rows_per_mod, tm=512):
    t = x.shape[0]
    tok = lambda w: pl.BlockSpec((tm, w), lambda i: (i, 0))
    full = lambda shape: pl.BlockSpec(shape, lambda i: (0,) * len(shape))
    return pl.pallas_call(
        _mixout_kernel,
        out_shape=jax.ShapeDtypeStruct((t, D_MODEL), F32),
        grid=(t // tm,),
        in_specs=[
            tok(D_MODEL), tok(WIDTH_A), tok(WIDTH_B), tok(WIDTH_B),
            pl.BlockSpec((1, N_MOD, D_MODEL), lambda i: (i * tm // rows_per_mod, 0, 0)),
            full((1, WIDTH_B)),
            full((N_GROUPS_B, CHUNK, CHUNK)),
            full((CHUNK, N_GROUPS_B)),
            full((1, WIDTH_A)),
            full((1, WIDTH_B)),
            pl.BlockSpec((MIX_WIDTH, D_MODEL), lambda i: (0, 0), pipeline_mode=pl.Buffered(1)),
        ],
        out_specs=tok(D_MODEL),
        scratch_shapes=[pltpu.VMEM((tm, WIDTH_B), F32)],
        compiler_params=pltpu.CompilerParams(
            dimension_semantics=("parallel",), vmem_limit_bytes=VMEM_LIMIT),
        name="mixout",
    )(x, a, u, vg, mod, gmlp_norm, w_s, b_s_t, na_g, nb_g, w_out)


def kernel(x_prompt, x_sample, cache_k, cache_v, c, c_ctx, w_ada, b_ada, ffn1_norm, ffn1_w_gate, ffn1_w_up, ffn1_w_down, mix_norm, w_in, rpb, gmlp_norm, w_s, b_s, out_norm_a, out_norm_b, w_out, ffn2_norm, ffn2_w_gate, ffn2_w_up, ffn2_w_down, final_norm):
    batch, seq, _ = x_prompt.shape
    dec_batch, dec_seq, _ = x_sample.shape
    depth = w_ada.shape[0]
    x_ctx = x_prompt.reshape(batch * seq, D_MODEL)
    x_lat = x_sample.reshape(dec_batch * dec_seq, D_MODEL)
    final_g = final_norm.reshape(1, D_MODEL)

    cvec = jnp.concatenate([c_ctx[None, :], c], axis=0)
    mod_rows = -(-cvec.shape[0] // 8) * 8
    cvec = jnp.pad(cvec, ((0, mod_rows - cvec.shape[0]), (0, 0)))

    new_k, new_v = [], []
    for l in range(depth):
        last = l == depth - 1
        mod = _adaln(cvec, w_ada[l], b_ada[l][None, :]).reshape(mod_rows, N_MOD, D_MODEL)
        mods = (mod[0:1], mod[1:1 + dec_batch])
        rows_per_mod = (batch * seq, dec_seq)

        f1 = (ffn1_norm[l][None, :], ffn1_w_gate[l].astype(BF16), ffn1_w_up[l].astype(BF16),
              ffn1_w_down[l].astype(BF16))
        mix_g = mix_norm[l][None, :]
        w_s_l = w_s[l].astype(BF16)
        b_s_t = b_s[l].T
        gn = gmlp_norm[l][None, :]
        na_g = out_norm_a[l][None, :]
        nb_g = out_norm_b[l][None, :]
        bias = _bias_table(rpb[l], dec_seq // GRID_W)
        ck = cache_k[:, l].reshape(dec_batch, -1, WIDTH_A)
        cv = cache_v[:, l].reshape(dec_batch, -1, WIDTH_A)

        n_j = D_FF // FFN_TF
        n_ctx, n_lat = x_ctx.shape[0] // FFN_TM, x_lat.shape[0] // FFN_TM
        x_ctx, w_in_l, w_out_l = _ffn(
            x_ctx, mods[0], *f1, mod_base=0, rows_per_mod=rows_per_mod[0],
            side=(_cast_plan(w_in[l], n_ctx, n_j), _cast_plan(w_out[l], n_ctx, n_j)))
        x_lat, *f2_w = _ffn(
            x_lat, mods[1], *f1, mod_base=0, rows_per_mod=rows_per_mod[1],
            side=(_cast_plan(ffn2_w_gate[l], n_lat, n_j), _cast_plan(ffn2_w_up[l], n_lat, n_j),
                  _cast_plan_t(ffn2_w_down[l], n_lat, n_j)))
        f2 = (ffn2_norm[l][None, :], *f2_w)

        xs = []
        for path, (x, m, rpm) in enumerate(zip((x_ctx, x_lat), mods, rows_per_mod)):
            q, k, v, u, vg = _mixin(x, m, mix_g, w_in_l, rows_per_mod=rpm,
                                    kv_dtype=F32 if path == 0 else BF16)
            if path == 0:
                a = _ctx_attention(q, k, v, seq)
                new_k.append(k.reshape(batch, seq, N_HEADS_A, HEAD_DIM_A))
                new_v.append(v.reshape(batch, seq, N_HEADS_A, HEAD_DIM_A))
            else:
                a = _nbr_attention(q, k, v, ck, cv, bias, dec_seq)
            x = _mixout(x, a, u, vg, m, gn, w_s_l, b_s_t, na_g, nb_g, w_out_l, rows_per_mod=rpm)
            x, = _ffn(x, m, *f2, mod_base=6, rows_per_mod=rpm,
                      final_g=final_g if last else None)
            xs.append(x)
        x_ctx, x_lat = xs

    if depth == 0:
        raise ValueError("depth must be positive")
    y_prompt = x_ctx.reshape(batch, seq, D_MODEL)
    y_sample = x_lat.reshape(dec_batch, dec_seq, D_MODEL)
    return (y_prompt, y_sample, jnp.stack(new_k, axis=1), jnp.stack(new_v, axis=1))
```

```python
import functools
import math

import jax
import jax.numpy as jnp
from jax import lax
from jax.experimental import pallas as pl
from jax.experimental.pallas import tpu as pltpu

D_MODEL = 2048
N_HEADS_A = 16
HEAD_DIM_A = 64
WIDTH_A = N_HEADS_A * HEAD_DIM_A
GRID_W = 64
NA_ROWS = 8
NA_COLS = 16
N_GROUPS_B = 8
GROUP_DIM_B = 128
WIDTH_B = N_GROUPS_B * GROUP_DIM_B
CHUNK = 128
MIX_WIDTH = WIDTH_A + WIDTH_B
IN_WIDTH = 3 * WIDTH_A + 2 * WIDTH_B
D_FF = 5632
N_MOD = 9
EPS = 1e-6
NEG_INF = -1e30

LANES = 128
HEADS_PER_BLOCK = LANES // HEAD_DIM_A
N_HEAD_BLOCKS = N_HEADS_A // HEADS_PER_BLOCK
Q_ROWS = 4
Q_TOK = Q_ROWS * GRID_W
WIN_BLOCKS = 3
VMEM_LIMIT = 62 * 1024 * 1024
ROW_CHUNK = 16
NORM_UNROLL = 8
FFN_OUT_CHUNK = 512
FFN_TM = 1024
FFN_TF = 512
PAIRS_PER_ITER = 4
LOG2E = math.log2(math.e)
Q_SCALE = HEAD_DIM_A ** -0.5 * LOG2E

F32 = jnp.float32
BF16 = jnp.bfloat16


def _dot(a, b):
    return jnp.dot(a, b, preferred_element_type=F32)


def _dot_nt(a, b):
    return lax.dot_general(a, b, (((1,), (1,)), ((), ())), preferred_element_type=F32)


def _rms(x, g):
    ms = jnp.mean(x * x, axis=-1, keepdims=True)
    return x * lax.rsqrt(ms + EPS) * g


def _silu(x):
    return x * jax.nn.sigmoid(x)


def _gelu_tanh(x):
    c = 0.7978845608028654
    return 0.5 * x * (1.0 + jnp.tanh(c * (x + 0.044715 * (x * x * x))))


def _row_chunk(c):
    return pl.ds(pl.multiple_of(c * ROW_CHUNK, ROW_CHUNK), ROW_CHUNK)


def _lane_tile(v, width):
    return jnp.concatenate([v] * (width // LANES), axis=1)


def _row_inv_rms(x_ref, rs_ref):
    def body(c, carry):
        r = _row_chunk(c)
        x = x_ref[r, :]
        ms = jnp.mean(x * x, axis=-1, keepdims=True)
        rs_ref[r, :] = jnp.broadcast_to(lax.rsqrt(ms + EPS), (ROW_CHUNK, LANES))
        return carry

    lax.fori_loop(0, x_ref.shape[0] // ROW_CHUNK, body, 0, unroll=NORM_UNROLL)


def _adaln_kernel(c_ref, w_ref, b_ref, o_ref):
    s = _silu(c_ref[...]).astype(BF16)
    o_ref[...] = _dot(s, w_ref[...].astype(BF16)) + b_ref[...]


def _adaln(cvec, w_ada, b_ada, tn=1024):
    rows = cvec.shape[0]
    n = w_ada.shape[1]
    return pl.pallas_call(
        _adaln_kernel,
        out_shape=jax.ShapeDtypeStruct((rows, n), F32),
        grid=(n // tn,),
        in_specs=[
            pl.BlockSpec((rows, D_MODEL), lambda j: (0, 0)),
            pl.BlockSpec((D_MODEL, tn), lambda j: (0, j)),
            pl.BlockSpec((1, tn), lambda j: (0, j)),
        ],
        out_specs=pl.BlockSpec((rows, tn), lambda j: (0, j)),
        compiler_params=pltpu.CompilerParams(
            dimension_semantics=("arbitrary",), vmem_limit_bytes=VMEM_LIMIT),
        name="adaln",
    )(cvec, w_ada, b_ada)


def _ffn_kernel(*refs, mod_base, final, n_side):
    x_ref, mod_ref, g_ref, wg_ref, wu_ref, wd_ref = refs[:6]
    n_in = 7 if final else 6
    fg_ref = refs[6] if final else None
    side_in = refs[n_in:n_in + n_side]
    o_ref = refs[n_in + n_side]
    side_out = refs[n_in + n_side + 1:n_in + 2 * n_side + 1]
    h_ref, rs_ref = refs[n_in + 2 * n_side + 1:]
    j = pl.program_id(1)
    n_row_chunks = x_ref.shape[0] // ROW_CHUNK

    def step(acc_ref):
        for src_ref, dst_ref in zip(side_in, side_out):
            dst_ref[...] = src_ref[...].astype(BF16)
        h = h_ref[...]
        a = (_silu(_dot(h, wg_ref[...])) * _dot(h, wu_ref[...])).astype(BF16)
        half_gate = 0.5 * mod_ref[0, mod_base + 2:mod_base + 3, :]
        for n in range(D_MODEL // FFN_OUT_CHUNK):
            cols = slice(n * FFN_OUT_CHUNK, (n + 1) * FFN_OUT_CHUNK)
            o_ref[:, cols] = acc_ref[:, cols] + half_gate[:, cols] * _dot(a, wd_ref[:, cols])

    @pl.when(j == 0)
    def _():
        _row_inv_rms(x_ref, rs_ref)
        shift = mod_ref[0, mod_base:mod_base + 1, :]
        gain = g_ref[...] * (1.0 + mod_ref[0, mod_base + 1:mod_base + 2, :])

        def body(c, carry):
            r = _row_chunk(c)
            h_ref[r, :] = (x_ref[r, :] * _lane_tile(rs_ref[r, :], D_MODEL) * gain + shift).astype(BF16)
            return carry

        lax.fori_loop(0, n_row_chunks, body, 0, unroll=NORM_UNROLL)
        step(x_ref)

    @pl.when(j > 0)
    def _():
        step(o_ref)

    if final:
        @pl.when(j == pl.num_programs(1) - 1)
        def _():
            _row_inv_rms(o_ref, rs_ref)
            fg = fg_ref[...]

            def body(c, carry):
                r = _row_chunk(c)
                o_ref[r, :] = o_ref[r, :] * _lane_tile(rs_ref[r, :], D_MODEL) * fg
                return carry

            lax.fori_loop(0, n_row_chunks, body, 0, unroll=NORM_UNROLL)


def _cast_plan(w, n_i, n_j):
    rows, cols = w.shape
    col_blocks = max(d for d in range(1, n_j + 1) if cols % d == 0 and (cols // d) % LANES == 0)
    assert rows % n_i == 0 and (rows // n_i) % 16 == 0
    block = (rows // n_i, cols // col_blocks)
    return w, block, lambda i, j: (i, jnp.minimum(j, col_blocks - 1))


def _cast_plan_t(w, n_i, n_j):
    rows, cols = w.shape
    assert rows % n_j == 0 and (rows // n_j) % 16 == 0 and cols % n_i == 0 and (cols // n_i) % LANES == 0
    return w, (rows // n_j, cols // n_i), lambda i, j: (j, i)


def _ffn(x, mod, norm_g, wg, wu, wd, *, mod_base, rows_per_mod, final_g=None, side=(),
         tm=FFN_TM, tf=FFN_TF):
    t = x.shape[0]
    final = final_g is not None
    in_specs = [
        pl.BlockSpec((tm, D_MODEL), lambda i, j: (i, 0)),
        pl.BlockSpec((1, N_MOD, D_MODEL), lambda i, j: (i * tm // rows_per_mod, 0, 0)),
        pl.BlockSpec((1, D_MODEL), lambda i, j: (0, 0)),
        pl.BlockSpec((D_MODEL, tf), lambda i, j: (0, j)),
        pl.BlockSpec((D_MODEL, tf), lambda i, j: (0, j)),
        pl.BlockSpec((tf, D_MODEL), lambda i, j: (j, 0)),
    ]
    args = [x, mod, norm_g, wg, wu, wd]
    if final:
        in_specs.append(pl.BlockSpec((1, D_MODEL), lambda i, j: (0, 0)))
        args.append(final_g)
    out_shape = [jax.ShapeDtypeStruct((t, D_MODEL), F32)]
    out_specs = [pl.BlockSpec((tm, D_MODEL), lambda i, j: (i, 0))]
    for w, block, index_map in side:
        in_specs.append(pl.BlockSpec(block, index_map))
        args.append(w)
        out_shape.append(jax.ShapeDtypeStruct(w.shape, BF16))
        out_specs.append(pl.BlockSpec(block, index_map))
    return pl.pallas_call(
        functools.partial(_ffn_kernel, mod_base=mod_base, final=final, n_side=len(side)),
        out_shape=out_shape,
        grid=(t // tm, D_FF // tf),
        in_specs=in_specs,
        out_specs=out_specs,
        scratch_shapes=[pltpu.VMEM((tm, D_MODEL), BF16), pltpu.VMEM((tm, LANES), F32)],
        compiler_params=pltpu.CompilerParams(
            dimension_semantics=("arbitrary", "arbitrary"), vmem_limit_bytes=VMEM_LIMIT),
        name="ffn_final" if final else "ffn",
    )(*args)


def _mixin_kernel(x_ref, mod_ref, g_ref, w_ref, q_ref, k_ref, v_ref, u_ref, vg_ref):
    shift = mod_ref[0, 3:4, :]
    scale = mod_ref[0, 4:5, :]
    h = (_rms(x_ref[...], g_ref[...]) * (1.0 + scale) + shift).astype(BF16)
    for idx, o_ref in enumerate((q_ref, k_ref, v_ref, u_ref, vg_ref)):
        p = _dot(h, w_ref[:, idx * WIDTH_A:(idx + 1) * WIDTH_A])
        if idx == 0:
            p = p * Q_SCALE
        o_ref[...] = p.astype(o_ref.dtype)


def _mixin(x, mod, norm_g, w_in, *, rows_per_mod, kv_dtype, tm=512):
    t = x.shape[0]
    out_dtypes = (BF16, kv_dtype, kv_dtype, BF16, BF16)
    return pl.pallas_call(
        _mixin_kernel,
        out_shape=[jax.ShapeDtypeStruct((t, WIDTH_A), dt) for dt in out_dtypes],
        grid=(t // tm,),
        in_specs=[
            pl.BlockSpec((tm, D_MODEL), lambda i: (i, 0)),
            pl.BlockSpec((1, N_MOD, D_MODEL), lambda i: (i * tm // rows_per_mod, 0, 0)),
            pl.BlockSpec((1, D_MODEL), lambda i: (0, 0)),
            pl.BlockSpec((D_MODEL, IN_WIDTH), lambda i: (0, 0), pipeline_mode=pl.Buffered(1)),
        ],
        out_specs=[pl.BlockSpec((tm, WIDTH_A), lambda i: (i, 0)) for _ in out_dtypes],
        compiler_params=pltpu.CompilerParams(
            dimension_semantics=("parallel",), vmem_limit_bytes=VMEM_LIMIT),
        name="mixin",
    )(x, mod, norm_g, w_in)


def _attend_pair(q, keys, values, bias_fn):
    lane = lax.broadcasted_iota(jnp.int32, (1, LANES), 1)
    out = jnp.zeros(q.shape, F32)
    for hh in range(HEADS_PER_BLOCK):
        head = (lane // HEAD_DIM_A) == hh
        ones_lane = lane == ((hh + 1) % HEADS_PER_BLOCK) * HEAD_DIM_A
        ones_row = jnp.where(ones_lane, 1.0, 0.0).astype(BF16)
        qh = jnp.where(head, q, jnp.zeros_like(q))
        scores = []
        for i, k in enumerate(keys):
            s = _dot_nt(qh, k)
            b = bias_fn(hh, i)
            if b is not None:
                s = s + b
            scores.append(s)
        m = scores[0].max(axis=-1, keepdims=True)
        for s in scores[1:]:
            m = jnp.maximum(m, s.max(axis=-1, keepdims=True))
        acc = jnp.zeros(q.shape, F32)
        for s, v in zip(scores, values):
            v_aug = jnp.where(head, v, jnp.broadcast_to(ones_row, v.shape))
            acc = acc + _dot(jnp.exp2(s - m).astype(BF16), v_aug)
        den = jnp.sum(jnp.where(ones_lane, acc, 0.0), axis=-1, keepdims=True)
        out = out + jnp.where(head, acc / den, 0.0)
    return out


def _pair_loop(pair_fn, pairs_per_iter):
    def body(it, carry):
        for sub in range(pairs_per_iter):
            p = it * pairs_per_iter + sub
            pair_fn(p, pl.ds(pl.multiple_of(p * LANES, LANES), LANES))
        return carry

    lax.fori_loop(0, N_HEAD_BLOCKS // pairs_per_iter, body, 0)


def _ctx_attn_kernel(q_ref, k_ref, v_ref, o_ref):
    def pair(p, cols):
        k = k_ref[:, cols].astype(BF16)
        v = v_ref[:, cols].astype(BF16)
        o = _attend_pair(q_ref[:, cols], [k], [v], lambda hh, i: None)
        o_ref[:, cols] = o.astype(o_ref.dtype)

    _pair_loop(pair, N_HEAD_BLOCKS)


def _ctx_attention(q, k, v, seq):
    t = q.shape[0]
    blk = lambda: pl.BlockSpec((seq, WIDTH_A), lambda b: (b, 0))
    return pl.pallas_call(
        _ctx_attn_kernel,
        out_shape=jax.ShapeDtypeStruct((t, WIDTH_A), BF16),
        grid=(t // seq,),
        in_specs=[blk(), blk(), blk()],
        out_specs=blk(),
        compiler_params=pltpu.CompilerParams(
            dimension_semantics=("parallel",), vmem_limit_bytes=VMEM_LIMIT),
        name="ctx_attn",
    )(q, k, v)


def _nbr_attn_kernel(q_ref, k0_ref, k1_ref, k2_ref, v0_ref, v1_ref, v2_ref,
                     ck_ref, cv_ref, bias_ref, o_ref, ckb_ref, cvb_ref):
    @pl.when(pl.program_id(1) == 0)
    def _():
        past = ckb_ref.shape[0]
        for src_ref, dst_ref in ((ck_ref, ckb_ref), (cv_ref, cvb_ref)):
            for p in range(N_HEAD_BLOCKS):
                heads = [src_ref[0, pl.ds(HEADS_PER_BLOCK * p + hh, past, stride=N_HEADS_A), :]
                         for hh in range(HEADS_PER_BLOCK)]
                dst_ref[:, p * LANES:(p + 1) * LANES] = jnp.concatenate(heads, axis=1).astype(BF16)

    def pair(p, cols):
        keys = [k0_ref[:, cols], k1_ref[:, cols], k2_ref[:, cols], ckb_ref[:, cols]]
        values = [v0_ref[:, cols], v1_ref[:, cols], v2_ref[:, cols], cvb_ref[:, cols]]

        def bias_fn(hh, i):
            if i >= WIN_BLOCKS:
                return None
            return bias_ref[0, HEADS_PER_BLOCK * p + hh, :, i * Q_TOK:(i + 1) * Q_TOK]

        o = _attend_pair(q_ref[:, cols], keys, values, bias_fn)
        o_ref[:, cols] = o.astype(o_ref.dtype)

    _pair_loop(pair, PAIRS_PER_ITER)


def _nbr_attention(q, k, v, ck, cv, bias, n_tok):
    t = q.shape[0]
    batch = t // n_tok
    past = ck.shape[1] // N_HEADS_A
    qb = n_tok // Q_TOK
    def win(m):
        return jnp.clip(m - 1, 0, qb - WIN_BLOCKS)
    def variant(m):
        return jnp.minimum(m, 1) + jnp.maximum(m - (qb - 2), 0)
    qspec = pl.BlockSpec((Q_TOK, WIDTH_A), lambda b, m: (b * qb + m, 0))
    def kvspec(w):
        return pl.BlockSpec((Q_TOK, WIDTH_A), lambda b, m: (b * qb + win(m) + w, 0))
    cspec = pl.BlockSpec((1, past * N_HEADS_A, HEAD_DIM_A), lambda b, m: (b, 0, 0),
                         pipeline_mode=pl.Buffered(1))
    bspec = pl.BlockSpec((1, N_HEADS_A, Q_TOK, WIN_BLOCKS * Q_TOK),
                         lambda b, m: (variant(m), 0, 0, 0))
    return pl.pallas_call(
        _nbr_attn_kernel,
        out_shape=jax.ShapeDtypeStruct((t, WIDTH_A), BF16),
        grid=(batch, qb),
        in_specs=[qspec] + [kvspec(w) for w in range(WIN_BLOCKS)]
                 + [kvspec(w) for w in range(WIN_BLOCKS)] + [cspec, cspec, bspec],
        out_specs=qspec,
        scratch_shapes=[pltpu.VMEM((past, WIDTH_A), BF16), pltpu.VMEM((past, WIDTH_A), BF16)],
        compiler_params=pltpu.CompilerParams(
            dimension_semantics=("arbitrary", "arbitrary"), vmem_limit_bytes=VMEM_LIMIT),
        name="nbr_attn",
    )(q, k, k, k, v, v, v, ck, cv, bias)


def _bias_kernel(rpb_ref, o_ref, *, rows):
    h = pl.program_id(0)
    n_dr = 2 * NA_ROWS - 1
    n_dc = 2 * NA_COLS - 1
    qc = lax.broadcasted_iota(jnp.int32, (GRID_W, LANES), 0)
    ln = lax.broadcasted_iota(jnp.int32, (GRID_W, LANES), 1)
    kc = ln % GRID_W
    dc_idx = jnp.clip(kc - qc + NA_COLS - 1, 0, n_dc - 1)
    col_start = jnp.clip(qc - NA_COLS // 2, 0, GRID_W - NA_COLS)
    col_ok = (kc >= col_start) & (kc < col_start + NA_COLS)
    neg = jnp.full((GRID_W, LANES), NEG_INF, F32)
    toeplitz = []
    for dr in range(n_dr):
        acc = jnp.zeros((GRID_W, LANES), F32)
        for dc in range(n_dc):
            acc = jnp.where(dc_idx == dc, rpb_ref[h * (n_dr * n_dc) + dr * n_dc + dc], acc)
        toeplitz.append(jnp.where(col_ok, acc * LOG2E, neg))
    qb = rows // Q_ROWS
    kr = min(NA_ROWS, rows)
    for var, m in enumerate((0, 1, qb - 1)):
        wb = min(max(m - 1, 0), qb - WIN_BLOCKS)
        for a in range(Q_ROWS):
            r = Q_ROWS * m + a
            row_start = min(max(r - kr // 2, 0), rows - kr)
            for tp in range(WIN_BLOCKS * Q_ROWS // 2):
                halves = []
                for t in (2 * tp, 2 * tp + 1):
                    key_row = Q_ROWS * wb + t
                    ok = row_start <= key_row < row_start + kr
                    halves.append(toeplitz[key_row - r + NA_ROWS - 1] if ok else neg)
                tile = jnp.where(ln < GRID_W, halves[0], halves[1])
                o_ref[var, 0, a * GRID_W:(a + 1) * GRID_W, tp * LANES:(tp + 1) * LANES] = tile


def _bias_table(rpb, rows):
    return pl.pallas_call(
        functools.partial(_bias_kernel, rows=rows),
        out_shape=jax.ShapeDtypeStruct((3, N_HEADS_A, Q_TOK, WIN_BLOCKS * Q_TOK), F32),
        grid=(N_HEADS_A,),
        in_specs=[pl.BlockSpec(memory_space=pltpu.SMEM)],
        out_specs=pl.BlockSpec((3, 1, Q_TOK, WIN_BLOCKS * Q_TOK), lambda h: (0, h, 0, 0)),
        compiler_params=pltpu.CompilerParams(
            dimension_semantics=("arbitrary",), vmem_limit_bytes=VMEM_LIMIT),
        name="nbr_bias",
    )(rpb.reshape(-1))


def _mixout_kernel(x_ref, a_ref, u_ref, vg_ref, mod_ref, gn_ref, ws_ref, bs_ref,
                   na_ref, nb_ref, wo_ref, o_ref, gm_ref):
    tm = x_ref.shape[0]
    for g in range(N_GROUPS_B):
        cols = slice(g * GROUP_DIM_B, (g + 1) * GROUP_DIM_B)
        vn = _rms(_gelu_tanh(vg_ref[:, cols].astype(F32)), gn_ref[:, cols]).astype(BF16)
        ug = _gelu_tanh(u_ref[:, cols].astype(F32))
        w = ws_ref[g]
        b = bs_ref[:, g:g + 1]
        for c in range(tm // CHUNK):
            rows = slice(c * CHUNK, (c + 1) * CHUNK)
            gm_ref[rows, cols] = ug[rows] * (_dot(w, vn[rows]) + b)
    ya = _rms(a_ref[...].astype(F32), na_ref[...]).astype(BF16)
    yb = _rms(gm_ref[...], nb_ref[...]).astype(BF16)
    y = _dot(ya, wo_ref[:WIDTH_A, :]) + _dot(yb, wo_ref[WIDTH_A:, :])
    o_ref[...] = x_ref[...] + mod_ref[0, 5:6, :] * y


def _mixout(x, a, u, vg, mod, gmlp_norm, w_s, b_s_t, na_g, nb_g, w_out, *, rows_per_mod, tm=512):
    t = x.shape[0]
    tok = lambda w: pl.BlockSpec((tm, w), lambda i: (i, 0))
    full = lambda shape: pl.BlockSpec(shape, lambda i: (0,) * len(shape))
    return pl.pallas_call(
        _mixout_kernel,
        out_shape=jax.ShapeDtypeStruct((t, D_MODEL), F32),
        grid=(t // tm,),
        in_specs=[
            tok(D_MODEL), tok(WIDTH_A), tok(WIDTH_B), tok(WIDTH_B),
            pl.BlockSpec((1, N_MOD, D_MODEL), lambda i: (i * tm // rows_per_mod, 0, 0)),
            full((1, WIDTH_B)),
            full((N_GROUPS_B, CHUNK, CHUNK)),
            full((CHUNK, N_GROUPS_B)),
            full((1, WIDTH_A)),
            full((1, WIDTH_B)),
            pl.BlockSpec((MIX_WIDTH, D_MODEL), lambda i: (0, 0), pipeline_mode=pl.Buffered(1)),
        ],
        out_specs=tok(D_MODEL),
        scratch_shapes=[pltpu.VMEM((tm, WIDTH_B), F32)],
        compiler_params=pltpu.CompilerParams(
            dimension_semantics=("parallel",), vmem_limit_bytes=VMEM_LIMIT),
        name="mixout",
    )(x, a, u, vg, mod, gmlp_norm, w_s, b_s_t, na_g, nb_g, w_out)


def kernel(x_prompt, x_sample, cache_k, cache_v, c, c_ctx, w_ada, b_ada, ffn1_norm, ffn1_w_gate, ffn1_w_up, ffn1_w_down, mix_norm, w_in, rpb, gmlp_norm, w_s, b_s, out_norm_a, out_norm_b, w_out, ffn2_norm, ffn2_w_gate, ffn2_w_up, ffn2_w_down, final_norm):
    batch, seq, _ = x_prompt.shape
    dec_batch, dec_seq, _ = x_sample.shape
    depth = w_ada.shape[0]
    x_ctx = x_prompt.reshape(batch * seq, D_MODEL)
    x_lat = x_sample.reshape(dec_batch * dec_seq, D_MODEL)
    final_g = final_norm.reshape(1, D_MODEL)

    cvec = jnp.concatenate([c_ctx[None, :], c], axis=0)
    mod_rows = -(-cvec.shape[0] // 8) * 8
    cvec = jnp.pad(cvec, ((0, mod_rows - cvec.shape[0]), (0, 0)))

    new_k, new_v = [], []
    for l in range(depth):
        last = l == depth - 1
        mod = _adaln(cvec, w_ada[l], b_ada[l][None, :]).reshape(mod_rows, N_MOD, D_MODEL)
        mods = (mod[0:1], mod[1:1 + dec_batch])
        rows_per_mod = (batch * seq, dec_seq)

        f1 = (ffn1_norm[l][None, :], ffn1_w_gate[l].astype(BF16), ffn1_w_up[l].astype(BF16),
              ffn1_w_down[l].astype(BF16))
        mix_g = mix_norm[l][None, :]
        w_s_l = w_s[l].astype(BF16)
        b_s_t = b_s[l].T
        gn = gmlp_norm[l][None, :]
        na_g = out_norm_a[l][None, :]
        nb_g = out_norm_b[l][None, :]
        bias = _bias_table(rpb[l], dec_seq // GRID_W)
        ck = cache_k[:, l].reshape(dec_batch, -1, HEAD_DIM_A)
        cv = cache_v[:, l].reshape(dec_batch, -1, HEAD_DIM_A)

        n_j = D_FF // FFN_TF
        n_ctx, n_lat = x_ctx.shape[0] // FFN_TM, x_lat.shape[0] // FFN_TM
        x_ctx, w_in_l, w_out_l = _ffn(
            x_ctx, mods[0], *f1, mod_base=0, rows_per_mod=rows_per_mod[0],
            side=(_cast_plan(w_in[l], n_ctx, n_j), _cast_plan(w_out[l], n_ctx, n_j)))
        x_lat, *f2_w = _ffn(
            x_lat, mods[1], *f1, mod_base=0, rows_per_mod=rows_per_mod[1],
            side=(_cast_plan(ffn2_w_gate[l], n_lat, n_j), _cast_plan(ffn2_w_up[l], n_lat, n_j),
                  _cast_plan_t(ffn2_w_down[l], n_lat, n_j)))
        f2 = (ffn2_norm[l][None, :], *f2_w)

        xs = []
        for path, (x, m, rpm) in enumerate(zip((x_ctx, x_lat), mods, rows_per_mod)):
            q, k, v, u, vg = _mixin(x, m, mix_g, w_in_l, rows_per_mod=rpm,
                                    kv_dtype=F32 if path == 0 else BF16)
            if path == 0:
                a = _ctx_attention(q, k, v, seq)
                new_k.append(k.reshape(batch, seq, N_HEADS_A, HEAD_DIM_A))
                new_v.append(v.reshape(batch, seq, N_HEADS_A, HEAD_DIM_A))
            else:
                a = _nbr_attention(q, k, v, ck, cv, bias, dec_seq)
            x = _mixout(x, a, u, vg, m, gn, w_s_l, b_s_t, na_g, nb_g, w_out_l, rows_per_mod=rpm)
            x, = _ffn(x, m, *f2, mod_base=6, rows_per_mod=rpm,
                      final_g=final_g if last else None)
            xs.append(x)
        x_ctx, x_lat = xs

    if depth == 0:
        raise ValueError("depth must be positive")
    y_prompt = x_ctx.reshape(batch, seq, D_MODEL)
    y_sample = x_lat.reshape(dec_batch, dec_seq, D_MODEL)
    return (y_prompt, y_sample, jnp.stack(new_k, axis=1), jnp.stack(new_v, axis=1))
```

```python
import functools
import math
from typing import NamedTuple

import jax
import jax.numpy as jnp
from jax import lax
from jax.experimental import pallas as pl
from jax.experimental.pallas import tpu as pltpu

D_MODEL = 2048
N_HEADS_A = 16
HEAD_DIM_A = 64
WIDTH_A = N_HEADS_A * HEAD_DIM_A
GRID_W = 64
NA_ROWS = 8
NA_COLS = 16
N_GROUPS_B = 8
GROUP_DIM_B = 128
WIDTH_B = N_GROUPS_B * GROUP_DIM_B
CHUNK = 128
MIX_WIDTH = WIDTH_A + WIDTH_B
IN_WIDTH = 3 * WIDTH_A + 2 * WIDTH_B
D_FF = 5632
N_MOD = 9
EPS = 1e-6
NEG_INF = -1e30

LANES = 128
HEADS_PER_BLOCK = LANES // HEAD_DIM_A
N_HEAD_BLOCKS = N_HEADS_A // HEADS_PER_BLOCK
Q_ROWS = 4
Q_TOK = Q_ROWS * GRID_W
WIN_BLOCKS = 3
VMEM_LIMIT = 62 * 1024 * 1024
ROW_CHUNK = 16
NORM_UNROLL = 8
FFN_OUT_CHUNK = 512
FFN_TM = 1024
FFN_TF = 512
PAIRS_PER_ITER = 4
LOG2E = math.log2(math.e)
Q_SCALE = HEAD_DIM_A ** -0.5 * LOG2E

F32 = jnp.float32
BF16 = jnp.bfloat16


def _dot(a, b):
    return jnp.dot(a, b, preferred_element_type=F32)


def _dot_nt(a, b):
    return lax.dot_general(a, b, (((1,), (1,)), ((), ())), preferred_element_type=F32)


def _rms(x, g):
    ms = jnp.mean(x * x, axis=-1, keepdims=True)
    return x * lax.rsqrt(ms + EPS) * g


def _silu(x):
    return x * jax.nn.sigmoid(x)


def _gelu_tanh(x):
    c = 0.7978845608028654
    return 0.5 * x * (1.0 + jnp.tanh(c * (x + 0.044715 * (x * x * x))))


def _row_chunk(c):
    return pl.ds(pl.multiple_of(c * ROW_CHUNK, ROW_CHUNK), ROW_CHUNK)


def _lane_tile(v, width):
    return jnp.concatenate([v] * (width // LANES), axis=1)


def _row_inv_rms(x_ref, rs_ref):
    def body(c, carry):
        r = _row_chunk(c)
        x = x_ref[r, :]
        ms = jnp.mean(x * x, axis=-1, keepdims=True)
        rs_ref[r, :] = jnp.broadcast_to(lax.rsqrt(ms + EPS), (ROW_CHUNK, LANES))
        return carry

    lax.fori_loop(0, x_ref.shape[0] // ROW_CHUNK, body, 0, unroll=NORM_UNROLL)


def _adaln_kernel(c_ref, w_ref, b_ref, o_ref):
    s = _silu(c_ref[...]).astype(BF16)
    o_ref[...] = _dot(s, w_ref[...].astype(BF16)) + b_ref[...]


def _adaln(cvec, w_ada, b_ada, tn=1024):
    rows = cvec.shape[0]
    n = w_ada.shape[1]
    return pl.pallas_call(
        _adaln_kernel,
        out_shape=jax.ShapeDtypeStruct((rows, n), F32),
        grid=(n // tn,),
        in_specs=[
            pl.BlockSpec((rows, D_MODEL), lambda j: (0, 0)),
            pl.BlockSpec((D_MODEL, tn), lambda j: (0, j)),
            pl.BlockSpec((1, tn), lambda j: (0, j)),
        ],
        out_specs=pl.BlockSpec((rows, tn), lambda j: (0, j)),
        compiler_params=pltpu.CompilerParams(
            dimension_semantics=("arbitrary",), vmem_limit_bytes=VMEM_LIMIT),
        name="adaln",
    )(cvec, w_ada, b_ada)


def _ffn_kernel(*refs, mod_base, final, n_side):
    x_ref, mod_ref, g_ref, wg_ref, wu_ref, wd_ref = refs[:6]
    n_in = 7 if final else 6
    fg_ref = refs[6] if final else None
    side_in = refs[n_in:n_in + n_side]
    o_ref = refs[n_in + n_side]
    side_out = refs[n_in + n_side + 1:n_in + 2 * n_side + 1]
    h_ref, rs_ref = refs[n_in + 2 * n_side + 1:]
    j = pl.program_id(1)
    n_row_chunks = x_ref.shape[0] // ROW_CHUNK

    def step(acc_ref):
        for src_ref, dst_ref in zip(side_in, side_out):
            dst_ref[...] = src_ref[...].astype(BF16)
        h = h_ref[...]
        a = (_silu(_dot(h, wg_ref[...])) * _dot(h, wu_ref[...])).astype(BF16)
        half_gate = 0.5 * mod_ref[0, mod_base + 2:mod_base + 3, :]
        for n in range(D_MODEL // FFN_OUT_CHUNK):
            cols = slice(n * FFN_OUT_CHUNK, (n + 1) * FFN_OUT_CHUNK)
            o_ref[:, cols] = acc_ref[:, cols] + half_gate[:, cols] * _dot(a, wd_ref[:, cols])

    @pl.when(j == 0)
    def _():
        _row_inv_rms(x_ref, rs_ref)
        shift = mod_ref[0, mod_base:mod_base + 1, :]
        gain = g_ref[...] * (1.0 + mod_ref[0, mod_base + 1:mod_base + 2, :])

        def body(c, carry):
            r = _row_chunk(c)
            h_ref[r, :] = (x_ref[r, :] * _lane_tile(rs_ref[r, :], D_MODEL) * gain + shift).astype(BF16)
            return carry

        lax.fori_loop(0, n_row_chunks, body, 0, unroll=NORM_UNROLL)
        step(x_ref)

    @pl.when(j > 0)
    def _():
        step(o_ref)

    if final:
        @pl.when(j == pl.num_programs(1) - 1)
        def _():
            _row_inv_rms(o_ref, rs_ref)
            fg = fg_ref[...]

            def body(c, carry):
                r = _row_chunk(c)
                o_ref[r, :] = o_ref[r, :] * _lane_tile(rs_ref[r, :], D_MODEL) * fg
                return carry

            lax.fori_loop(0, n_row_chunks, body, 0, unroll=NORM_UNROLL)


def _cast_plan(w, n_i, n_j):
    rows, cols = w.shape
    col_blocks = max(d for d in range(1, n_j + 1) if cols % d == 0 and (cols // d) % LANES == 0)
    assert rows % n_i == 0 and (rows // n_i) % 16 == 0
    block = (rows // n_i, cols // col_blocks)
    return w, block, lambda i, j: (i, jnp.minimum(j, col_blocks - 1))


def _cast_plan_t(w, n_i, n_j):
    rows, cols = w.shape
    assert rows % n_j == 0 and (rows // n_j) % 16 == 0 and cols % n_i == 0 and (cols // n_i) % LANES == 0
    return w, (rows // n_j, cols // n_i), lambda i, j: (j, i)


def _ffn(x, mod, norm_g, wg, wu, wd, *, mod_base, rows_per_mod, final_g=None, side=(),
         tm=FFN_TM, tf=FFN_TF):
    t = x.shape[0]
    final = final_g is not None
    in_specs = [
        pl.BlockSpec((tm, D_MODEL), lambda i, j: (i, 0)),
        pl.BlockSpec((1, N_MOD, D_MODEL), lambda i, j: (i * tm // rows_per_mod, 0, 0)),
        pl.BlockSpec((1, D_MODEL), lambda i, j: (0, 0)),
        pl.BlockSpec((D_MODEL, tf), lambda i, j: (0, j)),
        pl.BlockSpec((D_MODEL, tf), lambda i, j: (0, j)),
        pl.BlockSpec((tf, D_MODEL), lambda i, j: (j, 0)),
    ]
    args = [x, mod, norm_g, wg, wu, wd]
    if final:
        in_specs.append(pl.BlockSpec((1, D_MODEL), lambda i, j: (0, 0)))
        args.append(final_g)
    out_shape = [jax.ShapeDtypeStruct((t, D_MODEL), F32)]
    out_specs = [pl.BlockSpec((tm, D_MODEL), lambda i, j: (i, 0))]
    for w, block, index_map in side:
        in_specs.append(pl.BlockSpec(block, index_map))
        args.append(w)
        out_shape.append(jax.ShapeDtypeStruct(w.shape, BF16))
        out_specs.append(pl.BlockSpec(block, index_map))
    return pl.pallas_call(
        functools.partial(_ffn_kernel, mod_base=mod_base, final=final, n_side=len(side)),
        out_shape=out_shape,
        grid=(t // tm, D_FF // tf),
        in_specs=in_specs,
        out_specs=out_specs,
        scratch_shapes=[pltpu.VMEM((tm, D_MODEL), BF16), pltpu.VMEM((tm, LANES), F32)],
        compiler_params=pltpu.CompilerParams(
            dimension_semantics=("arbitrary", "arbitrary"), vmem_limit_bytes=VMEM_LIMIT),
        name="ffn_final" if final else "ffn",
    )(*args)


def _mixin_kernel(x_ref, mod_ref, g_ref, w_ref, q_ref, k_ref, v_ref, u_ref, vg_ref, *, kv_seq):
    shift = mod_ref[0, 3:4, :]
    scale = mod_ref[0, 4:5, :]
    h = (_rms(x_ref[...], g_ref[...]) * (1.0 + scale) + shift).astype(BF16)
    for idx, o_ref in enumerate((q_ref, k_ref, v_ref, u_ref, vg_ref)):
        p = _dot(h, w_ref[:, idx * WIDTH_A:(idx + 1) * WIDTH_A])
        if idx == 0:
            p = p * Q_SCALE
        if kv_seq and idx in (1, 2):
            for e in range(p.shape[0] // kv_seq):
                o_ref[e] = p[e * kv_seq:(e + 1) * kv_seq, :].T
        else:
            o_ref[...] = p.astype(o_ref.dtype)


def _mixin(x, mod, norm_g, w_in, *, rows_per_mod, kv_seq=None, tm=512):
    t = x.shape[0]
    tok_shape = jax.ShapeDtypeStruct((t, WIDTH_A), BF16)
    tok_spec = pl.BlockSpec((tm, WIDTH_A), lambda i: (i, 0))
    if kv_seq:
        kv_shape = jax.ShapeDtypeStruct((t // kv_seq, WIDTH_A, kv_seq), F32)
        kv_spec = pl.BlockSpec((tm // kv_seq, WIDTH_A, kv_seq), lambda i: (i, 0, 0))
    else:
        kv_shape, kv_spec = tok_shape, tok_spec
    return pl.pallas_call(
        functools.partial(_mixin_kernel, kv_seq=kv_seq),
        out_shape=[tok_shape, kv_shape, kv_shape, tok_shape, tok_shape],
        grid=(t // tm,),
        in_specs=[
            pl.BlockSpec((tm, D_MODEL), lambda i: (i, 0)),
            pl.BlockSpec((1, N_MOD, D_MODEL), lambda i: (i * tm // rows_per_mod, 0, 0)),
            pl.BlockSpec((1, D_MODEL), lambda i: (0, 0)),
            pl.BlockSpec((D_MODEL, IN_WIDTH), lambda i: (0, 0), pipeline_mode=pl.Buffered(1)),
        ],
        out_specs=[tok_spec, kv_spec, kv_spec, tok_spec, tok_spec],
        compiler_params=pltpu.CompilerParams(
            dimension_semantics=("parallel",), vmem_limit_bytes=VMEM_LIMIT),
        name="mixin",
    )(x, mod, norm_g, w_in)


class _Seg(NamedTuple):
    k: jax.Array
    v: jax.Array
    feature_major: bool


def _attend_pair(q, segs, bias_fn):
    lane = lax.broadcasted_iota(jnp.int32, (1, LANES), 1)
    row = lax.broadcasted_iota(jnp.int32, (LANES, 1), 0)
    out = jnp.zeros(q.shape, F32)
    for hh in range(HEADS_PER_BLOCK):
        ones_idx = ((hh + 1) % HEADS_PER_BLOCK) * HEAD_DIM_A
        head = (lane // HEAD_DIM_A) == hh
        ones_lane = lane == ones_idx
        ones_row = jnp.where(ones_lane, 1.0, 0.0).astype(BF16)
        qh = jnp.where(head, q, jnp.zeros_like(q))
        scores = []
        for i, seg in enumerate(segs):
            s = _dot(qh, seg.k) if seg.feature_major else _dot_nt(qh, seg.k)
            b = bias_fn(hh, i)
            if b is not None:
                s = s + b
            scores.append(s)
        m = scores[0].max(axis=-1, keepdims=True)
        for s in scores[1:]:
            m = jnp.maximum(m, s.max(axis=-1, keepdims=True))
        acc = jnp.zeros(q.shape, F32)
        for s, seg in zip(scores, segs):
            p = jnp.exp2(s - m).astype(BF16)
            if seg.feature_major:
                ones_col = jnp.where(row == ones_idx, 1.0, 0.0)
                vt_aug = jnp.where((row // HEAD_DIM_A) == hh, seg.v, ones_col).astype(BF16)
                acc = acc + _dot_nt(p, vt_aug)
            else:
                v_aug = jnp.where(head, seg.v, jnp.broadcast_to(ones_row, seg.v.shape))
                acc = acc + _dot(p, v_aug)
        den = jnp.sum(jnp.where(ones_lane, acc, 0.0), axis=-1, keepdims=True)
        out = out + jnp.where(head, acc / den, 0.0)
    return out


def _pair_loop(pair_fn, pairs_per_iter):
    def body(it, carry):
        for sub in range(pairs_per_iter):
            p = it * pairs_per_iter + sub
            pair_fn(p, pl.ds(pl.multiple_of(p * LANES, LANES), LANES))
        return carry

    lax.fori_loop(0, N_HEAD_BLOCKS // pairs_per_iter, body, 0)


def _ctx_attn_kernel(q_ref, kt_ref, vt_ref, o_ref):
    def pair(p, cols):
        seg = _Seg(kt_ref[0, cols, :].astype(BF16), vt_ref[0, cols, :], True)
        o = _attend_pair(q_ref[:, cols], [seg], lambda hh, i: None)
        o_ref[:, cols] = o.astype(o_ref.dtype)

    _pair_loop(pair, N_HEAD_BLOCKS)


def _ctx_attention(q, kt, vt, seq):
    t = q.shape[0]
    qspec = pl.BlockSpec((seq, WIDTH_A), lambda b: (b, 0))
    tspec = pl.BlockSpec((1, WIDTH_A, seq), lambda b: (b, 0, 0))
    return pl.pallas_call(
        _ctx_attn_kernel,
        out_shape=jax.ShapeDtypeStruct((t, WIDTH_A), BF16),
        grid=(t // seq,),
        in_specs=[qspec, tspec, tspec],
        out_specs=qspec,
        compiler_params=pltpu.CompilerParams(
            dimension_semantics=("parallel",), vmem_limit_bytes=VMEM_LIMIT),
        name="ctx_attn",
    )(q, kt, vt)


def _nbr_attn_kernel(q_ref, k0_ref, k1_ref, k2_ref, v0_ref, v1_ref, v2_ref,
                     ckt_ref, cvt_ref, bias_ref, o_ref):
    def pair(p, cols):
        segs = [_Seg(k_ref[:, cols], v_ref[:, cols], False)
                for k_ref, v_ref in ((k0_ref, v0_ref), (k1_ref, v1_ref), (k2_ref, v2_ref))]
        segs.append(_Seg(ckt_ref[0, cols, :].astype(BF16), cvt_ref[0, cols, :], True))

        def bias_fn(hh, i):
            if i >= WIN_BLOCKS:
                return None
            return bias_ref[0, HEADS_PER_BLOCK * p + hh, :, i * Q_TOK:(i + 1) * Q_TOK]

        o = _attend_pair(q_ref[:, cols], segs, bias_fn)
        o_ref[:, cols] = o.astype(o_ref.dtype)

    _pair_loop(pair, PAIRS_PER_ITER)


def _nbr_attention(q, k, v, ckt, cvt, bias, n_tok):
    t = q.shape[0]
    batch = t // n_tok
    past = ckt.shape[2]
    qb = n_tok // Q_TOK
    def win(m):
        return jnp.clip(m - 1, 0, qb - WIN_BLOCKS)
    def variant(m):
        return jnp.minimum(m, 1) + jnp.maximum(m - (qb - 2), 0)
    qspec = pl.BlockSpec((Q_TOK, WIDTH_A), lambda m, b: (b * qb + m, 0))
    def kvspec(w):
        return pl.BlockSpec((Q_TOK, WIDTH_A), lambda m, b: (b * qb + win(m) + w, 0))
    cspec = pl.BlockSpec((1, WIDTH_A, past), lambda m, b: (b, 0, 0))
    bspec = pl.BlockSpec((1, N_HEADS_A, Q_TOK, WIN_BLOCKS * Q_TOK),
                         lambda m, b: (variant(m), 0, 0, 0))
    return pl.pallas_call(
        _nbr_attn_kernel,
        out_shape=jax.ShapeDtypeStruct((t, WIDTH_A), BF16),
        grid=(qb, batch),
        in_specs=[qspec] + [kvspec(w) for w in range(WIN_BLOCKS)]
                 + [kvspec(w) for w in range(WIN_BLOCKS)] + [cspec, cspec, bspec],
        out_specs=qspec,
        compiler_params=pltpu.CompilerParams(
            dimension_semantics=("parallel", "arbitrary"), vmem_limit_bytes=VMEM_LIMIT),
        name="nbr_attn",
    )(q, k, k, k, v, v, v, ckt, cvt, bias)


def _bias_kernel(rpb_ref, o_ref, *, rows):
    h = pl.program_id(0)
    n_dr = 2 * NA_ROWS - 1
    n_dc = 2 * NA_COLS - 1
    qc = lax.broadcasted_iota(jnp.int32, (GRID_W, LANES), 0)
    ln = lax.broadcasted_iota(jnp.int32, (GRID_W, LANES), 1)
    kc = ln % GRID_W
    dc_idx = jnp.clip(kc - qc + NA_COLS - 1, 0, n_dc - 1)
    col_start = jnp.clip(qc - NA_COLS // 2, 0, GRID_W - NA_COLS)
    col_ok = (kc >= col_start) & (kc < col_start + NA_COLS)
    neg = jnp.full((GRID_W, LANES), NEG_INF, F32)
    toeplitz = []
    for dr in range(n_dr):
        acc = jnp.zeros((GRID_W, LANES), F32)
        for dc in range(n_dc):
            acc = jnp.where(dc_idx == dc, rpb_ref[h * (n_dr * n_dc) + dr * n_dc + dc], acc)
        toeplitz.append(jnp.where(col_ok, acc * LOG2E, neg))
    qb = rows // Q_ROWS
    kr = min(NA_ROWS, rows)
    for var, m in enumerate((0, 1, qb - 1)):
        wb = min(max(m - 1, 0), qb - WIN_BLOCKS)
        for a in range(Q_ROWS):
            r = Q_ROWS * m + a
            row_start = min(max(r - kr // 2, 0), rows - kr)
            for tp in range(WIN_BLOCKS * Q_ROWS // 2):
                halves = []
                for t in (2 * tp, 2 * tp + 1):
                    key_row = Q_ROWS * wb + t
                    ok = row_start <= key_row < row_start + kr
                    halves.append(toeplitz[key_row - r + NA_ROWS - 1] if ok else neg)
                tile = jnp.where(ln < GRID_W, halves[0], halves[1])
                o_ref[var, 0, a * GRID_W:(a + 1) * GRID_W, tp * LANES:(tp + 1) * LANES] = tile


def _bias_table(rpb, rows):
    return pl.pallas_call(
        functools.partial(_bias_kernel, rows=rows),
        out_shape=jax.ShapeDtypeStruct((3, N_HEADS_A, Q_TOK, WIN_BLOCKS * Q_TOK), F32),
        grid=(N_HEADS_A,),
        in_specs=[pl.BlockSpec(memory_space=pltpu.SMEM)],
        out_specs=pl.BlockSpec((3, 1, Q_TOK, WIN_BLOCKS * Q_TOK), lambda h: (0, h, 0, 0)),
        compiler_params=pltpu.CompilerParams(
            dimension_semantics=("arbitrary",), vmem_limit_bytes=VMEM_LIMIT),
        name="nbr_bias",
    )(rpb.reshape(-1))


def _mixout_kernel(x_ref, a_ref, u_ref, vg_ref, mod_ref, gn_ref, ws_ref, bs_ref,
                   na_ref, nb_ref, wo_ref, o_ref, gm_ref):
    tm = x_ref.shape[0]
    for g in range(N_GROUPS_B):
        cols = slice(g * GROUP_DIM_B, (g + 1) * GROUP_DIM_B)
        vn = _rms(_gelu_tanh(vg_ref[:, cols].astype(F32)), gn_ref[:, cols]).astype(BF16)
        ug = _gelu_tanh(u_ref[:, cols].astype(F32))
        w = ws_ref[g]
        b = bs_ref[:, g:g + 1]
        for c in range(tm // CHUNK):
            rows = slice(c * CHUNK, (c + 1) * CHUNK)
            gm_ref[rows, cols] = ug[rows] * (_dot(w, vn[rows]) + b)
    ya = _rms(a_ref[...].astype(F32), na_ref[...]).astype(BF16)
    yb = _rms(gm_ref[...], nb_ref[...]).astype(BF16)
    y = _dot(ya, wo_ref[:WIDTH_A, :]) + _dot(yb, wo_ref[WIDTH_A:, :])
    o_ref[...] = x_ref[...] + mod_ref[0, 5:6, :] * y


def _mixout(x, a, u, vg, mod, gmlp_norm, w_s, b_s_t, na_g, nb_g, w_out, *, rows_per_mod, tm=512):
    t = x.shape[0]
    tok = lambda w: pl.BlockSpec((tm, w), lambda i: (i, 0))
    full = lambda shape: pl.BlockSpec(shape, lambda i: (0,) * len(shape))
    return pl.pallas_call(
        _mixout_kernel,
        out_shape=jax.ShapeDtypeStruct((t, D_MODEL), F32),
        grid=(t // tm,),
        in_specs=[
            tok(D_MODEL), tok(WIDTH_A), tok(WIDTH_B), tok(WIDTH_B),
            pl.BlockSpec((1, N_MOD, D_MODEL), lambda i: (i * tm // rows_per_mod, 0, 0)),
            full((1, WIDTH_B)),
            full((N_GROUPS_B, CHUNK, CHUNK)),
            full((CHUNK, N_GROUPS_B)),
            full((1, WIDTH_A)),
            full((1, WIDTH_B)),
            pl.BlockSpec((MIX_WIDTH, D_MODEL), lambda i: (0, 0), pipeline_mode=pl.Buffered(1)),
        ],
        out_specs=tok(D_MODEL),
        scratch_shapes=[pltpu.VMEM((tm, WIDTH_B), F32)],
        compiler_params=pltpu.CompilerParams(
            dimension_semantics=("parallel",), vmem_limit_bytes=VMEM_LIMIT),
        name="mixout",
    )(x, a, u, vg, mod, gmlp_norm, w_s, b_s_t, na_g, nb_g, w_out)


def kernel(x_prompt, x_sample, cache_k, cache_v, c, c_ctx, w_ada, b_ada, ffn1_norm, ffn1_w_gate, ffn1_w_up, ffn1_w_down, mix_norm, w_in, rpb, gmlp_norm, w_s, b_s, out_norm_a, out_norm_b, w_out, ffn2_norm, ffn2_w_gate, ffn2_w_up, ffn2_w_down, final_norm):
    batch, seq, _ = x_prompt.shape
    dec_batch, dec_seq, _ = x_sample.shape
    depth = w_ada.shape[0]
    x_ctx = x_prompt.reshape(batch * seq, D_MODEL)
    x_lat = x_sample.reshape(dec_batch * dec_seq, D_MODEL)
    final_g = final_norm.reshape(1, D_MODEL)

    cvec = jnp.concatenate([c_ctx[None, :], c], axis=0)
    mod_rows = -(-cvec.shape[0] // 8) * 8
    cvec = jnp.pad(cvec, ((0, mod_rows - cvec.shape[0]), (0, 0)))

    new_k, new_v = [], []
    for l in range(depth):
        last = l == depth - 1
        mod = _adaln(cvec, w_ada[l], b_ada[l][None, :]).reshape(mod_rows, N_MOD, D_MODEL)
        mods = (mod[0:1], mod[1:1 + dec_batch])
        rows_per_mod = (batch * seq, dec_seq)

        f1 = (ffn1_norm[l][None, :], ffn1_w_gate[l].astype(BF16), ffn1_w_up[l].astype(BF16),
              ffn1_w_down[l].astype(BF16))
        mix_g = mix_norm[l][None, :]
        w_s_l = w_s[l].astype(BF16)
        b_s_t = b_s[l].T
        gn = gmlp_norm[l][None, :]
        na_g = out_norm_a[l][None, :]
        nb_g = out_norm_b[l][None, :]
        bias = _bias_table(rpb[l], dec_seq // GRID_W)
        ckt = jnp.transpose(cache_k[:, l], (0, 2, 3, 1)).reshape(dec_batch, WIDTH_A, -1)
        cvt = jnp.transpose(cache_v[:, l], (0, 2, 3, 1)).reshape(dec_batch, WIDTH_A, -1)

        n_j = D_FF // FFN_TF
        n_ctx, n_lat = x_ctx.shape[0] // FFN_TM, x_lat.shape[0] // FFN_TM
        x_ctx, w_in_l, w_out_l = _ffn(
            x_ctx, mods[0], *f1, mod_base=0, rows_per_mod=rows_per_mod[0],
            side=(_cast_plan(w_in[l], n_ctx, n_j), _cast_plan(w_out[l], n_ctx, n_j)))
        x_lat, *f2_w = _ffn(
            x_lat, mods[1], *f1, mod_base=0, rows_per_mod=rows_per_mod[1],
            side=(_cast_plan(ffn2_w_gate[l], n_lat, n_j), _cast_plan(ffn2_w_up[l], n_lat, n_j),
                  _cast_plan_t(ffn2_w_down[l], n_lat, n_j)))
        f2 = (ffn2_norm[l][None, :], *f2_w)

        xs = []
        for path, (x, m, rpm) in enumerate(zip((x_ctx, x_lat), mods, rows_per_mod)):
            q, k, v, u, vg = _mixin(x, m, mix_g, w_in_l, rows_per_mod=rpm,
                                    kv_seq=seq if path == 0 else None)
            if path == 0:
                a = _ctx_attention(q, k, v, seq)
                for store, kt in ((new_k, k), (new_v, v)):
                    kt = kt.reshape(batch, N_HEADS_A, HEAD_DIM_A, seq)
                    store.append(jnp.transpose(kt, (0, 3, 1, 2)))
            else:
                a = _nbr_attention(q, k, v, ckt, cvt, bias, dec_seq)
            x = _mixout(x, a, u, vg, m, gn, w_s_l, b_s_t, na_g, nb_g, w_out_l, rows_per_mod=rpm)
            x, = _ffn(x, m, *f2, mod_base=6, rows_per_mod=rpm,
                      final_g=final_g if last else None)
            xs.append(x)
        x_ctx, x_lat = xs

    if depth == 0:
        raise ValueError("depth must be positive")
    y_prompt = x_ctx.reshape(batch, seq, D_MODEL)
    y_sample = x_lat.reshape(dec_batch, dec_seq, D_MODEL)
    return (y_prompt, y_sample, jnp.stack(new_k, axis=1), jnp.stack(new_v, axis=1))
```

```python
import functools
import math
from typing import NamedTuple

import jax
import jax.numpy as jnp
from jax import lax
from jax.experimental import pallas as pl
from jax.experimental.pallas import tpu as pltpu

D_MODEL = 2048
N_HEADS_A = 16
HEAD_DIM_A = 64
WIDTH_A = N_HEADS_A * HEAD_DIM_A
GRID_W = 64
NA_ROWS = 8
NA_COLS = 16
N_GROUPS_B = 8
GROUP_DIM_B = 128
WIDTH_B = N_GROUPS_B * GROUP_DIM_B
CHUNK = 128
MIX_WIDTH = WIDTH_A + WIDTH_B
IN_WIDTH = 3 * WIDTH_A + 2 * WIDTH_B
D_FF = 5632
N_MOD = 9
EPS = 1e-6
NEG_INF = -1e30

LANES = 128
HEADS_PER_BLOCK = LANES // HEAD_DIM_A
N_HEAD_BLOCKS = N_HEADS_A // HEADS_PER_BLOCK
Q_ROWS = 4
Q_TOK = Q_ROWS * GRID_W
WIN_BLOCKS = 3
VMEM_LIMIT = 62 * 1024 * 1024
ROW_CHUNK = 16
NORM_UNROLL = 8
FFN_OUT_CHUNK = 512
FFN_TM = 1024
FFN_TF = 512
PAIRS_PER_ITER = 8
MIXOUT_GROUPS_PER_DOT = 4
LOG2E = math.log2(math.e)
Q_SCALE = HEAD_DIM_A ** -0.5 * LOG2E

F32 = jnp.float32
BF16 = jnp.bfloat16


def _dot(a, b):
    return jnp.dot(a, b, preferred_element_type=F32)


def _dot_nt(a, b):
    return lax.dot_general(a, b, (((1,), (1,)), ((), ())), preferred_element_type=F32)


def _rms(x, g):
    ms = jnp.mean(x * x, axis=-1, keepdims=True)
    return x * lax.rsqrt(ms + EPS) * g


def _silu(x):
    return x * jax.nn.sigmoid(x)


def _gelu_tanh(x):
    c = 0.7978845608028654
    return 0.5 * x * (1.0 + jnp.tanh(c * (x + 0.044715 * (x * x * x))))


def _row_chunk(c):
    return pl.ds(pl.multiple_of(c * ROW_CHUNK, ROW_CHUNK), ROW_CHUNK)


def _lane_tile(v, width):
    return jnp.concatenate([v] * (width // LANES), axis=1)


def _row_inv_rms(x_ref, rs_ref):
    def body(c, carry):
        r = _row_chunk(c)
        x = x_ref[r, :]
        ms = jnp.mean(x * x, axis=-1, keepdims=True)
        rs_ref[r, :] = jnp.broadcast_to(lax.rsqrt(ms + EPS), (ROW_CHUNK, LANES))
        return carry

    lax.fori_loop(0, x_ref.shape[0] // ROW_CHUNK, body, 0, unroll=NORM_UNROLL)


def _adaln_kernel(c_ref, w_ref, b_ref, o_ref):
    s = _silu(c_ref[...]).astype(BF16)
    o_ref[...] = _dot(s, w_ref[...].astype(BF16)) + b_ref[...]


def _adaln(cvec, w_ada, b_ada, tn=1024):
    rows = cvec.shape[0]
    n = w_ada.shape[1]
    return pl.pallas_call(
        _adaln_kernel,
        out_shape=jax.ShapeDtypeStruct((rows, n), F32),
        grid=(n // tn,),
        in_specs=[
            pl.BlockSpec((rows, D_MODEL), lambda j: (0, 0)),
            pl.BlockSpec((D_MODEL, tn), lambda j: (0, j)),
            pl.BlockSpec((1, tn), lambda j: (0, j)),
        ],
        out_specs=pl.BlockSpec((rows, tn), lambda j: (0, j)),
        compiler_params=pltpu.CompilerParams(
            dimension_semantics=("arbitrary",), vmem_limit_bytes=VMEM_LIMIT),
        name="adaln",
    )(cvec, w_ada, b_ada)


def _ffn_kernel(*refs, mod_base, final, n_side):
    x_ref, mod_ref, g_ref, wg_ref, wu_ref, wd_ref = refs[:6]
    n_in = 7 if final else 6
    fg_ref = refs[6] if final else None
    side_in = refs[n_in:n_in + n_side]
    o_ref = refs[n_in + n_side]
    side_out = refs[n_in + n_side + 1:n_in + 2 * n_side + 1]
    h_ref, rs_ref = refs[n_in + 2 * n_side + 1:]
    j = pl.program_id(1)
    n_row_chunks = x_ref.shape[0] // ROW_CHUNK

    def step(acc_ref):
        for src_ref, dst_ref in zip(side_in, side_out):
            dst_ref[...] = src_ref[...].astype(BF16)
        h = h_ref[...]
        a = (_silu(_dot(h, wg_ref[...])) * _dot(h, wu_ref[...])).astype(BF16)
        half_gate = 0.5 * mod_ref[0, mod_base + 2:mod_base + 3, :]
        for n in range(D_MODEL // FFN_OUT_CHUNK):
            cols = slice(n * FFN_OUT_CHUNK, (n + 1) * FFN_OUT_CHUNK)
            o_ref[:, cols] = acc_ref[:, cols] + half_gate[:, cols] * _dot(a, wd_ref[:, cols])

    @pl.when(j == 0)
    def _():
        _row_inv_rms(x_ref, rs_ref)
        shift = mod_ref[0, mod_base:mod_base + 1, :]
        gain = g_ref[...] * (1.0 + mod_ref[0, mod_base + 1:mod_base + 2, :])

        def body(c, carry):
            r = _row_chunk(c)
            h_ref[r, :] = (x_ref[r, :] * _lane_tile(rs_ref[r, :], D_MODEL) * gain + shift).astype(BF16)
            return carry

        lax.fori_loop(0, n_row_chunks, body, 0, unroll=NORM_UNROLL)
        step(x_ref)

    @pl.when(j > 0)
    def _():
        step(o_ref)

    if final:
        @pl.when(j == pl.num_programs(1) - 1)
        def _():
            _row_inv_rms(o_ref, rs_ref)
            fg = fg_ref[...]

            def body(c, carry):
                r = _row_chunk(c)
                o_ref[r, :] = o_ref[r, :] * _lane_tile(rs_ref[r, :], D_MODEL) * fg
                return carry

            lax.fori_loop(0, n_row_chunks, body, 0, unroll=NORM_UNROLL)


def _cast_plan(w, n_i, n_j):
    rows, cols = w.shape
    col_blocks = max(d for d in range(1, n_j + 1) if cols % d == 0 and (cols // d) % LANES == 0)
    assert rows % n_i == 0 and (rows // n_i) % 16 == 0
    block = (rows // n_i, cols // col_blocks)
    return w, block, lambda i, j: (i, jnp.minimum(j, col_blocks - 1))


def _cast_plan_t(w, n_i, n_j):
    rows, cols = w.shape
    assert rows % n_j == 0 and (rows // n_j) % 16 == 0 and cols % n_i == 0 and (cols // n_i) % LANES == 0
    return w, (rows // n_j, cols // n_i), lambda i, j: (j, i)


def _ffn(x, mod, norm_g, wg, wu, wd, *, mod_base, rows_per_mod, final_g=None, side=(),
         tm=FFN_TM, tf=FFN_TF):
    t = x.shape[0]
    final = final_g is not None
    in_specs = [
        pl.BlockSpec((tm, D_MODEL), lambda i, j: (i, 0)),
        pl.BlockSpec((1, N_MOD, D_MODEL), lambda i, j: (i * tm // rows_per_mod, 0, 0)),
        pl.BlockSpec((1, D_MODEL), lambda i, j: (0, 0)),
        pl.BlockSpec((D_MODEL, tf), lambda i, j: (0, j)),
        pl.BlockSpec((D_MODEL, tf), lambda i, j: (0, j)),
        pl.BlockSpec((tf, D_MODEL), lambda i, j: (j, 0)),
    ]
    args = [x, mod, norm_g, wg, wu, wd]
    if final:
        in_specs.append(pl.BlockSpec((1, D_MODEL), lambda i, j: (0, 0)))
        args.append(final_g)
    out_shape = [jax.ShapeDtypeStruct((t, D_MODEL), F32)]
    out_specs = [pl.BlockSpec((tm, D_MODEL), lambda i, j: (i, 0))]
    for w, block, index_map in side:
        in_specs.append(pl.BlockSpec(block, index_map))
        args.append(w)
        out_shape.append(jax.ShapeDtypeStruct(w.shape, BF16))
        out_specs.append(pl.BlockSpec(block, index_map))
    return pl.pallas_call(
        functools.partial(_ffn_kernel, mod_base=mod_base, final=final, n_side=len(side)),
        out_shape=out_shape,
        grid=(t // tm, D_FF // tf),
        in_specs=in_specs,
        out_specs=out_specs,
        scratch_shapes=[pltpu.VMEM((tm, D_MODEL), BF16), pltpu.VMEM((tm, LANES), F32)],
        compiler_params=pltpu.CompilerParams(
            dimension_semantics=("arbitrary", "arbitrary"), vmem_limit_bytes=VMEM_LIMIT),
        name="ffn_final" if final else "ffn",
    )(*args)


def _mixin_kernel(x_ref, mod_ref, g_ref, w_ref, q_ref, k_ref, v_ref, u_ref, vg_ref, *, kv_seq):
    shift = mod_ref[0, 3:4, :]
    scale = mod_ref[0, 4:5, :]
    h = (_rms(x_ref[...], g_ref[...]) * (1.0 + scale) + shift).astype(BF16)
    for idx, o_ref in enumerate((q_ref, k_ref, v_ref, u_ref, vg_ref)):
        p = _dot(h, w_ref[:, idx * WIDTH_A:(idx + 1) * WIDTH_A])
        if idx == 0:
            p = p * Q_SCALE
        if kv_seq and idx in (1, 2):
            for e in range(p.shape[0] // kv_seq):
                o_ref[e] = p[e * kv_seq:(e + 1) * kv_seq, :].T
        else:
            o_ref[...] = p.astype(o_ref.dtype)


def _mixin(x, mod, norm_g, w_in, *, rows_per_mod, kv_seq=None, tm=512):
    t = x.shape[0]
    tok_shape = jax.ShapeDtypeStruct((t, WIDTH_A), BF16)
    tok_spec = pl.BlockSpec((tm, WIDTH_A), lambda i: (i, 0))
    if kv_seq:
        kv_shape = jax.ShapeDtypeStruct((t // kv_seq, WIDTH_A, kv_seq), F32)
        kv_spec = pl.BlockSpec((tm // kv_seq, WIDTH_A, kv_seq), lambda i: (i, 0, 0))
    else:
        kv_shape, kv_spec = tok_shape, tok_spec
    return pl.pallas_call(
        functools.partial(_mixin_kernel, kv_seq=kv_seq),
        out_shape=[tok_shape, kv_shape, kv_shape, tok_shape, tok_shape],
        grid=(t // tm,),
        in_specs=[
            pl.BlockSpec((tm, D_MODEL), lambda i: (i, 0)),
            pl.BlockSpec((1, N_MOD, D_MODEL), lambda i: (i * tm // rows_per_mod, 0, 0)),
            pl.BlockSpec((1, D_MODEL), lambda i: (0, 0)),
            pl.BlockSpec((D_MODEL, IN_WIDTH), lambda i: (0, 0), pipeline_mode=pl.Buffered(1)),
        ],
        out_specs=[tok_spec, kv_spec, kv_spec, tok_spec, tok_spec],
        compiler_params=pltpu.CompilerParams(
            dimension_semantics=("parallel",), vmem_limit_bytes=VMEM_LIMIT),
        name="mixin",
    )(x, mod, norm_g, w_in)


class _Seg(NamedTuple):
    k: jax.Array
    v: jax.Array
    feature_major: bool


def _head_masks(hh):
    lane = lax.broadcasted_iota(jnp.int32, (1, LANES), 1)
    ones_idx = ((hh + 1) % HEADS_PER_BLOCK) * HEAD_DIM_A
    return (lane // HEAD_DIM_A) == hh, lane == ones_idx, ones_idx


def _head_scores(q, segs, biases, hh):
    head, _, _ = _head_masks(hh)
    qh = jnp.where(head, q, jnp.zeros_like(q))
    scores = []
    for seg, b in zip(segs, biases):
        s = _dot(qh, seg.k) if seg.feature_major else _dot_nt(qh, seg.k)
        scores.append(s if b is None else s + b)
    return scores


def _head_probs(scores):
    m = scores[0].max(axis=-1, keepdims=True)
    for s in scores[1:]:
        m = jnp.maximum(m, s.max(axis=-1, keepdims=True))
    return [jnp.exp2(s - m).astype(BF16) for s in scores]


def _head_output(probs, segs, hh):
    head, ones_lane, ones_idx = _head_masks(hh)
    row = lax.broadcasted_iota(jnp.int32, (LANES, 1), 0)
    ones_row = jnp.where(ones_lane, 1.0, 0.0).astype(BF16)
    acc = None
    for p, seg in zip(probs, segs):
        if seg.feature_major:
            ones_col = jnp.where(row == ones_idx, 1.0, 0.0)
            vt_aug = jnp.where((row // HEAD_DIM_A) == hh, seg.v, ones_col).astype(BF16)
            d = _dot_nt(p, vt_aug)
        else:
            d = _dot(p, jnp.where(head, seg.v, jnp.broadcast_to(ones_row, seg.v.shape)))
        acc = d if acc is None else acc + d
    den = jnp.sum(jnp.where(ones_lane, acc, 0.0), axis=-1, keepdims=True)
    return jnp.where(head, acc / den, 0.0)


def _attention_heads(q_ref, o_ref, seg_fn, bias_fn, pairs_per_iter):
    def body(it, carry):
        pairs = [it * pairs_per_iter + sub for sub in range(pairs_per_iter)]
        cols = [pl.ds(pl.multiple_of(p * LANES, LANES), LANES) for p in pairs]
        heads = [(i, hh) for i in range(pairs_per_iter) for hh in range(HEADS_PER_BLOCK)]
        operands = {}

        def load(i):
            if i not in operands:
                operands[i] = (q_ref[:, cols[i]], seg_fn(pairs[i], cols[i]))
            return operands[i]

        def scores_of(t):
            i, hh = heads[t]
            q, segs = load(i)
            return _head_scores(q, segs, [bias_fn(pairs[i], hh, k) for k in range(len(segs))], hh)

        scores, probs, out = {}, {}, None
        for t in range(len(heads) + 2):
            if t < len(heads):
                scores[t] = scores_of(t)
            if 1 <= t <= len(heads):
                probs[t - 1] = _head_probs(scores.pop(t - 1))
            if t >= 2:
                i, hh = heads[t - 2]
                o = _head_output(probs.pop(t - 2), load(i)[1], hh)
                out = o if hh == 0 else out + o
                if hh == HEADS_PER_BLOCK - 1:
                    o_ref[:, cols[i]] = out.astype(o_ref.dtype)
        return carry

    lax.fori_loop(0, N_HEAD_BLOCKS // pairs_per_iter, body, 0)


def _ctx_attn_kernel(q_ref, kt_ref, vt_ref, o_ref):
    def seg_fn(p, cols):
        return [_Seg(kt_ref[0, cols, :].astype(BF16), vt_ref[0, cols, :], True)]

    _attention_heads(q_ref, o_ref, seg_fn, lambda p, hh, i: None, N_HEAD_BLOCKS)


def _ctx_attention(q, kt, vt, seq):
    t = q.shape[0]
    qspec = pl.BlockSpec((seq, WIDTH_A), lambda b: (b, 0))
    tspec = pl.BlockSpec((1, WIDTH_A, seq), lambda b: (b, 0, 0))
    return pl.pallas_call(
        _ctx_attn_kernel,
        out_shape=jax.ShapeDtypeStruct((t, WIDTH_A), BF16),
        grid=(t // seq,),
        in_specs=[qspec, tspec, tspec],
        out_specs=qspec,
        compiler_params=pltpu.CompilerParams(
            dimension_semantics=("parallel",), vmem_limit_bytes=VMEM_LIMIT),
        name="ctx_attn",
    )(q, kt, vt)


def _nbr_attn_kernel(q_ref, k0_ref, k1_ref, k2_ref, v0_ref, v1_ref, v2_ref,
                     ckt_ref, cvt_ref, bias_ref, o_ref):
    def seg_fn(p, cols):
        segs = [_Seg(k_ref[:, cols], v_ref[:, cols], False)
                for k_ref, v_ref in ((k0_ref, v0_ref), (k1_ref, v1_ref), (k2_ref, v2_ref))]
        segs.append(_Seg(ckt_ref[0, cols, :].astype(BF16), cvt_ref[0, cols, :], True))
        return segs

    def bias_fn(p, hh, i):
        if i >= WIN_BLOCKS:
            return None
        return bias_ref[0, HEADS_PER_BLOCK * p + hh, :, i * Q_TOK:(i + 1) * Q_TOK]

    _attention_heads(q_ref, o_ref, seg_fn, bias_fn, PAIRS_PER_ITER)


def _nbr_attention(q, k, v, ckt, cvt, bias, n_tok):
    t = q.shape[0]
    batch = t // n_tok
    past = ckt.shape[2]
    qb = n_tok // Q_TOK
    def win(m):
        return jnp.clip(m - 1, 0, qb - WIN_BLOCKS)
    def variant(m):
        return jnp.minimum(m, 1) + jnp.maximum(m - (qb - 2), 0)
    qspec = pl.BlockSpec((Q_TOK, WIDTH_A), lambda m, b: (b * qb + m, 0))
    def kvspec(w):
        return pl.BlockSpec((Q_TOK, WIDTH_A), lambda m, b: (b * qb + win(m) + w, 0))
    cspec = pl.BlockSpec((1, WIDTH_A, past), lambda m, b: (b, 0, 0))
    bspec = pl.BlockSpec((1, N_HEADS_A, Q_TOK, WIN_BLOCKS * Q_TOK),
                         lambda m, b: (variant(m), 0, 0, 0))
    return pl.pallas_call(
        _nbr_attn_kernel,
        out_shape=jax.ShapeDtypeStruct((t, WIDTH_A), BF16),
        grid=(qb, batch),
        in_specs=[qspec] + [kvspec(w) for w in range(WIN_BLOCKS)]
                 + [kvspec(w) for w in range(WIN_BLOCKS)] + [cspec, cspec, bspec],
        out_specs=qspec,
        compiler_params=pltpu.CompilerParams(
            dimension_semantics=("parallel", "arbitrary"), vmem_limit_bytes=VMEM_LIMIT),
        name="nbr_attn",
    )(q, k, k, k, v, v, v, ckt, cvt, bias)


def _bias_kernel(rpb_ref, o_ref, *, rows):
    h = pl.program_id(0)
    n_dr = 2 * NA_ROWS - 1
    n_dc = 2 * NA_COLS - 1
    qc = lax.broadcasted_iota(jnp.int32, (GRID_W, LANES), 0)
    ln = lax.broadcasted_iota(jnp.int32, (GRID_W, LANES), 1)
    kc = ln % GRID_W
    dc_idx = jnp.clip(kc - qc + NA_COLS - 1, 0, n_dc - 1)
    col_start = jnp.clip(qc - NA_COLS // 2, 0, GRID_W - NA_COLS)
    col_ok = (kc >= col_start) & (kc < col_start + NA_COLS)
    neg = jnp.full((GRID_W, LANES), NEG_INF, F32)
    toeplitz = []
    for dr in range(n_dr):
        acc = jnp.zeros((GRID_W, LANES), F32)
        for dc in range(n_dc):
            acc = jnp.where(dc_idx == dc, rpb_ref[h * (n_dr * n_dc) + dr * n_dc + dc], acc)
        toeplitz.append(jnp.where(col_ok, acc * LOG2E, neg))
    qb = rows // Q_ROWS
    kr = min(NA_ROWS, rows)
    for var, m in enumerate((0, 1, qb - 1)):
        wb = min(max(m - 1, 0), qb - WIN_BLOCKS)
        for a in range(Q_ROWS):
            r = Q_ROWS * m + a
            row_start = min(max(r - kr // 2, 0), rows - kr)
            for tp in range(WIN_BLOCKS * Q_ROWS // 2):
                halves = []
                for t in (2 * tp, 2 * tp + 1):
                    key_row = Q_ROWS * wb + t
                    ok = row_start <= key_row < row_start + kr
                    halves.append(toeplitz[key_row - r + NA_ROWS - 1] if ok else neg)
                tile = jnp.where(ln < GRID_W, halves[0], halves[1])
                o_ref[var, 0, a * GRID_W:(a + 1) * GRID_W, tp * LANES:(tp + 1) * LANES] = tile


def _bias_table(rpb, rows):
    return pl.pallas_call(
        functools.partial(_bias_kernel, rows=rows),
        out_shape=jax.ShapeDtypeStruct((3, N_HEADS_A, Q_TOK, WIN_BLOCKS * Q_TOK), F32),
        grid=(N_HEADS_A,),
        in_specs=[pl.BlockSpec(memory_space=pltpu.SMEM)],
        out_specs=pl.BlockSpec((3, 1, Q_TOK, WIN_BLOCKS * Q_TOK), lambda h: (0, h, 0, 0)),
        compiler_params=pltpu.CompilerParams(
            dimension_semantics=("arbitrary",), vmem_limit_bytes=VMEM_LIMIT),
        name="nbr_bias",
    )(rpb.reshape(-1))


def _mixout_kernel(x_ref, a_ref, u_ref, vg_ref, mod_ref, gn_ref, ws_ref, bs_ref,
                   na_ref, nb_ref, wo_ref, o_ref):
    tm = x_ref.shape[0]
    gate = mod_ref[0, 5:6, :]
    ya = _rms(a_ref[...].astype(F32), na_ref[...]).astype(BF16)
    n_a_chunks = N_GROUPS_B // MIXOUT_GROUPS_PER_DOT
    a_cols = D_MODEL // n_a_chunks

    def attn_part(n):
        cols = slice(n * a_cols, (n + 1) * a_cols)
        o_ref[:, cols] = x_ref[:, cols] + gate[:, cols] * _dot(ya, wo_ref[:WIDTH_A, cols])

    def gated_group(g):
        cols = slice(g * GROUP_DIM_B, (g + 1) * GROUP_DIM_B)
        vn = _rms(_gelu_tanh(vg_ref[:, cols].astype(F32)), gn_ref[:, cols]).astype(BF16)
        ug = _gelu_tanh(u_ref[:, cols].astype(F32))
        w = ws_ref[g]
        b = bs_ref[:, g:g + 1]
        gm = jnp.concatenate(
            [ug[c * CHUNK:(c + 1) * CHUNK] * (_dot(w, vn[c * CHUNK:(c + 1) * CHUNK]) + b)
             for c in range(tm // CHUNK)], axis=0)
        return jnp.sum(gm * gm, axis=-1, keepdims=True), (gm * nb_ref[:, cols]).astype(BF16)

    ssq = jnp.zeros((tm, 1), F32)
    yb = jnp.zeros((tm, D_MODEL), F32)
    for n in range(n_a_chunks):
        attn_part(n)
        parts = []
        for g in range(n * MIXOUT_GROUPS_PER_DOT, (n + 1) * MIXOUT_GROUPS_PER_DOT):
            sq, part = gated_group(g)
            ssq = ssq + sq
            parts.append(part)
        k0 = WIDTH_A + n * MIXOUT_GROUPS_PER_DOT * GROUP_DIM_B
        yb = yb + _dot(jnp.concatenate(parts, axis=1),
                       wo_ref[k0:k0 + MIXOUT_GROUPS_PER_DOT * GROUP_DIM_B, :])
    o_ref[...] += gate * (lax.rsqrt(ssq * (1.0 / WIDTH_B) + EPS) * yb)


def _mixout(x, a, u, vg, mod, gmlp_norm, w_s, b_s_t, na_g, nb_g, w_out, *, rows_per_mod, tm=512):
    t = x.shape[0]
    tok = lambda w: pl.BlockSpec((tm, w), lambda i: (i, 0))
    full = lambda shape: pl.BlockSpec(shape, lambda i: (0,) * len(shape))
    return pl.pallas_call(
        _mixout_kernel,
        out_shape=jax.ShapeDtypeStruct((t, D_MODEL), F32),
        grid=(t // tm,),
        in_specs=[
            tok(D_MODEL), tok(WIDTH_A), tok(WIDTH_B), tok(WIDTH_B),
            pl.BlockSpec((1, N_MOD, D_MODEL), lambda i: (i * tm // rows_per_mod, 0, 0)),
            full((1, WIDTH_B)),
            full((N_GROUPS_B, CHUNK, CHUNK)),
            full((CHUNK, N_GROUPS_B)),
            full((1, WIDTH_A)),
            full((1, WIDTH_B)),
            pl.BlockSpec((MIX_WIDTH, D_MODEL), lambda i: (0, 0), pipeline_mode=pl.Buffered(1)),
        ],
        out_specs=tok(D_MODEL),
        compiler_params=pltpu.CompilerParams(
            dimension_semantics=("parallel",), vmem_limit_bytes=VMEM_LIMIT),
        name="mixout",
    )(x, a, u, vg, mod, gmlp_norm, w_s, b_s_t, na_g, nb_g, w_out)


def kernel(x_prompt, x_sample, cache_k, cache_v, c, c_ctx, w_ada, b_ada, ffn1_norm, ffn1_w_gate, ffn1_w_up, ffn1_w_down, mix_norm, w_in, rpb, gmlp_norm, w_s, b_s, out_norm_a, out_norm_b, w_out, ffn2_norm, ffn2_w_gate, ffn2_w_up, ffn2_w_down, final_norm):
    batch, seq, _ = x_prompt.shape
    dec_batch, dec_seq, _ = x_sample.shape
    depth = w_ada.shape[0]
    x_ctx = x_prompt.reshape(batch * seq, D_MODEL)
    x_lat = x_sample.reshape(dec_batch * dec_seq, D_MODEL)
    final_g = final_norm.reshape(1, D_MODEL)

    cvec = jnp.concatenate([c_ctx[None, :], c], axis=0)
    mod_rows = -(-cvec.shape[0] // 8) * 8
    cvec = jnp.pad(cvec, ((0, mod_rows - cvec.shape[0]), (0, 0)))

    new_k, new_v = [], []
    for l in range(depth):
        last = l == depth - 1
        mod = _adaln(cvec, w_ada[l], b_ada[l][None, :]).reshape(mod_rows, N_MOD, D_MODEL)
        mods = (mod[0:1], mod[1:1 + dec_batch])
        rows_per_mod = (batch * seq, dec_seq)

        f1 = (ffn1_norm[l][None, :], ffn1_w_gate[l].astype(BF16), ffn1_w_up[l].astype(BF16),
              ffn1_w_down[l].astype(BF16))
        mix_g = mix_norm[l][None, :]
        w_s_l = w_s[l].astype(BF16)
        b_s_t = b_s[l].T
        gn = gmlp_norm[l][None, :]
        na_g = out_norm_a[l][None, :]
        nb_g = out_norm_b[l][None, :]
        bias = _bias_table(rpb[l], dec_seq // GRID_W)
        ckt = jnp.transpose(cache_k[:, l], (0, 2, 3, 1)).reshape(dec_batch, WIDTH_A, -1)
        cvt = jnp.transpose(cache_v[:, l], (0, 2, 3, 1)).reshape(dec_batch, WIDTH_A, -1)

        n_j = D_FF // FFN_TF
        n_ctx, n_lat = x_ctx.shape[0] // FFN_TM, x_lat.shape[0] // FFN_TM
        x_ctx, w_in_l, w_out_l = _ffn(
            x_ctx, mods[0], *f1, mod_base=0, rows_per_mod=rows_per_mod[0],
            side=(_cast_plan(w_in[l], n_ctx, n_j), _cast_plan(w_out[l], n_ctx, n_j)))
        x_lat, *f2_w = _ffn(
            x_lat, mods[1], *f1, mod_base=0, rows_per_mod=rows_per_mod[1],
            side=(_cast_plan(ffn2_w_gate[l], n_lat, n_j), _cast_plan(ffn2_w_up[l], n_lat, n_j),
                  _cast_plan_t(ffn2_w_down[l], n_lat, n_j)))
        f2 = (ffn2_norm[l][None, :], *f2_w)

        xs = []
        for path, (x, m, rpm) in enumerate(zip((x_ctx, x_lat), mods, rows_per_mod)):
            q, k, v, u, vg = _mixin(x, m, mix_g, w_in_l, rows_per_mod=rpm,
                                    kv_seq=seq if path == 0 else None)
            if path == 0:
                a = _ctx_attention(q, k, v, seq)
                for store, kt in ((new_k, k), (new_v, v)):
                    kt = kt.reshape(batch, N_HEADS_A, HEAD_DIM_A, seq)
                    store.append(jnp.transpose(kt, (0, 3, 1, 2)))
            else:
                a = _nbr_attention(q, k, v, ckt, cvt, bias, dec_seq)
            x = _mixout(x, a, u, vg, m, gn, w_s_l, b_s_t, na_g, nb_g, w_out_l, rows_per_mod=rpm)
            x, = _ffn(x, m, *f2, mod_base=6, rows_per_mod=rpm,
                      final_g=final_g if last else None)
            xs.append(x)
        x_ctx, x_lat = xs

    if depth == 0:
        raise ValueError("depth must be positive")
    y_prompt = x_ctx.reshape(batch, seq, D_MODEL)
    y_sample = x_lat.reshape(dec_batch, dec_seq, D_MODEL)
    return (y_prompt, y_sample, jnp.stack(new_k, axis=1), jnp.stack(new_v, axis=1))
```

```python
import functools
import math
from typing import NamedTuple

import jax
import jax.numpy as jnp
from jax import lax
from jax.experimental import pallas as pl
from jax.experimental.pallas import tpu as pltpu

D_MODEL = 2048
N_HEADS_A = 16
HEAD_DIM_A = 64
WIDTH_A = N_HEADS_A * HEAD_DIM_A
GRID_W = 64
NA_ROWS = 8
NA_COLS = 16
N_GROUPS_B = 8
GROUP_DIM_B = 128
WIDTH_B = N_GROUPS_B * GROUP_DIM_B
CHUNK = 128
MIX_WIDTH = WIDTH_A + WIDTH_B
IN_WIDTH = 3 * WIDTH_A + 2 * WIDTH_B
D_FF = 5632
N_MOD = 9
EPS = 1e-6
NEG_INF = -1e30

LANES = 128
HEADS_PER_BLOCK = LANES // HEAD_DIM_A
N_HEAD_BLOCKS = N_HEADS_A // HEADS_PER_BLOCK
Q_ROWS = 4
Q_TOK = Q_ROWS * GRID_W
WIN_BLOCKS = 3
VMEM_LIMIT = 62 * 1024 * 1024
ROW_CHUNK = 16
NORM_UNROLL = 8
FFN_OUT_CHUNK = 512
FFN_TM = 1024
FFN_TF = 512
PAIRS_PER_ITER = 8
MIXOUT_GROUPS_PER_DOT = 4
LOG2E = math.log2(math.e)
Q_SCALE = HEAD_DIM_A ** -0.5 * LOG2E

F32 = jnp.float32
BF16 = jnp.bfloat16


def _dot(a, b):
    return jnp.dot(a, b, preferred_element_type=F32)


def _dot_nt(a, b):
    return lax.dot_general(a, b, (((1,), (1,)), ((), ())), preferred_element_type=F32)


def _rms(x, g):
    ms = jnp.mean(x * x, axis=-1, keepdims=True)
    return x * lax.rsqrt(ms + EPS) * g


def _silu(x):
    return x * jax.nn.sigmoid(x)


def _gelu_tanh(x):
    c = 0.7978845608028654
    return 0.5 * x * (1.0 + jnp.tanh(c * (x + 0.044715 * (x * x * x))))


def _row_chunk(c):
    return pl.ds(pl.multiple_of(c * ROW_CHUNK, ROW_CHUNK), ROW_CHUNK)


def _lane_tile(v, width):
    return jnp.concatenate([v] * (width // LANES), axis=1)


def _row_inv_rms(x_ref, rs_ref):
    def body(c, carry):
        r = _row_chunk(c)
        x = x_ref[r, :]
        ms = jnp.mean(x * x, axis=-1, keepdims=True)
        rs_ref[r, :] = jnp.broadcast_to(lax.rsqrt(ms + EPS), (ROW_CHUNK, LANES))
        return carry

    lax.fori_loop(0, x_ref.shape[0] // ROW_CHUNK, body, 0, unroll=NORM_UNROLL)


def _adaln_kernel(c_ref, w_ref, b_ref, o_ref):
    s = _silu(c_ref[...]).astype(BF16)
    o_ref[...] = _dot(s, w_ref[...].astype(BF16)) + b_ref[...]


def _adaln(cvec, w_ada, b_ada, tn=1024):
    rows = cvec.shape[0]
    n = w_ada.shape[1]
    return pl.pallas_call(
        _adaln_kernel,
        out_shape=jax.ShapeDtypeStruct((rows, n), F32),
        grid=(n // tn,),
        in_specs=[
            pl.BlockSpec((rows, D_MODEL), lambda j: (0, 0)),
            pl.BlockSpec((D_MODEL, tn), lambda j: (0, j)),
            pl.BlockSpec((1, tn), lambda j: (0, j)),
        ],
        out_specs=pl.BlockSpec((rows, tn), lambda j: (0, j)),
        compiler_params=pltpu.CompilerParams(
            dimension_semantics=("arbitrary",), vmem_limit_bytes=VMEM_LIMIT),
        name="adaln",
    )(cvec, w_ada, b_ada)


def _ffn_kernel(*refs, mod_base, final, n_side):
    x_ref, mod_ref, g_ref, wg_ref, wu_ref, wd_ref = refs[:6]
    n_in = 7 if final else 6
    fg_ref = refs[6] if final else None
    side_in = refs[n_in:n_in + n_side]
    o_ref = refs[n_in + n_side]
    side_out = refs[n_in + n_side + 1:n_in + 2 * n_side + 1]
    h_ref, rs_ref = refs[n_in + 2 * n_side + 1:]
    j = pl.program_id(1)
    n_row_chunks = x_ref.shape[0] // ROW_CHUNK

    def step(acc_ref):
        for src_ref, dst_ref in zip(side_in, side_out):
            if len(dst_ref.shape) == 3:
                dst_ref[0] = src_ref[...].astype(BF16)
            else:
                dst_ref[...] = src_ref[...].astype(BF16)
        h = h_ref[...]
        wg = wg_ref[0] if len(wg_ref.shape) == 3 else wg_ref[...]
        wu = wu_ref[0] if len(wu_ref.shape) == 3 else wu_ref[...]
        a = (_silu(_dot(h, wg)) * _dot(h, wu)).astype(BF16)
        half_gate = 0.5 * mod_ref[0, mod_base + 2:mod_base + 3, :]
        for n in range(D_MODEL // FFN_OUT_CHUNK):
            cols = slice(n * FFN_OUT_CHUNK, (n + 1) * FFN_OUT_CHUNK)
            o_ref[:, cols] = acc_ref[:, cols] + half_gate[:, cols] * _dot(a, wd_ref[:, cols])

    @pl.when(j == 0)
    def _():
        _row_inv_rms(x_ref, rs_ref)
        shift = mod_ref[0, mod_base:mod_base + 1, :]
        gain = g_ref[...] * (1.0 + mod_ref[0, mod_base + 1:mod_base + 2, :])

        def body(c, carry):
            r = _row_chunk(c)
            h_ref[r, :] = (x_ref[r, :] * _lane_tile(rs_ref[r, :], D_MODEL) * gain + shift).astype(BF16)
            return carry

        lax.fori_loop(0, n_row_chunks, body, 0, unroll=NORM_UNROLL)
        step(x_ref)

    @pl.when(j > 0)
    def _():
        step(o_ref)

    if final:
        @pl.when(j == pl.num_programs(1) - 1)
        def _():
            _row_inv_rms(o_ref, rs_ref)
            fg = fg_ref[...]

            def body(c, carry):
                r = _row_chunk(c)
                o_ref[r, :] = o_ref[r, :] * _lane_tile(rs_ref[r, :], D_MODEL) * fg
                return carry

            lax.fori_loop(0, n_row_chunks, body, 0, unroll=NORM_UNROLL)


def _cast_plan(w, n_i, n_j):
    rows, cols = w.shape
    col_blocks = max(d for d in range(1, n_j + 1) if cols % d == 0 and (cols // d) % LANES == 0)
    assert rows % n_i == 0 and (rows // n_i) % 16 == 0
    block = (rows // n_i, cols // col_blocks)
    index_map = lambda i, j: (i, jnp.minimum(j, col_blocks - 1))
    return w, block, index_map, w.shape, block, index_map


def _cast_plan_t(w, n_i, n_j):
    rows, cols = w.shape
    assert rows % n_j == 0 and (rows // n_j) % 16 == 0 and cols % n_i == 0 and (cols // n_i) % LANES == 0
    block = (rows // n_j, cols // n_i)
    index_map = lambda i, j: (j, i)
    return w, block, index_map, w.shape, block, index_map


def _cast_plan_tiled(w, n_i, n_j):
    rows, cols = w.shape
    assert rows % n_i == 0 and (rows // n_i) % 16 == 0 and cols % n_j == 0 and (cols // n_j) % LANES == 0
    tile = cols // n_j
    return (w, (rows // n_i, tile), lambda i, j: (i, j),
            (n_j, rows, tile), (1, rows // n_i, tile), lambda i, j: (j, i, 0))


def _ffn_in_weight_spec(w, tf):
    if w.ndim == 3:
        return pl.BlockSpec((1, D_MODEL, tf), lambda i, j: (j, 0, 0))
    return pl.BlockSpec((D_MODEL, tf), lambda i, j: (0, j))


def _ffn(x, mod, norm_g, wg, wu, wd, *, mod_base, rows_per_mod, final_g=None, side=(),
         tm=FFN_TM, tf=FFN_TF):
    t = x.shape[0]
    final = final_g is not None
    in_specs = [
        pl.BlockSpec((tm, D_MODEL), lambda i, j: (i, 0)),
        pl.BlockSpec((1, N_MOD, D_MODEL), lambda i, j: (i * tm // rows_per_mod, 0, 0)),
        pl.BlockSpec((1, D_MODEL), lambda i, j: (0, 0)),
        _ffn_in_weight_spec(wg, tf),
        _ffn_in_weight_spec(wu, tf),
        pl.BlockSpec((tf, D_MODEL), lambda i, j: (j, 0)),
    ]
    args = [x, mod, norm_g, wg, wu, wd]
    if final:
        in_specs.append(pl.BlockSpec((1, D_MODEL), lambda i, j: (0, 0)))
        args.append(final_g)
    out_shape = [jax.ShapeDtypeStruct((t, D_MODEL), F32)]
    out_specs = [pl.BlockSpec((tm, D_MODEL), lambda i, j: (i, 0))]
    for w, in_block, in_map, o_shape, o_block, o_map in side:
        in_specs.append(pl.BlockSpec(in_block, in_map))
        args.append(w)
        out_shape.append(jax.ShapeDtypeStruct(o_shape, BF16))
        out_specs.append(pl.BlockSpec(o_block, o_map))
    return pl.pallas_call(
        functools.partial(_ffn_kernel, mod_base=mod_base, final=final, n_side=len(side)),
        out_shape=out_shape,
        grid=(t // tm, D_FF // tf),
        in_specs=in_specs,
        out_specs=out_specs,
        scratch_shapes=[pltpu.VMEM((tm, D_MODEL), BF16), pltpu.VMEM((tm, LANES), F32)],
        compiler_params=pltpu.CompilerParams(
            dimension_semantics=("arbitrary", "arbitrary"), vmem_limit_bytes=VMEM_LIMIT),
        name="ffn_final" if final else "ffn",
    )(*args)


def _mixin_kernel(x_ref, mod_ref, g_ref, w_ref, q_ref, k_ref, v_ref, u_ref, vg_ref, *, kv_seq):
    shift = mod_ref[0, 3:4, :]
    scale = mod_ref[0, 4:5, :]
    h = (_rms(x_ref[...], g_ref[...]) * (1.0 + scale) + shift).astype(BF16)
    for idx, o_ref in enumerate((q_ref, k_ref, v_ref, u_ref, vg_ref)):
        p = _dot(h, w_ref[:, idx * WIDTH_A:(idx + 1) * WIDTH_A])
        if idx == 0:
            p = p * Q_SCALE
        if kv_seq and idx in (1, 2):
            for e in range(p.shape[0] // kv_seq):
                o_ref[e] = p[e * kv_seq:(e + 1) * kv_seq, :].T
        else:
            o_ref[...] = p.astype(o_ref.dtype)


def _mixin(x, mod, norm_g, w_in, *, rows_per_mod, kv_seq=None, tm=512):
    t = x.shape[0]
    tok_shape = jax.ShapeDtypeStruct((t, WIDTH_A), BF16)
    tok_spec = pl.BlockSpec((tm, WIDTH_A), lambda i: (i, 0))
    if kv_seq:
        kv_shape = jax.ShapeDtypeStruct((t // kv_seq, WIDTH_A, kv_seq), F32)
        kv_spec = pl.BlockSpec((tm // kv_seq, WIDTH_A, kv_seq), lambda i: (i, 0, 0))
    else:
        kv_shape, kv_spec = tok_shape, tok_spec
    return pl.pallas_call(
        functools.partial(_mixin_kernel, kv_seq=kv_seq),
        out_shape=[tok_shape, kv_shape, kv_shape, tok_shape, tok_shape],
        grid=(t // tm,),
        in_specs=[
            pl.BlockSpec((tm, D_MODEL), lambda i: (i, 0)),
            pl.BlockSpec((1, N_MOD, D_MODEL), lambda i: (i * tm // rows_per_mod, 0, 0)),
            pl.BlockSpec((1, D_MODEL), lambda i: (0, 0)),
            pl.BlockSpec((D_MODEL, IN_WIDTH), lambda i: (0, 0), pipeline_mode=pl.Buffered(1)),
        ],
        out_specs=[tok_spec, kv_spec, kv_spec, tok_spec, tok_spec],
        compiler_params=pltpu.CompilerParams(
            dimension_semantics=("parallel",), vmem_limit_bytes=VMEM_LIMIT),
        name="mixin",
    )(x, mod, norm_g, w_in)


class _Seg(NamedTuple):
    k: jax.Array
    v: jax.Array
    feature_major: bool


def _head_masks(hh):
    lane = lax.broadcasted_iota(jnp.int32, (1, LANES), 1)
    ones_idx = ((hh + 1) % HEADS_PER_BLOCK) * HEAD_DIM_A
    return (lane // HEAD_DIM_A) == hh, lane == ones_idx, ones_idx


def _head_scores(q, segs, biases, hh):
    head, _, _ = _head_masks(hh)
    qh = jnp.where(head, q, jnp.zeros_like(q))
    scores = []
    for seg, b in zip(segs, biases):
        s = _dot(qh, seg.k) if seg.feature_major else _dot_nt(qh, seg.k)
        scores.append(s if b is None else s + b)
    return scores


def _head_probs(scores):
    m = scores[0].max(axis=-1, keepdims=True)
    for s in scores[1:]:
        m = jnp.maximum(m, s.max(axis=-1, keepdims=True))
    return [jnp.exp2(s - m).astype(BF16) for s in scores]


def _head_output(probs, segs, hh):
    head, ones_lane, ones_idx = _head_masks(hh)
    row = lax.broadcasted_iota(jnp.int32, (LANES, 1), 0)
    ones_row = jnp.where(ones_lane, 1.0, 0.0).astype(BF16)
    acc = None
    for p, seg in zip(probs, segs):
        if seg.feature_major:
            ones_col = jnp.where(row == ones_idx, 1.0, 0.0)
            vt_aug = jnp.where((row // HEAD_DIM_A) == hh, seg.v, ones_col).astype(BF16)
            d = _dot_nt(p, vt_aug)
        else:
            d = _dot(p, jnp.where(head, seg.v, jnp.broadcast_to(ones_row, seg.v.shape)))
        acc = d if acc is None else acc + d
    den = jnp.sum(jnp.where(ones_lane, acc, 0.0), axis=-1, keepdims=True)
    return jnp.where(head, acc / den, 0.0)


def _attention_heads(q_ref, o_ref, seg_fn, bias_fn, pairs_per_iter):
    def body(it, carry):
        pairs = [it * pairs_per_iter + sub for sub in range(pairs_per_iter)]
        cols = [pl.ds(pl.multiple_of(p * LANES, LANES), LANES) for p in pairs]
        heads = [(i, hh) for i in range(pairs_per_iter) for hh in range(HEADS_PER_BLOCK)]
        operands = {}

        def load(i):
            if i not in operands:
                operands[i] = (q_ref[:, cols[i]], seg_fn(pairs[i], cols[i]))
            return operands[i]

        def scores_of(t):
            i, hh = heads[t]
            q, segs = load(i)
            return _head_scores(q, segs, [bias_fn(pairs[i], hh, k) for k in range(len(segs))], hh)

        scores, probs, out = {}, {}, None
        for t in range(len(heads) + 2):
            if t < len(heads):
                scores[t] = scores_of(t)
            if 1 <= t <= len(heads):
                probs[t - 1] = _head_probs(scores.pop(t - 1))
            if t >= 2:
                i, hh = heads[t - 2]
                o = _head_output(probs.pop(t - 2), load(i)[1], hh)
                out = o if hh == 0 else out + o
                if hh == HEADS_PER_BLOCK - 1:
                    o_ref[:, cols[i]] = out.astype(o_ref.dtype)
        return carry

    lax.fori_loop(0, N_HEAD_BLOCKS // pairs_per_iter, body, 0)


def _ctx_attn_kernel(q_ref, kt_ref, vt_ref, o_ref):
    def seg_fn(p, cols):
        return [_Seg(kt_ref[0, cols, :].astype(BF16), vt_ref[0, cols, :], True)]

    _attention_heads(q_ref, o_ref, seg_fn, lambda p, hh, i: None, N_HEAD_BLOCKS)


def _ctx_attention(q, kt, vt, seq):
    t = q.shape[0]
    qspec = pl.BlockSpec((seq, WIDTH_A), lambda b: (b, 0))
    tspec = pl.BlockSpec((1, WIDTH_A, seq), lambda b: (b, 0, 0))
    return pl.pallas_call(
        _ctx_attn_kernel,
        out_shape=jax.ShapeDtypeStruct((t, WIDTH_A), BF16),
        grid=(t // seq,),
        in_specs=[qspec, tspec, tspec],
        out_specs=qspec,
        compiler_params=pltpu.CompilerParams(
            dimension_semantics=("parallel",), vmem_limit_bytes=VMEM_LIMIT),
        name="ctx_attn",
    )(q, kt, vt)


def _nbr_attn_kernel(q_ref, k0_ref, k1_ref, k2_ref, v0_ref, v1_ref, v2_ref,
                     ckt_ref, cvt_ref, bias_ref, o_ref):
    def seg_fn(p, cols):
        segs = [_Seg(k_ref[:, cols], v_ref[:, cols], False)
                for k_ref, v_ref in ((k0_ref, v0_ref), (k1_ref, v1_ref), (k2_ref, v2_ref))]
        segs.append(_Seg(ckt_ref[0, cols, :].astype(BF16), cvt_ref[0, cols, :], True))
        return segs

    def bias_fn(p, hh, i):
        if i >= WIN_BLOCKS:
            return None
        return bias_ref[0, HEADS_PER_BLOCK * p + hh, :, i * Q_TOK:(i + 1) * Q_TOK]

    _attention_heads(q_ref, o_ref, seg_fn, bias_fn, PAIRS_PER_ITER)


def _nbr_attention(q, k, v, ckt, cvt, bias, n_tok):
    t = q.shape[0]
    batch = t // n_tok
    past = ckt.shape[2]
    qb = n_tok // Q_TOK
    def win(m):
        return jnp.clip(m - 1, 0, qb - WIN_BLOCKS)
    def variant(m):
        return jnp.minimum(m, 1) + jnp.maximum(m - (qb - 2), 0)
    qspec = pl.BlockSpec((Q_TOK, WIDTH_A), lambda m, b: (b * qb + m, 0))
    def kvspec(w):
        return pl.BlockSpec((Q_TOK, WIDTH_A), lambda m, b: (b * qb + win(m) + w, 0))
    cspec = pl.BlockSpec((1, WIDTH_A, past), lambda m, b: (b, 0, 0))
    bspec = pl.BlockSpec((1, N_HEADS_A, Q_TOK, WIN_BLOCKS * Q_TOK),
                         lambda m, b: (variant(m), 0, 0, 0))
    return pl.pallas_call(
        _nbr_attn_kernel,
        out_shape=jax.ShapeDtypeStruct((t, WIDTH_A), BF16),
        grid=(qb, batch),
        in_specs=[qspec] + [kvspec(w) for w in range(WIN_BLOCKS)]
                 + [kvspec(w) for w in range(WIN_BLOCKS)] + [cspec, cspec, bspec],
        out_specs=qspec,
        compiler_params=pltpu.CompilerParams(
            dimension_semantics=("parallel", "arbitrary"), vmem_limit_bytes=VMEM_LIMIT),
        name="nbr_attn",
    )(q, k, k, k, v, v, v, ckt, cvt, bias)


def _bias_kernel(rpb_ref, o_ref, *, rows):
    h = pl.program_id(0)
    n_dr = 2 * NA_ROWS - 1
    n_dc = 2 * NA_COLS - 1
    qc = lax.broadcasted_iota(jnp.int32, (GRID_W, LANES), 0)
    ln = lax.broadcasted_iota(jnp.int32, (GRID_W, LANES), 1)
    kc = ln % GRID_W
    dc_idx = jnp.clip(kc - qc + NA_COLS - 1, 0, n_dc - 1)
    col_start = jnp.clip(qc - NA_COLS // 2, 0, GRID_W - NA_COLS)
    col_ok = (kc >= col_start) & (kc < col_start + NA_COLS)
    neg = jnp.full((GRID_W, LANES), NEG_INF, F32)
    toeplitz = []
    for dr in range(n_dr):
        acc = jnp.zeros((GRID_W, LANES), F32)
        for dc in range(n_dc):
            acc = jnp.where(dc_idx == dc, rpb_ref[h * (n_dr * n_dc) + dr * n_dc + dc], acc)
        toeplitz.append(jnp.where(col_ok, acc * LOG2E, neg))
    qb = rows // Q_ROWS
    kr = min(NA_ROWS, rows)
    for var, m in enumerate((0, 1, qb - 1)):
        wb = min(max(m - 1, 0), qb - WIN_BLOCKS)
        for a in range(Q_ROWS):
            r = Q_ROWS * m + a
            row_start = min(max(r - kr // 2, 0), rows - kr)
            for tp in range(WIN_BLOCKS * Q_ROWS // 2):
                halves = []
                for t in (2 * tp, 2 * tp + 1):
                    key_row = Q_ROWS * wb + t
                    ok = row_start <= key_row < row_start + kr
                    halves.append(toeplitz[key_row - r + NA_ROWS - 1] if ok else neg)
                tile = jnp.where(ln < GRID_W, halves[0], halves[1])
                o_ref[var, 0, a * GRID_W:(a + 1) * GRID_W, tp * LANES:(tp + 1) * LANES] = tile


def _bias_table(rpb, rows):
    return pl.pallas_call(
        functools.partial(_bias_kernel, rows=rows),
        out_shape=jax.ShapeDtypeStruct((3, N_HEADS_A, Q_TOK, WIN_BLOCKS * Q_TOK), F32),
        grid=(N_HEADS_A,),
        in_specs=[pl.BlockSpec(memory_space=pltpu.SMEM)],
        out_specs=pl.BlockSpec((3, 1, Q_TOK, WIN_BLOCKS * Q_TOK), lambda h: (0, h, 0, 0)),
        compiler_params=pltpu.CompilerParams(
            dimension_semantics=("arbitrary",), vmem_limit_bytes=VMEM_LIMIT),
        name="nbr_bias",
    )(rpb.reshape(-1))


def _mixout_kernel(x_ref, a_ref, u_ref, vg_ref, mod_ref, gn_ref, ws_ref, bs_ref,
                   na_ref, nb_ref, wo_ref, o_ref, lhs_ref, ssq_ref):
    tm = x_ref.shape[0]
    gate = mod_ref[0, 5:6, :]
    ya = _rms(a_ref[...].astype(F32), na_ref[...]).astype(BF16)
    n_parts = N_GROUPS_B // MIXOUT_GROUPS_PER_DOT
    a_cols = D_MODEL // n_parts

    def attn_part(n):
        cols = slice(n * a_cols, (n + 1) * a_cols)
        o_ref[:, cols] = x_ref[:, cols] + gate[:, cols] * _dot(ya, wo_ref[:WIDTH_A, cols])

    def gated_tile(g, c):
        rows = slice(c * CHUNK, (c + 1) * CHUNK)
        cols = slice(g * GROUP_DIM_B, (g + 1) * GROUP_DIM_B)
        vn = _rms(_gelu_tanh(vg_ref[rows, cols].astype(F32)), gn_ref[:, cols]).astype(BF16)
        gm = _gelu_tanh(u_ref[rows, cols].astype(F32)) * (_dot(ws_ref[g], vn) + bs_ref[:, g:g + 1])
        lhs_ref[rows, cols] = (gm * nb_ref[:, cols]).astype(BF16)
        sq = jnp.broadcast_to(jnp.sum(gm * gm, axis=-1, keepdims=True), (CHUNK, LANES))
        ssq_ref[rows, :] = sq if g == 0 else ssq_ref[rows, :] + sq

    yb = None
    for n in range(n_parts):
        attn_part(n)
        for g in range(n * MIXOUT_GROUPS_PER_DOT, (n + 1) * MIXOUT_GROUPS_PER_DOT):
            for c in range(tm // CHUNK):
                gated_tile(g, c)
        ks = slice(n * MIXOUT_GROUPS_PER_DOT * GROUP_DIM_B, (n + 1) * MIXOUT_GROUPS_PER_DOT * GROUP_DIM_B)
        d = _dot(lhs_ref[:, ks], wo_ref[WIDTH_A + ks.start:WIDTH_A + ks.stop, :])
        yb = d if yb is None else yb + d
    row_scale = lax.rsqrt(ssq_ref[...] * (1.0 / WIDTH_B) + EPS)
    o_ref[...] += gate * (_lane_tile(row_scale, D_MODEL) * yb)


def _mixout(x, a, u, vg, mod, gmlp_norm, w_s, b_s_t, na_g, nb_g, w_out, *, rows_per_mod, tm=512):
    t = x.shape[0]
    tok = lambda w: pl.BlockSpec((tm, w), lambda i: (i, 0))
    full = lambda shape: pl.BlockSpec(shape, lambda i: (0,) * len(shape))
    return pl.pallas_call(
        _mixout_kernel,
        out_shape=jax.ShapeDtypeStruct((t, D_MODEL), F32),
        grid=(t // tm,),
        in_specs=[
            tok(D_MODEL), tok(WIDTH_A), tok(WIDTH_B), tok(WIDTH_B),
            pl.BlockSpec((1, N_MOD, D_MODEL), lambda i: (i * tm // rows_per_mod, 0, 0)),
            full((1, WIDTH_B)),
            full((N_GROUPS_B, CHUNK, CHUNK)),
            full((CHUNK, N_GROUPS_B)),
            full((1, WIDTH_A)),
            full((1, WIDTH_B)),
            pl.BlockSpec((MIX_WIDTH, D_MODEL), lambda i: (0, 0), pipeline_mode=pl.Buffered(1)),
        ],
        out_specs=tok(D_MODEL),
        scratch_shapes=[pltpu.VMEM((tm, WIDTH_B), BF16), pltpu.VMEM((tm, LANES), F32)],
        compiler_params=pltpu.CompilerParams(
            dimension_semantics=("parallel",), vmem_limit_bytes=VMEM_LIMIT),
        name="mixout",
    )(x, a, u, vg, mod, gmlp_norm, w_s, b_s_t, na_g, nb_g, w_out)


def kernel(x_prompt, x_sample, cache_k, cache_v, c, c_ctx, w_ada, b_ada, ffn1_norm, ffn1_w_gate, ffn1_w_up, ffn1_w_down, mix_norm, w_in, rpb, gmlp_norm, w_s, b_s, out_norm_a, out_norm_b, w_out, ffn2_norm, ffn2_w_gate, ffn2_w_up, ffn2_w_down, final_norm):
    batch, seq, _ = x_prompt.shape
    dec_batch, dec_seq, _ = x_sample.shape
    depth = w_ada.shape[0]
    x_ctx = x_prompt.reshape(batch * seq, D_MODEL)
    x_lat = x_sample.reshape(dec_batch * dec_seq, D_MODEL)
    final_g = final_norm.reshape(1, D_MODEL)

    cvec = jnp.concatenate([c_ctx[None, :], c], axis=0)
    mod_rows = -(-cvec.shape[0] // 8) * 8
    cvec = jnp.pad(cvec, ((0, mod_rows - cvec.shape[0]), (0, 0)))

    new_k, new_v = [], []
    for l in range(depth):
        last = l == depth - 1
        mod = _adaln(cvec, w_ada[l], b_ada[l][None, :]).reshape(mod_rows, N_MOD, D_MODEL)
        mods = (mod[0:1], mod[1:1 + dec_batch])
        rows_per_mod = (batch * seq, dec_seq)

        f1 = (ffn1_norm[l][None, :], ffn1_w_gate[l].astype(BF16), ffn1_w_up[l].astype(BF16),
              ffn1_w_down[l].astype(BF16))
        mix_g = mix_norm[l][None, :]
        w_s_l = w_s[l].astype(BF16)
        b_s_t = b_s[l].T
        gn = gmlp_norm[l][None, :]
        na_g = out_norm_a[l][None, :]
        nb_g = out_norm_b[l][None, :]
        bias = _bias_table(rpb[l], dec_seq // GRID_W)
        ckt = jnp.transpose(cache_k[:, l], (0, 2, 3, 1)).reshape(dec_batch, WIDTH_A, -1)
        cvt = jnp.transpose(cache_v[:, l], (0, 2, 3, 1)).reshape(dec_batch, WIDTH_A, -1)

        n_j = D_FF // FFN_TF
        n_ctx, n_lat = x_ctx.shape[0] // FFN_TM, x_lat.shape[0] // FFN_TM
        x_ctx, w_in_l, w_out_l = _ffn(
            x_ctx, mods[0], *f1, mod_base=0, rows_per_mod=rows_per_mod[0],
            side=(_cast_plan(w_in[l], n_ctx, n_j), _cast_plan(w_out[l], n_ctx, n_j)))
        x_lat, *f2_w = _ffn(
            x_lat, mods[1], *f1, mod_base=0, rows_per_mod=rows_per_mod[1],
            side=(_cast_plan_tiled(ffn2_w_gate[l], n_lat, n_j), _cast_plan_tiled(ffn2_w_up[l], n_lat, n_j),
                  _cast_plan_t(ffn2_w_down[l], n_lat, n_j)))
        f2 = (ffn2_norm[l][None, :], *f2_w)

        xs = []
        for path, (x, m, rpm) in enumerate(zip((x_ctx, x_lat), mods, rows_per_mod)):
            q, k, v, u, vg = _mixin(x, m, mix_g, w_in_l, rows_per_mod=rpm,
                                    kv_seq=seq if path == 0 else None)
            if path == 0:
                a = _ctx_attention(q, k, v, seq)
                for store, kt in ((new_k, k), (new_v, v)):
                    kt = kt.reshape(batch, N_HEADS_A, HEAD_DIM_A, seq)
                    store.append(jnp.transpose(kt, (0, 3, 1, 2)))
            else:
                a = _nbr_attention(q, k, v, ckt, cvt, bias, dec_seq)
            x = _mixout(x, a, u, vg, m, gn, w_s_l, b_s_t, na_g, nb_g, w_out_l, rows_per_mod=rpm)
            x, = _ffn(x, m, *f2, mod_base=6, rows_per_mod=rpm,
                      final_g=final_g if last else None)
            xs.append(x)
        x_ctx, x_lat = xs

    if depth == 0:
        raise ValueError("depth must be positive")
    y_prompt = x_ctx.reshape(batch, seq, D_MODEL)
    y_sample = x_lat.reshape(dec_batch, dec_seq, D_MODEL)
    return (y_prompt, y_sample, jnp.stack(new_k, axis=1), jnp.stack(new_v, axis=1))
```

```python
import functools
import math
from typing import NamedTuple

import jax
import jax.numpy as jnp
from jax import lax
from jax.experimental import pallas as pl
from jax.experimental.pallas import tpu as pltpu

D_MODEL = 2048
N_HEADS_A = 16
HEAD_DIM_A = 64
WIDTH_A = N_HEADS_A * HEAD_DIM_A
GRID_W = 64
NA_ROWS = 8
NA_COLS = 16
N_GROUPS_B = 8
GROUP_DIM_B = 128
WIDTH_B = N_GROUPS_B * GROUP_DIM_B
CHUNK = 128
MIX_WIDTH = WIDTH_A + WIDTH_B
IN_WIDTH = 3 * WIDTH_A + 2 * WIDTH_B
D_FF = 5632
N_MOD = 9
EPS = 1e-6
NEG_INF = -1e30

LANES = 128
HEADS_PER_BLOCK = LANES // HEAD_DIM_A
N_HEAD_BLOCKS = N_HEADS_A // HEADS_PER_BLOCK
Q_ROWS = 4
Q_TOK = Q_ROWS * GRID_W
WIN_BLOCKS = 3
VMEM_LIMIT = 62 * 1024 * 1024
FFN_OUT_CHUNK = 512
FFN_TM = 1024
FFN_TF = 512
PAIRS_PER_ITER = 8
MIXOUT_GROUPS_PER_DOT = 4
LOG2E = math.log2(math.e)
Q_SCALE = HEAD_DIM_A ** -0.5 * LOG2E

F32 = jnp.float32
BF16 = jnp.bfloat16


def _dot(a, b):
    return jnp.dot(a, b, preferred_element_type=F32)


def _dot_nt(a, b):
    return lax.dot_general(a, b, (((1,), (1,)), ((), ())), preferred_element_type=F32)


def _rms(x, g):
    ms = jnp.mean(x * x, axis=-1, keepdims=True)
    return x * lax.rsqrt(ms + EPS) * g


def _silu(x):
    return x * jax.nn.sigmoid(x)


def _gelu_tanh(x):
    c = 0.7978845608028654
    return 0.5 * x * (1.0 + jnp.tanh(c * (x + 0.044715 * (x * x * x))))


def _lane_tile(v, width):
    return jnp.concatenate([v] * (width // LANES), axis=1)


def _adaln_kernel(c_ref, w_ref, b_ref, o_ref):
    s = _silu(c_ref[...]).astype(BF16)
    o_ref[...] = _dot(s, w_ref[...].astype(BF16)) + b_ref[...]


def _adaln(cvec, w_ada, b_ada, tn=1024):
    rows = cvec.shape[0]
    n = w_ada.shape[1]
    return pl.pallas_call(
        _adaln_kernel,
        out_shape=jax.ShapeDtypeStruct((rows, n), F32),
        grid=(n // tn,),
        in_specs=[
            pl.BlockSpec((rows, D_MODEL), lambda j: (0, 0)),
            pl.BlockSpec((D_MODEL, tn), lambda j: (0, j)),
            pl.BlockSpec((1, tn), lambda j: (0, j)),
        ],
        out_specs=pl.BlockSpec((rows, tn), lambda j: (0, j)),
        compiler_params=pltpu.CompilerParams(
            dimension_semantics=("arbitrary",), vmem_limit_bytes=VMEM_LIMIT),
        name="adaln",
    )(cvec, w_ada, b_ada)


def _ffn_kernel(*refs, mod_base, final, n_side):
    x_ref, mod_ref, g_ref, wg_ref, wu_ref, wd_ref = refs[:6]
    n_in = 7 if final else 6
    fg_ref = refs[6] if final else None
    side_in = refs[n_in:n_in + n_side]
    o_ref = refs[n_in + n_side]
    side_out = refs[n_in + n_side + 1:n_in + 2 * n_side + 1]
    h_ref, = refs[n_in + 2 * n_side + 1:]
    j = pl.program_id(1)

    def step(acc_ref, h):
        for src_ref, dst_ref in zip(side_in, side_out):
            if len(dst_ref.shape) == 3:
                dst_ref[0] = src_ref[...].astype(BF16)
            else:
                dst_ref[...] = src_ref[...].astype(BF16)
        wg = wg_ref[0] if len(wg_ref.shape) == 3 else wg_ref[...]
        wu = wu_ref[0] if len(wu_ref.shape) == 3 else wu_ref[...]
        a = (_silu(_dot(h, wg)) * _dot(h, wu)).astype(BF16)
        half_gate = 0.5 * mod_ref[0, mod_base + 2:mod_base + 3, :]
        for n in range(D_MODEL // FFN_OUT_CHUNK):
            cols = slice(n * FFN_OUT_CHUNK, (n + 1) * FFN_OUT_CHUNK)
            o_ref[:, cols] = acc_ref[:, cols] + half_gate[:, cols] * _dot(a, wd_ref[:, cols])

    @pl.when(j == 0)
    def _():
        x = x_ref[...]
        shift = mod_ref[0, mod_base:mod_base + 1, :]
        gain = g_ref[...] * (1.0 + mod_ref[0, mod_base + 1:mod_base + 2, :])
        ms = jnp.mean(x * x, axis=-1, keepdims=True)
        h = (x * lax.rsqrt(ms + EPS) * gain + shift).astype(BF16)
        h_ref[...] = h
        step(x_ref, h)

    last = pl.num_programs(1) - 1

    @pl.when((j > 0) & (j < last) if final else j > 0)
    def _():
        step(o_ref, h_ref[...])

    if final:
        @pl.when(j == last)
        def _():
            step(o_ref, h_ref[...])
            o_ref[...] = _rms(o_ref[...], fg_ref[...])


def _cast_plan(w, n_i, n_j):
    rows, cols = w.shape
    col_blocks = max(d for d in range(1, n_j + 1) if cols % d == 0 and (cols // d) % LANES == 0)
    assert rows % n_i == 0 and (rows // n_i) % 16 == 0
    block = (rows // n_i, cols // col_blocks)
    index_map = lambda i, j: (i, jnp.minimum(j, col_blocks - 1))
    return w, block, index_map, w.shape, block, index_map


def _cast_plan_t(w, n_i, n_j):
    rows, cols = w.shape
    assert rows % n_j == 0 and (rows // n_j) % 16 == 0 and cols % n_i == 0 and (cols // n_i) % LANES == 0
    block = (rows // n_j, cols // n_i)
    index_map = lambda i, j: (j, i)
    return w, block, index_map, w.shape, block, index_map


def _cast_plan_tiled(w, n_i, n_j):
    rows, cols = w.shape
    assert rows % n_i == 0 and (rows // n_i) % 16 == 0 and cols % n_j == 0 and (cols // n_j) % LANES == 0
    tile = cols // n_j
    return (w, (rows // n_i, tile), lambda i, j: (i, j),
            (n_j, rows, tile), (1, rows // n_i, tile), lambda i, j: (j, i, 0))


def _ffn_in_weight_spec(w, tf):
    if w.ndim == 3:
        return pl.BlockSpec((1, D_MODEL, tf), lambda i, j: (j, 0, 0))
    return pl.BlockSpec((D_MODEL, tf), lambda i, j: (0, j))


def _ffn(x, mod, norm_g, wg, wu, wd, *, mod_base, rows_per_mod, final_g=None, side=(),
         tm=FFN_TM, tf=FFN_TF):
    t = x.shape[0]
    final = final_g is not None
    in_specs = [
        pl.BlockSpec((tm, D_MODEL), lambda i, j: (i, 0)),
        pl.BlockSpec((1, N_MOD, D_MODEL), lambda i, j: (i * tm // rows_per_mod, 0, 0)),
        pl.BlockSpec((1, D_MODEL), lambda i, j: (0, 0)),
        _ffn_in_weight_spec(wg, tf),
        _ffn_in_weight_spec(wu, tf),
        pl.BlockSpec((tf, D_MODEL), lambda i, j: (j, 0)),
    ]
    args = [x, mod, norm_g, wg, wu, wd]
    if final:
        in_specs.append(pl.BlockSpec((1, D_MODEL), lambda i, j: (0, 0)))
        args.append(final_g)
    out_shape = [jax.ShapeDtypeStruct((t, D_MODEL), F32)]
    out_specs = [pl.BlockSpec((tm, D_MODEL), lambda i, j: (i, 0))]
    for w, in_block, in_map, o_shape, o_block, o_map in side:
        in_specs.append(pl.BlockSpec(in_block, in_map))
        args.append(w)
        out_shape.append(jax.ShapeDtypeStruct(o_shape, BF16))
        out_specs.append(pl.BlockSpec(o_block, o_map))
    return pl.pallas_call(
        functools.partial(_ffn_kernel, mod_base=mod_base, final=final, n_side=len(side)),
        out_shape=out_shape,
        grid=(t // tm, D_FF // tf),
        in_specs=in_specs,
        out_specs=out_specs,
        scratch_shapes=[pltpu.VMEM((tm, D_MODEL), BF16)],
        compiler_params=pltpu.CompilerParams(
            dimension_semantics=("arbitrary", "arbitrary"), vmem_limit_bytes=VMEM_LIMIT),
        name="ffn_final" if final else "ffn",
    )(*args)


def _mixin_kernel(x_ref, mod_ref, g_ref, w_ref, q_ref, k_ref, v_ref, u_ref, vg_ref, *, kv_seq):
    shift = mod_ref[0, 3:4, :]
    scale = mod_ref[0, 4:5, :]
    h = (_rms(x_ref[...], g_ref[...]) * (1.0 + scale) + shift).astype(BF16)
    for idx, o_ref in enumerate((q_ref, k_ref, v_ref, u_ref, vg_ref)):
        p = _dot(h, w_ref[:, idx * WIDTH_A:(idx + 1) * WIDTH_A])
        if idx == 0:
            p = p * Q_SCALE
        if kv_seq and idx in (1, 2):
            for e in range(p.shape[0] // kv_seq):
                o_ref[e] = p[e * kv_seq:(e + 1) * kv_seq, :].T
        else:
            o_ref[...] = p.astype(o_ref.dtype)


def _mixin(x, mod, norm_g, w_in, *, rows_per_mod, kv_seq=None, tm=512):
    t = x.shape[0]
    tok_shape = jax.ShapeDtypeStruct((t, WIDTH_A), BF16)
    tok_spec = pl.BlockSpec((tm, WIDTH_A), lambda i: (i, 0))
    if kv_seq:
        kv_shape = jax.ShapeDtypeStruct((t // kv_seq, WIDTH_A, kv_seq), F32)
        kv_spec = pl.BlockSpec((tm // kv_seq, WIDTH_A, kv_seq), lambda i: (i, 0, 0))
    else:
        kv_shape, kv_spec = tok_shape, tok_spec
    return pl.pallas_call(
        functools.partial(_mixin_kernel, kv_seq=kv_seq),
        out_shape=[tok_shape, kv_shape, kv_shape, tok_shape, tok_shape],
        grid=(t // tm,),
        in_specs=[
            pl.BlockSpec((tm, D_MODEL), lambda i: (i, 0)),
            pl.BlockSpec((1, N_MOD, D_MODEL), lambda i: (i * tm // rows_per_mod, 0, 0)),
            pl.BlockSpec((1, D_MODEL), lambda i: (0, 0)),
            pl.BlockSpec((D_MODEL, IN_WIDTH), lambda i: (0, 0), pipeline_mode=pl.Buffered(1)),
        ],
        out_specs=[tok_spec, kv_spec, kv_spec, tok_spec, tok_spec],
        compiler_params=pltpu.CompilerParams(
            dimension_semantics=("parallel",), vmem_limit_bytes=VMEM_LIMIT),
        name="mixin",
    )(x, mod, norm_g, w_in)


class _Seg(NamedTuple):
    k: jax.Array
    v: jax.Array
    feature_major: bool


def _head_masks(hh):
    lane = lax.broadcasted_iota(jnp.int32, (1, LANES), 1)
    ones_idx = ((hh + 1) % HEADS_PER_BLOCK) * HEAD_DIM_A
    return (lane // HEAD_DIM_A) == hh, lane == ones_idx, ones_idx


def _head_scores(q, segs, biases, hh):
    head, _, _ = _head_masks(hh)
    qh = jnp.where(head, q, jnp.zeros_like(q))
    scores = []
    for seg, b in zip(segs, biases):
        s = _dot(qh, seg.k) if seg.feature_major else _dot_nt(qh, seg.k)
        scores.append(s if b is None else s + b)
    return scores


def _head_probs(scores):
    m = scores[0].max(axis=-1, keepdims=True)
    for s in scores[1:]:
        m = jnp.maximum(m, s.max(axis=-1, keepdims=True))
    return [jnp.exp2(s - m).astype(BF16) for s in scores]


def _head_output(probs, segs, hh):
    head, ones_lane, ones_idx = _head_masks(hh)
    row = lax.broadcasted_iota(jnp.int32, (LANES, 1), 0)
    ones_row = jnp.where(ones_lane, 1.0, 0.0).astype(BF16)
    acc = None
    for p, seg in zip(probs, segs):
        if seg.feature_major:
            ones_col = jnp.where(row == ones_idx, 1.0, 0.0)
            vt_aug = jnp.where((row // HEAD_DIM_A) == hh, seg.v, ones_col).astype(BF16)
            d = _dot_nt(p, vt_aug)
        else:
            d = _dot(p, jnp.where(head, seg.v, jnp.broadcast_to(ones_row, seg.v.shape)))
        acc = d if acc is None else acc + d
    den = jnp.sum(jnp.where(ones_lane, acc, 0.0), axis=-1, keepdims=True)
    return jnp.where(head, acc / den, 0.0)


def _attention_heads(q_ref, o_ref, seg_fn, bias_fn, pairs_per_iter):
    def body(it, carry):
        pairs = [it * pairs_per_iter + sub for sub in range(pairs_per_iter)]
        cols = [pl.ds(pl.multiple_of(p * LANES, LANES), LANES) for p in pairs]
        heads = [(i, hh) for i in range(pairs_per_iter) for hh in range(HEADS_PER_BLOCK)]
        operands = {}

        def load(i):
            if i not in operands:
                operands[i] = (q_ref[:, cols[i]], seg_fn(pairs[i], cols[i]))
            return operands[i]

        def scores_of(t):
            i, hh = heads[t]
            q, segs = load(i)
            return _head_scores(q, segs, [bias_fn(pairs[i], hh, k) for k in range(len(segs))], hh)

        scores, probs, out = {}, {}, None
        for t in range(len(heads) + 2):
            if t < len(heads):
                scores[t] = scores_of(t)
            if 1 <= t <= len(heads):
                probs[t - 1] = _head_probs(scores.pop(t - 1))
            if t >= 2:
                i, hh = heads[t - 2]
                o = _head_output(probs.pop(t - 2), load(i)[1], hh)
                out = o if hh == 0 else out + o
                if hh == HEADS_PER_BLOCK - 1:
                    o_ref[:, cols[i]] = out.astype(o_ref.dtype)
        return carry

    lax.fori_loop(0, N_HEAD_BLOCKS // pairs_per_iter, body, 0)


def _ctx_attn_kernel(q_ref, kt_ref, vt_ref, o_ref):
    def seg_fn(p, cols):
        return [_Seg(kt_ref[0, cols, :].astype(BF16), vt_ref[0, cols, :], True)]

    _attention_heads(q_ref, o_ref, seg_fn, lambda p, hh, i: None, N_HEAD_BLOCKS)


def _ctx_attention(q, kt, vt, seq):
    t = q.shape[0]
    qspec = pl.BlockSpec((seq, WIDTH_A), lambda b: (b, 0))
    tspec = pl.BlockSpec((1, WIDTH_A, seq), lambda b: (b, 0, 0))
    return pl.pallas_call(
        _ctx_attn_kernel,
        out_shape=jax.ShapeDtypeStruct((t, WIDTH_A), BF16),
        grid=(t // seq,),
        in_specs=[qspec, tspec, tspec],
        out_specs=qspec,
        compiler_params=pltpu.CompilerParams(
            dimension_semantics=("parallel",), vmem_limit_bytes=VMEM_LIMIT),
        name="ctx_attn",
    )(q, kt, vt)


def _nbr_attn_kernel(q_ref, k0_ref, k1_ref, k2_ref, v0_ref, v1_ref, v2_ref,
                     ckt_ref, cvt_ref, bias_ref, o_ref):
    def seg_fn(p, cols):
        segs = [_Seg(k_ref[:, cols], v_ref[:, cols], False)
                for k_ref, v_ref in ((k0_ref, v0_ref), (k1_ref, v1_ref), (k2_ref, v2_ref))]
        segs.append(_Seg(ckt_ref[0, cols, :].astype(BF16), cvt_ref[0, cols, :], True))
        return segs

    def bias_fn(p, hh, i):
        if i >= WIN_BLOCKS:
            return None
        return bias_ref[0, HEADS_PER_BLOCK * p + hh, :, i * Q_TOK:(i + 1) * Q_TOK]

    _attention_heads(q_ref, o_ref, seg_fn, bias_fn, PAIRS_PER_ITER)


def _nbr_attention(q, k, v, ckt, cvt, bias, n_tok):
    t = q.shape[0]
    batch = t // n_tok
    past = ckt.shape[2]
    qb = n_tok // Q_TOK
    def win(m):
        return jnp.clip(m - 1, 0, qb - WIN_BLOCKS)
    def variant(m):
        return jnp.minimum(m, 1) + jnp.maximum(m - (qb - 2), 0)
    qspec = pl.BlockSpec((Q_TOK, WIDTH_A), lambda m, b: (b * qb + m, 0))
    def kvspec(w):
        return pl.BlockSpec((Q_TOK, WIDTH_A), lambda m, b: (b * qb + win(m) + w, 0))
    cspec = pl.BlockSpec((1, WIDTH_A, past), lambda m, b: (b, 0, 0))
    bspec = pl.BlockSpec((1, N_HEADS_A, Q_TOK, WIN_BLOCKS * Q_TOK),
                         lambda m, b: (variant(m), 0, 0, 0))
    return pl.pallas_call(
        _nbr_attn_kernel,
        out_shape=jax.ShapeDtypeStruct((t, WIDTH_A), BF16),
        grid=(qb, batch),
        in_specs=[qspec] + [kvspec(w) for w in range(WIN_BLOCKS)]
                 + [kvspec(w) for w in range(WIN_BLOCKS)] + [cspec, cspec, bspec],
        out_specs=qspec,
        compiler_params=pltpu.CompilerParams(
            dimension_semantics=("parallel", "arbitrary"), vmem_limit_bytes=VMEM_LIMIT),
        name="nbr_attn",
    )(q, k, k, k, v, v, v, ckt, cvt, bias)


def _bias_kernel(rpb_ref, o_ref, *, rows):
    h = pl.program_id(0)
    n_dr = 2 * NA_ROWS - 1
    n_dc = 2 * NA_COLS - 1
    qc = lax.broadcasted_iota(jnp.int32, (GRID_W, LANES), 0)
    ln = lax.broadcasted_iota(jnp.int32, (GRID_W, LANES), 1)
    kc = ln % GRID_W
    dc_idx = jnp.clip(kc - qc + NA_COLS - 1, 0, n_dc - 1)
    col_start = jnp.clip(qc - NA_COLS // 2, 0, GRID_W - NA_COLS)
    col_ok = (kc >= col_start) & (kc < col_start + NA_COLS)
    neg = jnp.full((GRID_W, LANES), NEG_INF, F32)
    toeplitz = []
    for dr in range(n_dr):
        acc = jnp.zeros((GRID_W, LANES), F32)
        for dc in range(n_dc):
            acc = jnp.where(dc_idx == dc, rpb_ref[h * (n_dr * n_dc) + dr * n_dc + dc], acc)
        toeplitz.append(jnp.where(col_ok, acc * LOG2E, neg))
    qb = rows // Q_ROWS
    kr = min(NA_ROWS, rows)
    for var, m in enumerate((0, 1, qb - 1)):
        wb = min(max(m - 1, 0), qb - WIN_BLOCKS)
        for a in range(Q_ROWS):
            r = Q_ROWS * m + a
            row_start = min(max(r - kr // 2, 0), rows - kr)
            for tp in range(WIN_BLOCKS * Q_ROWS // 2):
                halves = []
                for t in (2 * tp, 2 * tp + 1):
                    key_row = Q_ROWS * wb + t
                    ok = row_start <= key_row < row_start + kr
                    halves.append(toeplitz[key_row - r + NA_ROWS - 1] if ok else neg)
                tile = jnp.where(ln < GRID_W, halves[0], halves[1])
                o_ref[var, 0, a * GRID_W:(a + 1) * GRID_W, tp * LANES:(tp + 1) * LANES] = tile


def _bias_table(rpb, rows):
    return pl.pallas_call(
        functools.partial(_bias_kernel, rows=rows),
        out_shape=jax.ShapeDtypeStruct((3, N_HEADS_A, Q_TOK, WIN_BLOCKS * Q_TOK), F32),
        grid=(N_HEADS_A,),
        in_specs=[pl.BlockSpec(memory_space=pltpu.SMEM)],
        out_specs=pl.BlockSpec((3, 1, Q_TOK, WIN_BLOCKS * Q_TOK), lambda h: (0, h, 0, 0)),
        compiler_params=pltpu.CompilerParams(
            dimension_semantics=("arbitrary",), vmem_limit_bytes=VMEM_LIMIT),
        name="nbr_bias",
    )(rpb.reshape(-1))


def _mixout_kernel(x_ref, a_ref, u_ref, vg_ref, mod_ref, gn_ref, ws_ref, bs_ref,
                   na_ref, nb_ref, wo_ref, o_ref, lhs_ref, ssq_ref):
    tm = x_ref.shape[0]
    gate = mod_ref[0, 5:6, :]
    ya = _rms(a_ref[...].astype(F32), na_ref[...]).astype(BF16)
    n_parts = N_GROUPS_B // MIXOUT_GROUPS_PER_DOT
    a_cols = D_MODEL // n_parts

    def attn_part(n):
        cols = slice(n * a_cols, (n + 1) * a_cols)
        o_ref[:, cols] = x_ref[:, cols] + gate[:, cols] * _dot(ya, wo_ref[:WIDTH_A, cols])

    def gated_tile(g, c):
        rows = slice(c * CHUNK, (c + 1) * CHUNK)
        cols = slice(g * GROUP_DIM_B, (g + 1) * GROUP_DIM_B)
        vn = _rms(_gelu_tanh(vg_ref[rows, cols].astype(F32)), gn_ref[:, cols]).astype(BF16)
        gm = _gelu_tanh(u_ref[rows, cols].astype(F32)) * (_dot(ws_ref[g], vn) + bs_ref[:, g:g + 1])
        lhs_ref[rows, cols] = (gm * nb_ref[:, cols]).astype(BF16)
        sq = jnp.broadcast_to(jnp.sum(gm * gm, axis=-1, keepdims=True), (CHUNK, LANES))
        ssq_ref[rows, :] = sq if g == 0 else ssq_ref[rows, :] + sq

    yb = None
    for n in range(n_parts):
        attn_part(n)
        for g in range(n * MIXOUT_GROUPS_PER_DOT, (n + 1) * MIXOUT_GROUPS_PER_DOT):
            for c in range(tm // CHUNK):
                gated_tile(g, c)
        ks = slice(n * MIXOUT_GROUPS_PER_DOT * GROUP_DIM_B, (n + 1) * MIXOUT_GROUPS_PER_DOT * GROUP_DIM_B)
        d = _dot(lhs_ref[:, ks], wo_ref[WIDTH_A + ks.start:WIDTH_A + ks.stop, :])
        yb = d if yb is None else yb + d
    row_scale = lax.rsqrt(ssq_ref[...] * (1.0 / WIDTH_B) + EPS)
    o_ref[...] += gate * (_lane_tile(row_scale, D_MODEL) * yb)


def _mixout(x, a, u, vg, mod, gmlp_norm, w_s, b_s_t, na_g, nb_g, w_out, *, rows_per_mod, tm=512):
    t = x.shape[0]
    tok = lambda w: pl.BlockSpec((tm, w), lambda i: (i, 0))
    full = lambda shape: pl.BlockSpec(shape, lambda i: (0,) * len(shape))
    return pl.pallas_call(
        _mixout_kernel,
        out_shape=jax.ShapeDtypeStruct((t, D_MODEL), F32),
        grid=(t // tm,),
        in_specs=[
            tok(D_MODEL), tok(WIDTH_A), tok(WIDTH_B), tok(WIDTH_B),
            pl.BlockSpec((1, N_MOD, D_MODEL), lambda i: (i * tm // rows_per_mod, 0, 0)),
            full((1, WIDTH_B)),
            full((N_GROUPS_B, CHUNK, CHUNK)),
            full((CHUNK, N_GROUPS_B)),
            full((1, WIDTH_A)),
            full((1, WIDTH_B)),
            pl.BlockSpec((MIX_WIDTH, D_MODEL), lambda i: (0, 0), pipeline_mode=pl.Buffered(1)),
        ],
        out_specs=tok(D_MODEL),
        scratch_shapes=[pltpu.VMEM((tm, WIDTH_B), BF16), pltpu.VMEM((tm, LANES), F32)],
        compiler_params=pltpu.CompilerParams(
            dimension_semantics=("parallel",), vmem_limit_bytes=VMEM_LIMIT),
        name="mixout",
    )(x, a, u, vg, mod, gmlp_norm, w_s, b_s_t, na_g, nb_g, w_out)


def kernel(x_prompt, x_sample, cache_k, cache_v, c, c_ctx, w_ada, b_ada, ffn1_norm, ffn1_w_gate, ffn1_w_up, ffn1_w_down, mix_norm, w_in, rpb, gmlp_norm, w_s, b_s, out_norm_a, out_norm_b, w_out, ffn2_norm, ffn2_w_gate, ffn2_w_up, ffn2_w_down, final_norm):
    batch, seq, _ = x_prompt.shape
    dec_batch, dec_seq, _ = x_sample.shape
    depth = w_ada.shape[0]
    x_ctx = x_prompt.reshape(batch * seq, D_MODEL)
    x_lat = x_sample.reshape(dec_batch * dec_seq, D_MODEL)
    final_g = final_norm.reshape(1, D_MODEL)

    cvec = jnp.concatenate([c_ctx[None, :], c], axis=0)
    mod_rows = -(-cvec.shape[0] // 8) * 8
    cvec = jnp.pad(cvec, ((0, mod_rows - cvec.shape[0]), (0, 0)))

    new_k, new_v = [], []
    for l in range(depth):
        last = l == depth - 1
        mod = _adaln(cvec, w_ada[l], b_ada[l][None, :]).reshape(mod_rows, N_MOD, D_MODEL)
        mods = (mod[0:1], mod[1:1 + dec_batch])
        rows_per_mod = (batch * seq, dec_seq)

        f1 = (ffn1_norm[l][None, :], ffn1_w_gate[l].astype(BF16), ffn1_w_up[l].astype(BF16),
              ffn1_w_down[l].astype(BF16))
        mix_g = mix_norm[l][None, :]
        w_s_l = w_s[l].astype(BF16)
        b_s_t = b_s[l].T
        gn = gmlp_norm[l][None, :]
        na_g = out_norm_a[l][None, :]
        nb_g = out_norm_b[l][None, :]
        bias = _bias_table(rpb[l], dec_seq // GRID_W)
        ckt = jnp.transpose(cache_k[:, l], (0, 2, 3, 1)).reshape(dec_batch, WIDTH_A, -1)
        cvt = jnp.transpose(cache_v[:, l], (0, 2, 3, 1)).reshape(dec_batch, WIDTH_A, -1)

        n_j = D_FF // FFN_TF
        n_ctx, n_lat = x_ctx.shape[0] // FFN_TM, x_lat.shape[0] // FFN_TM
        x_ctx, w_in_l, w_out_l = _ffn(
            x_ctx, mods[0], *f1, mod_base=0, rows_per_mod=rows_per_mod[0],
            side=(_cast_plan(w_in[l], n_ctx, n_j), _cast_plan(w_out[l], n_ctx, n_j)))
        x_lat, *f2_w = _ffn(
            x_lat, mods[1], *f1, mod_base=0, rows_per_mod=rows_per_mod[1],
            side=(_cast_plan_tiled(ffn2_w_gate[l], n_lat, n_j), _cast_plan_tiled(ffn2_w_up[l], n_lat, n_j),
                  _cast_plan_t(ffn2_w_down[l], n_lat, n_j)))
        f2 = (ffn2_norm[l][None, :], *f2_w)

        xs = []
        for path, (x, m, rpm) in enumerate(zip((x_ctx, x_lat), mods, rows_per_mod)):
            q, k, v, u, vg = _mixin(x, m, mix_g, w_in_l, rows_per_mod=rpm,
                                    kv_seq=seq if path == 0 else None)
            if path == 0:
                a = _ctx_attention(q, k, v, seq)
                for store, kt in ((new_k, k), (new_v, v)):
                    kt = kt.reshape(batch, N_HEADS_A, HEAD_DIM_A, seq)
                    store.append(jnp.transpose(kt, (0, 3, 1, 2)))
            else:
                a = _nbr_attention(q, k, v, ckt, cvt, bias, dec_seq)
            x = _mixout(x, a, u, vg, m, gn, w_s_l, b_s_t, na_g, nb_g, w_out_l, rows_per_mod=rpm)
            x, = _ffn(x, m, *f2, mod_base=6, rows_per_mod=rpm,
                      final_g=final_g if last else None)
            xs.append(x)
        x_ctx, x_lat = xs

    if depth == 0:
        raise ValueError("depth must be positive")
    y_prompt = x_ctx.reshape(batch, seq, D_MODEL)
    y_sample = x_lat.reshape(dec_batch, dec_seq, D_MODEL)
    return (y_prompt, y_sample, jnp.stack(new_k, axis=1), jnp.stack(new_v, axis=1))
```

```python
import functools
import math
from typing import NamedTuple

import jax
import jax.numpy as jnp
from jax import lax
from jax.experimental import pallas as pl
from jax.experimental.pallas import tpu as pltpu

D_MODEL = 2048
N_HEADS_A = 16
HEAD_DIM_A = 64
WIDTH_A = N_HEADS_A * HEAD_DIM_A
GRID_W = 64
NA_ROWS = 8
NA_COLS = 16
N_GROUPS_B = 8
GROUP_DIM_B = 128
WIDTH_B = N_GROUPS_B * GROUP_DIM_B
CHUNK = 128
MIX_WIDTH = WIDTH_A + WIDTH_B
IN_WIDTH = 3 * WIDTH_A + 2 * WIDTH_B
D_FF = 5632
N_MOD = 9
EPS = 1e-6
NEG_INF = -1e30

LANES = 128
HEADS_PER_BLOCK = LANES // HEAD_DIM_A
N_HEAD_BLOCKS = N_HEADS_A // HEADS_PER_BLOCK
Q_ROWS = 4
Q_TOK = Q_ROWS * GRID_W
WIN_BLOCKS = 3
VMEM_LIMIT = 62 * 1024 * 1024
FFN_OUT_CHUNK = 512
FFN_TM = 1024
FFN_TF = 512
PAIRS_PER_ITER = 8
PROBS_LAG = 2
NBR_OUT_LAG = 3
CTX_OUT_LAG = 4
MIXOUT_GROUPS_PER_DOT = 4
LOG2E = math.log2(math.e)
Q_SCALE = HEAD_DIM_A ** -0.5 * LOG2E

F32 = jnp.float32
BF16 = jnp.bfloat16


def _dot(a, b):
    return jnp.dot(a, b, preferred_element_type=F32)


def _dot_nt(a, b):
    return lax.dot_general(a, b, (((1,), (1,)), ((), ())), preferred_element_type=F32)


def _rms(x, g):
    ms = jnp.mean(x * x, axis=-1, keepdims=True)
    return x * lax.rsqrt(ms + EPS) * g


def _silu(x):
    return x * jax.nn.sigmoid(x)


def _gelu_tanh(x):
    c = 0.7978845608028654
    return 0.5 * x * (1.0 + jnp.tanh(c * (x + 0.044715 * (x * x * x))))


def _lane_tile(v, width):
    return jnp.concatenate([v] * (width // LANES), axis=1)


def _adaln_kernel(c_ref, w_ref, b_ref, o_ref):
    s = _silu(c_ref[...]).astype(BF16)
    o_ref[...] = _dot(s, w_ref[...].astype(BF16)) + b_ref[...]


def _adaln(cvec, w_ada, b_ada, tn=1024):
    rows = cvec.shape[0]
    n = w_ada.shape[1]
    return pl.pallas_call(
        _adaln_kernel,
        out_shape=jax.ShapeDtypeStruct((rows, n), F32),
        grid=(n // tn,),
        in_specs=[
            pl.BlockSpec((rows, D_MODEL), lambda j: (0, 0)),
            pl.BlockSpec((D_MODEL, tn), lambda j: (0, j)),
            pl.BlockSpec((1, tn), lambda j: (0, j)),
        ],
        out_specs=pl.BlockSpec((rows, tn), lambda j: (0, j)),
        compiler_params=pltpu.CompilerParams(
            dimension_semantics=("arbitrary",), vmem_limit_bytes=VMEM_LIMIT),
        name="adaln",
    )(cvec, w_ada, b_ada)


def _ffn_kernel(*refs, mod_base, final, n_side):
    x_ref, mod_ref, g_ref, wg_ref, wu_ref, wd_ref = refs[:6]
    n_in = 7 if final else 6
    fg_ref = refs[6] if final else None
    side_in = refs[n_in:n_in + n_side]
    o_ref = refs[n_in + n_side]
    side_out = refs[n_in + n_side + 1:n_in + 2 * n_side + 1]
    h_ref, = refs[n_in + 2 * n_side + 1:]
    j = pl.program_id(1)

    def step(acc_ref, h):
        for src_ref, dst_ref in zip(side_in, side_out):
            if len(dst_ref.shape) == 3:
                dst_ref[0] = src_ref[...].astype(BF16)
            else:
                dst_ref[...] = src_ref[...].astype(BF16)
        wg = wg_ref[0] if len(wg_ref.shape) == 3 else wg_ref[...]
        wu = wu_ref[0] if len(wu_ref.shape) == 3 else wu_ref[...]
        a = (_silu(_dot(h, wg)) * _dot(h, wu)).astype(BF16)
        half_gate = 0.5 * mod_ref[0, mod_base + 2:mod_base + 3, :]
        for n in range(D_MODEL // FFN_OUT_CHUNK):
            cols = slice(n * FFN_OUT_CHUNK, (n + 1) * FFN_OUT_CHUNK)
            o_ref[:, cols] = acc_ref[:, cols] + half_gate[:, cols] * _dot(a, wd_ref[:, cols])

    @pl.when(j == 0)
    def _():
        x = x_ref[...]
        shift = mod_ref[0, mod_base:mod_base + 1, :]
        gain = g_ref[...] * (1.0 + mod_ref[0, mod_base + 1:mod_base + 2, :])
        ms = jnp.mean(x * x, axis=-1, keepdims=True)
        h = (x * lax.rsqrt(ms + EPS) * gain + shift).astype(BF16)
        h_ref[...] = h
        step(x_ref, h)

    last = pl.num_programs(1) - 1

    @pl.when((j > 0) & (j < last) if final else j > 0)
    def _():
        step(o_ref, h_ref[...])

    if final:
        @pl.when(j == last)
        def _():
            step(o_ref, h_ref[...])
            o_ref[...] = _rms(o_ref[...], fg_ref[...])


def _cast_plan(w, n_i, n_j):
    rows, cols = w.shape
    col_blocks = max(d for d in range(1, n_j + 1) if cols % d == 0 and (cols // d) % LANES == 0)
    assert rows % n_i == 0 and (rows // n_i) % 16 == 0
    block = (rows // n_i, cols // col_blocks)
    index_map = lambda i, j: (i, jnp.minimum(j, col_blocks - 1))
    return w, block, index_map, w.shape, block, index_map


def _cast_plan_t(w, n_i, n_j):
    rows, cols = w.shape
    assert rows % n_j == 0 and (rows // n_j) % 16 == 0 and cols % n_i == 0 and (cols // n_i) % LANES == 0
    block = (rows // n_j, cols // n_i)
    index_map = lambda i, j: (j, i)
    return w, block, index_map, w.shape, block, index_map


def _cast_plan_tiled(w, n_i, n_j):
    rows, cols = w.shape
    assert rows % n_i == 0 and (rows // n_i) % 16 == 0 and cols % n_j == 0 and (cols // n_j) % LANES == 0
    tile = cols // n_j
    return (w, (rows // n_i, tile), lambda i, j: (i, j),
            (n_j, rows, tile), (1, rows // n_i, tile), lambda i, j: (j, i, 0))


def _ffn_in_weight_spec(w, tf):
    if w.ndim == 3:
        return pl.BlockSpec((1, D_MODEL, tf), lambda i, j: (j, 0, 0))
    return pl.BlockSpec((D_MODEL, tf), lambda i, j: (0, j))


def _ffn(x, mod, norm_g, wg, wu, wd, *, mod_base, rows_per_mod, final_g=None, side=(),
         tm=FFN_TM, tf=FFN_TF):
    t = x.shape[0]
    final = final_g is not None
    in_specs = [
        pl.BlockSpec((tm, D_MODEL), lambda i, j: (i, 0)),
        pl.BlockSpec((1, N_MOD, D_MODEL), lambda i, j: (i * tm // rows_per_mod, 0, 0)),
        pl.BlockSpec((1, D_MODEL), lambda i, j: (0, 0)),
        _ffn_in_weight_spec(wg, tf),
        _ffn_in_weight_spec(wu, tf),
        pl.BlockSpec((tf, D_MODEL), lambda i, j: (j, 0)),
    ]
    args = [x, mod, norm_g, wg, wu, wd]
    if final:
        in_specs.append(pl.BlockSpec((1, D_MODEL), lambda i, j: (0, 0)))
        args.append(final_g)
    out_shape = [jax.ShapeDtypeStruct((t, D_MODEL), F32)]
    out_specs = [pl.BlockSpec((tm, D_MODEL), lambda i, j: (i, 0))]
    for w, in_block, in_map, o_shape, o_block, o_map in side:
        in_specs.append(pl.BlockSpec(in_block, in_map))
        args.append(w)
        out_shape.append(jax.ShapeDtypeStruct(o_shape, BF16))
        out_specs.append(pl.BlockSpec(o_block, o_map))
    return pl.pallas_call(
        functools.partial(_ffn_kernel, mod_base=mod_base, final=final, n_side=len(side)),
        out_shape=out_shape,
        grid=(t // tm, D_FF // tf),
        in_specs=in_specs,
        out_specs=out_specs,
        scratch_shapes=[pltpu.VMEM((tm, D_MODEL), BF16)],
        compiler_params=pltpu.CompilerParams(
            dimension_semantics=("arbitrary", "arbitrary"), vmem_limit_bytes=VMEM_LIMIT),
        name="ffn_final" if final else "ffn",
    )(*args)


def _mixin_kernel(x_ref, mod_ref, g_ref, w_ref, q_ref, k_ref, v_ref, u_ref, vg_ref, *, t_seq):
    shift = mod_ref[0, 3:4, :]
    scale = mod_ref[0, 4:5, :]
    h = (_rms(x_ref[...], g_ref[...]) * (1.0 + scale) + shift).astype(BF16)
    for idx, o_ref in enumerate((q_ref, k_ref, v_ref, u_ref, vg_ref)):
        p = _dot(h, w_ref[:, idx * WIDTH_A:(idx + 1) * WIDTH_A])
        if idx == 0:
            p = p * Q_SCALE
        seq = t_seq[idx]
        if seq is None:
            o_ref[...] = p.astype(o_ref.dtype)
        else:
            per_block = max(p.shape[0] // seq, 1)
            rows = p.shape[0] // per_block
            for e in range(per_block):
                o_ref[e] = p[e * rows:(e + 1) * rows, :].T.astype(o_ref.dtype)


def _mixin(x, mod, norm_g, w_in, *, rows_per_mod, seq, k_t, kv_dtype, tm=512):
    t = x.shape[0]
    def tok(dtype):
        return jax.ShapeDtypeStruct((t, WIDTH_A), dtype), pl.BlockSpec((tm, WIDTH_A), lambda i: (i, 0))
    def feat(dtype):
        shape = jax.ShapeDtypeStruct((t // seq, WIDTH_A, seq), dtype)
        if seq >= tm:
            per_seq = seq // tm
            return shape, pl.BlockSpec((1, WIDTH_A, tm), lambda i: (i // per_seq, 0, i % per_seq))
        return shape, pl.BlockSpec((tm // seq, WIDTH_A, seq), lambda i: (i, 0, 0))
    outs = [tok(BF16), feat(kv_dtype) if k_t else tok(kv_dtype), feat(kv_dtype), tok(BF16), tok(BF16)]
    t_seq = (None, seq if k_t else None, seq, None, None)
    return pl.pallas_call(
        functools.partial(_mixin_kernel, t_seq=t_seq),
        out_shape=[o[0] for o in outs],
        grid=(t // tm,),
        in_specs=[
            pl.BlockSpec((tm, D_MODEL), lambda i: (i, 0)),
            pl.BlockSpec((1, N_MOD, D_MODEL), lambda i: (i * tm // rows_per_mod, 0, 0)),
            pl.BlockSpec((1, D_MODEL), lambda i: (0, 0)),
            pl.BlockSpec((D_MODEL, IN_WIDTH), lambda i: (0, 0), pipeline_mode=pl.Buffered(1)),
        ],
        out_specs=[o[1] for o in outs],
        compiler_params=pltpu.CompilerParams(
            dimension_semantics=("parallel",), vmem_limit_bytes=VMEM_LIMIT),
        name="mixin",
    )(x, mod, norm_g, w_in)


class _Seg(NamedTuple):
    k: jax.Array
    vt: jax.Array


def _head_scores(q, segs, biases, hh):
    lane = lax.broadcasted_iota(jnp.int32, (1, LANES), 1)
    qh = jnp.where((lane // HEAD_DIM_A) == hh, q, jnp.zeros_like(q))
    scores = []
    for seg, b in zip(segs, biases):
        s = _dot_nt(seg.k, qh)
        scores.append(s if b is None else s + b)
    return scores


def _head_probs(scores):
    m = scores[0].max(axis=0, keepdims=True)
    for s in scores[1:]:
        m = jnp.maximum(m, s.max(axis=0, keepdims=True))
    return [jnp.exp2(s - m).astype(BF16) for s in scores]


def _head_output(probs, segs, hh):
    row = lax.broadcasted_iota(jnp.int32, (LANES, 1), 0)
    ones_idx = ((hh + 1) % HEADS_PER_BLOCK) * HEAD_DIM_A
    head_rows = (row // HEAD_DIM_A) == hh
    ones_col = jnp.where(row == ones_idx, 1.0, 0.0)
    acc = None
    for p, seg in zip(probs, segs):
        vt_aug = jnp.where(head_rows, seg.vt.astype(F32), ones_col).astype(BF16)
        d = _dot(vt_aug, p)
        acc = d if acc is None else acc + d
    return jnp.where(head_rows, acc / acc[ones_idx:ones_idx + 1, :], 0.0)


def _attention_heads(q_ref, o_ref, seg_fn, bias_fn, pairs_per_iter, out_lag):
    def body(it, carry):
        pairs = [it * pairs_per_iter + sub for sub in range(pairs_per_iter)]
        cols = [pl.ds(pl.multiple_of(p * LANES, LANES), LANES) for p in pairs]
        heads = [(i, hh) for i in range(pairs_per_iter) for hh in range(HEADS_PER_BLOCK)]
        operands = {}

        def load(i):
            if i not in operands:
                operands[i] = (q_ref[:, cols[i]], seg_fn(pairs[i], cols[i]))
            return operands[i]

        def scores_of(t):
            i, hh = heads[t]
            q, segs = load(i)
            return _head_scores(q, segs, [bias_fn(pairs[i], hh, k) for k in range(len(segs))], hh)

        scores, probs, out = {}, {}, None
        for t in range(len(heads) + out_lag):
            if t < len(heads):
                scores[t] = scores_of(t)
            if PROBS_LAG <= t < len(heads) + PROBS_LAG:
                probs[t - PROBS_LAG] = _head_probs(scores.pop(t - PROBS_LAG))
            if t >= out_lag:
                i, hh = heads[t - out_lag]
                o = _head_output(probs.pop(t - out_lag), load(i)[1], hh)
                out = o if hh == 0 else out + o
                if hh == HEADS_PER_BLOCK - 1:
                    o_ref[:, cols[i]] = out.T.astype(o_ref.dtype)
        return carry

    lax.fori_loop(0, N_HEAD_BLOCKS // pairs_per_iter, body, 0)


def _keys_from_t(kt):
    return kt.T.astype(BF16)


def _ctx_attn_kernel(q_ref, kt_ref, vt_ref, o_ref):
    def seg_fn(p, cols):
        return [_Seg(_keys_from_t(kt_ref[0, cols, :]), vt_ref[0, cols, :])]

    _attention_heads(q_ref, o_ref, seg_fn, lambda p, hh, i: None, N_HEAD_BLOCKS, CTX_OUT_LAG)


def _ctx_attention(q, kt, vt, seq):
    t = q.shape[0]
    qspec = pl.BlockSpec((seq, WIDTH_A), lambda b: (b, 0))
    tspec = pl.BlockSpec((1, WIDTH_A, seq), lambda b: (b, 0, 0))
    return pl.pallas_call(
        _ctx_attn_kernel,
        out_shape=jax.ShapeDtypeStruct((t, WIDTH_A), BF16),
        grid=(t // seq,),
        in_specs=[qspec, tspec, tspec],
        out_specs=qspec,
        compiler_params=pltpu.CompilerParams(
            dimension_semantics=("parallel",), vmem_limit_bytes=VMEM_LIMIT),
        name="ctx_attn",
    )(q, kt, vt)


def _nbr_attn_kernel(q_ref, k0_ref, k1_ref, k2_ref, vt0_ref, vt1_ref, vt2_ref,
                     ckt_ref, cvt_ref, bias_ref, o_ref):
    def seg_fn(p, cols):
        segs = [_Seg(k_ref[:, cols], vt_ref[0, cols, :])
                for k_ref, vt_ref in ((k0_ref, vt0_ref), (k1_ref, vt1_ref), (k2_ref, vt2_ref))]
        segs.append(_Seg(_keys_from_t(ckt_ref[0, cols, :]), cvt_ref[0, cols, :]))
        return segs

    def bias_fn(p, hh, i):
        if i >= WIN_BLOCKS:
            return None
        return bias_ref[0, HEADS_PER_BLOCK * p + hh, i * Q_TOK:(i + 1) * Q_TOK, :]

    _attention_heads(q_ref, o_ref, seg_fn, bias_fn, PAIRS_PER_ITER, NBR_OUT_LAG)


def _nbr_attention(q, k, vt, ckt, cvt, bias, n_tok):
    t = q.shape[0]
    batch = t // n_tok
    past = ckt.shape[2]
    qb = n_tok // Q_TOK
    def win(m):
        return jnp.clip(m - 1, 0, qb - WIN_BLOCKS)
    def variant(m):
        return jnp.minimum(m, 1) + jnp.maximum(m - (qb - 2), 0)
    qspec = pl.BlockSpec((Q_TOK, WIDTH_A), lambda m, b: (b * qb + m, 0))
    def kspec(w):
        return pl.BlockSpec((Q_TOK, WIDTH_A), lambda m, b: (b * qb + win(m) + w, 0))
    def vtspec(w):
        return pl.BlockSpec((1, WIDTH_A, Q_TOK), lambda m, b: (b, 0, win(m) + w))
    cspec = pl.BlockSpec((1, WIDTH_A, past), lambda m, b: (b, 0, 0))
    bspec = pl.BlockSpec((1, N_HEADS_A, WIN_BLOCKS * Q_TOK, Q_TOK),
                         lambda m, b: (variant(m), 0, 0, 0))
    return pl.pallas_call(
        _nbr_attn_kernel,
        out_shape=jax.ShapeDtypeStruct((t, WIDTH_A), BF16),
        grid=(qb, batch),
        in_specs=[qspec] + [kspec(w) for w in range(WIN_BLOCKS)]
                 + [vtspec(w) for w in range(WIN_BLOCKS)] + [cspec, cspec, bspec],
        out_specs=qspec,
        compiler_params=pltpu.CompilerParams(
            dimension_semantics=("parallel", "arbitrary"), vmem_limit_bytes=VMEM_LIMIT),
        name="nbr_attn",
    )(q, k, k, k, vt, vt, vt, ckt, cvt, bias)


def _bias_kernel(rpb_ref, o_ref, *, rows):
    h = pl.program_id(0)
    n_dr = 2 * NA_ROWS - 1
    n_dc = 2 * NA_COLS - 1
    kc = lax.broadcasted_iota(jnp.int32, (GRID_W, LANES), 0)
    ln = lax.broadcasted_iota(jnp.int32, (GRID_W, LANES), 1)
    qc = ln % GRID_W
    col_start = jnp.clip(qc - NA_COLS // 2, 0, GRID_W - NA_COLS)
    col_ok = (kc >= col_start) & (kc < col_start + NA_COLS)
    neg = jnp.full((GRID_W, LANES), NEG_INF, F32)
    lane8 = lax.broadcasted_iota(jnp.int32, (8, LANES), 1)
    toeplitz = []
    for dr in range(n_dr):
        r8 = jnp.zeros((8, LANES), F32)
        for dc in range(n_dc):
            r8 = jnp.where(lane8 == n_dc - 1 - dc, rpb_ref[h * (n_dr * n_dc) + dr * n_dc + dc], r8)
        r = jnp.concatenate([r8] * (GRID_W // 8), axis=0)
        lo = pltpu.roll(r, LANES - (NA_COLS - 1), 1, stride=1, stride_axis=0)
        hi = pltpu.roll(r, GRID_W - (NA_COLS - 1), 1, stride=1, stride_axis=0)
        t = jnp.where(ln < GRID_W, lo, hi)
        toeplitz.append(jnp.where(col_ok, t * LOG2E, neg))
    qb = rows // Q_ROWS
    kr = min(NA_ROWS, rows)
    for var, m in enumerate((0, 1, qb - 1)):
        wb = min(max(m - 1, 0), qb - WIN_BLOCKS)
        for t in range(WIN_BLOCKS * Q_ROWS):
            key_row = Q_ROWS * wb + t
            for ap in range(Q_ROWS // 2):
                halves = []
                for a in (2 * ap, 2 * ap + 1):
                    r = Q_ROWS * m + a
                    row_start = min(max(r - kr // 2, 0), rows - kr)
                    ok = row_start <= key_row < row_start + kr
                    halves.append(toeplitz[key_row - r + NA_ROWS - 1] if ok else neg)
                tile = jnp.where(ln < GRID_W, halves[0], halves[1])
                o_ref[var, 0, t * GRID_W:(t + 1) * GRID_W, ap * LANES:(ap + 1) * LANES] = tile


def _bias_table(rpb, rows):
    return pl.pallas_call(
        functools.partial(_bias_kernel, rows=rows),
        out_shape=jax.ShapeDtypeStruct((3, N_HEADS_A, WIN_BLOCKS * Q_TOK, Q_TOK), F32),
        grid=(N_HEADS_A,),
        in_specs=[pl.BlockSpec(memory_space=pltpu.SMEM)],
        out_specs=pl.BlockSpec((3, 1, WIN_BLOCKS * Q_TOK, Q_TOK), lambda h: (0, h, 0, 0)),
        compiler_params=pltpu.CompilerParams(
            dimension_semantics=("arbitrary",), vmem_limit_bytes=VMEM_LIMIT),
        name="nbr_bias",
    )(rpb.reshape(-1))


def _mixout_kernel(x_ref, a_ref, u_ref, vg_ref, mod_ref, gn_ref, ws_ref, bs_ref,
                   na_ref, nb_ref, wo_ref, o_ref, lhs_ref, ssq_ref):
    tm = x_ref.shape[0]
    gate = mod_ref[0, 5:6, :]
    ya = _rms(a_ref[...].astype(F32), na_ref[...]).astype(BF16)
    n_parts = N_GROUPS_B // MIXOUT_GROUPS_PER_DOT
    a_cols = D_MODEL // n_parts

    def attn_part(n):
        cols = slice(n * a_cols, (n + 1) * a_cols)
        o_ref[:, cols] = x_ref[:, cols] + gate[:, cols] * _dot(ya, wo_ref[:WIDTH_A, cols])

    def gated_tile(g, c):
        rows = slice(c * CHUNK, (c + 1) * CHUNK)
        cols = slice(g * GROUP_DIM_B, (g + 1) * GROUP_DIM_B)
        vn = _rms(_gelu_tanh(vg_ref[rows, cols].astype(F32)), gn_ref[:, cols]).astype(BF16)
        gm = _gelu_tanh(u_ref[rows, cols].astype(F32)) * (_dot(ws_ref[g], vn) + bs_ref[:, g:g + 1])
        lhs_ref[rows, cols] = (gm * nb_ref[:, cols]).astype(BF16)
        sq = jnp.broadcast_to(jnp.sum(gm * gm, axis=-1, keepdims=True), (CHUNK, LANES))
        ssq_ref[rows, :] = sq if g == 0 else ssq_ref[rows, :] + sq

    yb = None
    for n in range(n_parts):
        attn_part(n)
        for g in range(n * MIXOUT_GROUPS_PER_DOT, (n + 1) * MIXOUT_GROUPS_PER_DOT):
            for c in range(tm // CHUNK):
                gated_tile(g, c)
        ks = slice(n * MIXOUT_GROUPS_PER_DOT * GROUP_DIM_B, (n + 1) * MIXOUT_GROUPS_PER_DOT * GROUP_DIM_B)
        d = _dot(lhs_ref[:, ks], wo_ref[WIDTH_A + ks.start:WIDTH_A + ks.stop, :])
        yb = d if yb is None else yb + d
    row_scale = lax.rsqrt(ssq_ref[...] * (1.0 / WIDTH_B) + EPS)
    o_ref[...] += gate * (_lane_tile(row_scale, D_MODEL) * yb)


def _mixout(x, a, u, vg, mod, gmlp_norm, w_s, b_s_t, na_g, nb_g, w_out, *, rows_per_mod, tm=512):
    t = x.shape[0]
    tok = lambda w: pl.BlockSpec((tm, w), lambda i: (i, 0))
    full = lambda shape: pl.BlockSpec(shape, lambda i: (0,) * len(shape))
    return pl.pallas_call(
        _mixout_kernel,
        out_shape=jax.ShapeDtypeStruct((t, D_MODEL), F32),
        grid=(t // tm,),
        in_specs=[
            tok(D_MODEL), tok(WIDTH_A), tok(WIDTH_B), tok(WIDTH_B),
            pl.BlockSpec((1, N_MOD, D_MODEL), lambda i: (i * tm // rows_per_mod, 0, 0)),
            full((1, WIDTH_B)),
            full((N_GROUPS_B, CHUNK, CHUNK)),
            full((CHUNK, N_GROUPS_B)),
            full((1, WIDTH_A)),
            full((1, WIDTH_B)),
            pl.BlockSpec((MIX_WIDTH, D_MODEL), lambda i: (0, 0), pipeline_mode=pl.Buffered(1)),
        ],
        out_specs=tok(D_MODEL),
        scratch_shapes=[pltpu.VMEM((tm, WIDTH_B), BF16), pltpu.VMEM((tm, LANES), F32)],
        compiler_params=pltpu.CompilerParams(
            dimension_semantics=("parallel",), vmem_limit_bytes=VMEM_LIMIT),
        name="mixout",
    )(x, a, u, vg, mod, gmlp_norm, w_s, b_s_t, na_g, nb_g, w_out)


def kernel(x_prompt, x_sample, cache_k, cache_v, c, c_ctx, w_ada, b_ada, ffn1_norm, ffn1_w_gate, ffn1_w_up, ffn1_w_down, mix_norm, w_in, rpb, gmlp_norm, w_s, b_s, out_norm_a, out_norm_b, w_out, ffn2_norm, ffn2_w_gate, ffn2_w_up, ffn2_w_down, final_norm):
    batch, seq, _ = x_prompt.shape
    dec_batch, dec_seq, _ = x_sample.shape
    depth = w_ada.shape[0]
    x_ctx = x_prompt.reshape(batch * seq, D_MODEL)
    x_lat = x_sample.reshape(dec_batch * dec_seq, D_MODEL)
    final_g = final_norm.reshape(1, D_MODEL)

    cvec = jnp.concatenate([c_ctx[None, :], c], axis=0)
    mod_rows = -(-cvec.shape[0] // 8) * 8
    cvec = jnp.pad(cvec, ((0, mod_rows - cvec.shape[0]), (0, 0)))

    new_k, new_v = [], []
    for l in range(depth):
        last = l == depth - 1
        mod = _adaln(cvec, w_ada[l], b_ada[l][None, :]).reshape(mod_rows, N_MOD, D_MODEL)
        mods = (mod[0:1], mod[1:1 + dec_batch])
        rows_per_mod = (batch * seq, dec_seq)

        f1 = (ffn1_norm[l][None, :], ffn1_w_gate[l].astype(BF16), ffn1_w_up[l].astype(BF16),
              ffn1_w_down[l].astype(BF16))
        mix_g = mix_norm[l][None, :]
        w_s_l = w_s[l].astype(BF16)
        b_s_t = b_s[l].T
        gn = gmlp_norm[l][None, :]
        na_g = out_norm_a[l][None, :]
        nb_g = out_norm_b[l][None, :]
        bias = _bias_table(rpb[l], dec_seq // GRID_W)
        ckt = jnp.transpose(cache_k[:, l], (0, 2, 3, 1)).reshape(dec_batch, WIDTH_A, -1)
        cvt = jnp.transpose(cache_v[:, l], (0, 2, 3, 1)).reshape(dec_batch, WIDTH_A, -1)

        n_j = D_FF // FFN_TF
        n_ctx, n_lat = x_ctx.shape[0] // FFN_TM, x_lat.shape[0] // FFN_TM
        x_ctx, w_in_l, w_out_l = _ffn(
            x_ctx, mods[0], *f1, mod_base=0, rows_per_mod=rows_per_mod[0],
            side=(_cast_plan(w_in[l], n_ctx, n_j), _cast_plan(w_out[l], n_ctx, n_j)))
        x_lat, *f2_w = _ffn(
            x_lat, mods[1], *f1, mod_base=0, rows_per_mod=rows_per_mod[1],
            side=(_cast_plan_tiled(ffn2_w_gate[l], n_lat, n_j), _cast_plan_tiled(ffn2_w_up[l], n_lat, n_j),
                  _cast_plan_t(ffn2_w_down[l], n_lat, n_j)))
        f2 = (ffn2_norm[l][None, :], *f2_w)

        xs = []
        for path, (x, m, rpm) in enumerate(zip((x_ctx, x_lat), mods, rows_per_mod)):
            q, k, v, u, vg = _mixin(x, m, mix_g, w_in_l, rows_per_mod=rpm,
                                    seq=seq if path == 0 else dec_seq, k_t=path == 0,
                                    kv_dtype=F32 if path == 0 else BF16)
            if path == 0:
                a = _ctx_attention(q, k, v, seq)
                for store, kt in ((new_k, k), (new_v, v)):
                    kt = kt.reshape(batch, N_HEADS_A, HEAD_DIM_A, seq)
                    store.append(jnp.transpose(kt, (0, 3, 1, 2)))
            else:
                a = _nbr_attention(q, k, v, ckt, cvt, bias, dec_seq)
            x = _mixout(x, a, u, vg, m, gn, w_s_l, b_s_t, na_g, nb_g, w_out_l, rows_per_mod=rpm)
            x, = _ffn(x, m, *f2, mod_base=6, rows_per_mod=rpm,
                      final_g=final_g if last else None)
            xs.append(x)
        x_ctx, x_lat = xs

    if depth == 0:
        raise ValueError("depth must be positive")
    y_prompt = x_ctx.reshape(batch, seq, D_MODEL)
    y_sample = x_lat.reshape(dec_batch, dec_seq, D_MODEL)
    return (y_prompt, y_sample, jnp.stack(new_k, axis=1), jnp.stack(new_v, axis=1))
```

```python
import functools
import math
from typing import NamedTuple

import jax
import jax.numpy as jnp
from jax import lax
from jax.experimental import pallas as pl
from jax.experimental.pallas import tpu as pltpu

D_MODEL = 2048
N_HEADS_A = 16
HEAD_DIM_A = 64
WIDTH_A = N_HEADS_A * HEAD_DIM_A
GRID_W = 64
NA_ROWS = 8
NA_COLS = 16
N_GROUPS_B = 8
GROUP_DIM_B = 128
WIDTH_B = N_GROUPS_B * GROUP_DIM_B
CHUNK = 128
MIX_WIDTH = WIDTH_A + WIDTH_B
IN_WIDTH = 3 * WIDTH_A + 2 * WIDTH_B
D_FF = 5632
N_MOD = 9
EPS = 1e-6
NEG_INF = -1e30

LANES = 128
HEADS_PER_BLOCK = LANES // HEAD_DIM_A
N_HEAD_BLOCKS = N_HEADS_A // HEADS_PER_BLOCK
Q_ROWS = 4
Q_TOK = Q_ROWS * GRID_W
WIN_BLOCKS = 3
VMEM_LIMIT = 62 * 1024 * 1024
FFN_OUT_CHUNK = 512
FFN_TM = 1024
FFN_TF = 512
PAIRS_PER_ITER = 8
PROBS_LAG = 2
NBR_OUT_LAG = 3
CTX_OUT_LAG = 4
MIXOUT_GROUPS_PER_DOT = 4
LOG2E = math.log2(math.e)
Q_SCALE = HEAD_DIM_A ** -0.5 * LOG2E

F32 = jnp.float32
BF16 = jnp.bfloat16


def _dot(a, b):
    return jnp.dot(a, b, preferred_element_type=F32)


def _dot_nt(a, b):
    return lax.dot_general(a, b, (((1,), (1,)), ((), ())), preferred_element_type=F32)


def _rms(x, g):
    ms = jnp.mean(x * x, axis=-1, keepdims=True)
    return x * lax.rsqrt(ms + EPS) * g


def _silu(x):
    return x * jax.nn.sigmoid(x)


def _gelu_tanh(x):
    c = 0.7978845608028654
    return 0.5 * x * (1.0 + jnp.tanh(c * (x + 0.044715 * (x * x * x))))


def _lane_tile(v, width):
    return jnp.concatenate([v] * (width // LANES), axis=1)


def _adaln_kernel(c_ref, w_ref, b_ref, o_ref):
    s = _silu(c_ref[...]).astype(BF16)
    o_ref[...] = _dot(s, w_ref[...].astype(BF16)) + b_ref[...]


def _adaln(cvec, w_ada, b_ada, tn=1024):
    rows = cvec.shape[0]
    n = w_ada.shape[1]
    return pl.pallas_call(
        _adaln_kernel,
        out_shape=jax.ShapeDtypeStruct((rows, n), F32),
        grid=(n // tn,),
        in_specs=[
            pl.BlockSpec((rows, D_MODEL), lambda j: (0, 0)),
            pl.BlockSpec((D_MODEL, tn), lambda j: (0, j)),
            pl.BlockSpec((1, tn), lambda j: (0, j)),
        ],
        out_specs=pl.BlockSpec((rows, tn), lambda j: (0, j)),
        compiler_params=pltpu.CompilerParams(
            dimension_semantics=("arbitrary",), vmem_limit_bytes=VMEM_LIMIT),
        name="adaln",
    )(cvec, w_ada, b_ada)


def _ffn_kernel(*refs, mod_base, final, n_side):
    x_ref, mod_ref, g_ref, wg_ref, wu_ref, wd_ref = refs[:6]
    n_in = 7 if final else 6
    fg_ref = refs[6] if final else None
    side_in = refs[n_in:n_in + n_side]
    o_ref = refs[n_in + n_side]
    side_out = refs[n_in + n_side + 1:n_in + 2 * n_side + 1]
    h_ref, = refs[n_in + 2 * n_side + 1:]
    j = pl.program_id(1)

    def step(acc_ref, h):
        for src_ref, dst_ref in zip(side_in, side_out):
            if len(dst_ref.shape) == 3:
                dst_ref[0] = src_ref[...].astype(BF16)
            else:
                dst_ref[...] = src_ref[...].astype(BF16)
        wg = wg_ref[0] if len(wg_ref.shape) == 3 else wg_ref[...]
        wu = wu_ref[0] if len(wu_ref.shape) == 3 else wu_ref[...]
        a = (_silu(_dot(h, wg)) * _dot(h, wu)).astype(BF16)
        half_gate = 0.5 * mod_ref[0, mod_base + 2:mod_base + 3, :]
        for n in range(D_MODEL // FFN_OUT_CHUNK):
            cols = slice(n * FFN_OUT_CHUNK, (n + 1) * FFN_OUT_CHUNK)
            o_ref[:, cols] = acc_ref[:, cols] + half_gate[:, cols] * _dot(a, wd_ref[:, cols])

    @pl.when(j == 0)
    def _():
        x = x_ref[...]
        shift = mod_ref[0, mod_base:mod_base + 1, :]
        gain = g_ref[...] * (1.0 + mod_ref[0, mod_base + 1:mod_base + 2, :])
        ms = jnp.mean(x * x, axis=-1, keepdims=True)
        h = (x * lax.rsqrt(ms + EPS) * gain + shift).astype(BF16)
        h_ref[...] = h
        step(x_ref, h)

    last = pl.num_programs(1) - 1

    @pl.when((j > 0) & (j < last) if final else j > 0)
    def _():
        step(o_ref, h_ref[...])

    if final:
        @pl.when(j == last)
        def _():
            step(o_ref, h_ref[...])
            o_ref[...] = _rms(o_ref[...], fg_ref[...])


def _cast_plan(w, n_i, n_j):
    rows, cols = w.shape
    col_blocks = max(d for d in range(1, n_j + 1) if cols % d == 0 and (cols // d) % LANES == 0)
    assert rows % n_i == 0 and (rows // n_i) % 16 == 0
    block = (rows // n_i, cols // col_blocks)
    index_map = lambda i, j: (i, jnp.minimum(j, col_blocks - 1))
    return w, block, index_map, w.shape, block, index_map


def _cast_plan_t(w, n_i, n_j):
    rows, cols = w.shape
    assert rows % n_j == 0 and (rows // n_j) % 16 == 0 and cols % n_i == 0 and (cols // n_i) % LANES == 0
    block = (rows // n_j, cols // n_i)
    index_map = lambda i, j: (j, i)
    return w, block, index_map, w.shape, block, index_map


def _cast_plan_tiled(w, n_i, n_j):
    rows, cols = w.shape
    assert rows % n_i == 0 and (rows // n_i) % 16 == 0 and cols % n_j == 0 and (cols // n_j) % LANES == 0
    tile = cols // n_j
    return (w, (rows // n_i, tile), lambda i, j: (i, j),
            (n_j, rows, tile), (1, rows // n_i, tile), lambda i, j: (j, i, 0))


def _ffn_in_weight_spec(w, tf):
    if w.ndim == 3:
        return pl.BlockSpec((1, D_MODEL, tf), lambda i, j: (j, 0, 0))
    return pl.BlockSpec((D_MODEL, tf), lambda i, j: (0, j))


def _ffn(x, mod, norm_g, wg, wu, wd, *, mod_base, rows_per_mod, final_g=None, side=(),
         tm=FFN_TM, tf=FFN_TF):
    t = x.shape[0]
    final = final_g is not None
    in_specs = [
        pl.BlockSpec((tm, D_MODEL), lambda i, j: (i, 0)),
        pl.BlockSpec((1, N_MOD, D_MODEL), lambda i, j: (i * tm // rows_per_mod, 0, 0)),
        pl.BlockSpec((1, D_MODEL), lambda i, j: (0, 0)),
        _ffn_in_weight_spec(wg, tf),
        _ffn_in_weight_spec(wu, tf),
        pl.BlockSpec((tf, D_MODEL), lambda i, j: (j, 0)),
    ]
    args = [x, mod, norm_g, wg, wu, wd]
    if final:
        in_specs.append(pl.BlockSpec((1, D_MODEL), lambda i, j: (0, 0)))
        args.append(final_g)
    out_shape = [jax.ShapeDtypeStruct((t, D_MODEL), F32)]
    out_specs = [pl.BlockSpec((tm, D_MODEL), lambda i, j: (i, 0))]
    for w, in_block, in_map, o_shape, o_block, o_map in side:
        in_specs.append(pl.BlockSpec(in_block, in_map))
        args.append(w)
        out_shape.append(jax.ShapeDtypeStruct(o_shape, BF16))
        out_specs.append(pl.BlockSpec(o_block, o_map))
    return pl.pallas_call(
        functools.partial(_ffn_kernel, mod_base=mod_base, final=final, n_side=len(side)),
        out_shape=out_shape,
        grid=(t // tm, D_FF // tf),
        in_specs=in_specs,
        out_specs=out_specs,
        scratch_shapes=[pltpu.VMEM((tm, D_MODEL), BF16)],
        compiler_params=pltpu.CompilerParams(
            dimension_semantics=("arbitrary", "arbitrary"), vmem_limit_bytes=VMEM_LIMIT),
        name="ffn_final" if final else "ffn",
    )(*args)


def _mixin_kernel(x_ref, mod_ref, g_ref, w_ref, q_ref, k_ref, v_ref, u_ref, vg_ref, *, t_seq):
    shift = mod_ref[0, 3:4, :]
    scale = mod_ref[0, 4:5, :]
    h = (_rms(x_ref[...], g_ref[...]) * (1.0 + scale) + shift).astype(BF16)
    for idx, o_ref in enumerate((q_ref, k_ref, v_ref, u_ref, vg_ref)):
        p = _dot(h, w_ref[:, idx * WIDTH_A:(idx + 1) * WIDTH_A])
        if idx == 0:
            p = p * Q_SCALE
        seq = t_seq[idx]
        if seq is None:
            o_ref[...] = p.astype(o_ref.dtype)
        else:
            per_block = max(p.shape[0] // seq, 1)
            rows = p.shape[0] // per_block
            for e in range(per_block):
                o_ref[e] = p[e * rows:(e + 1) * rows, :].T.astype(o_ref.dtype)


def _mixin(x, mod, norm_g, w_in, *, rows_per_mod, seq, k_t, kv_dtype, tm=512):
    t = x.shape[0]
    def tok(dtype):
        return jax.ShapeDtypeStruct((t, WIDTH_A), dtype), pl.BlockSpec((tm, WIDTH_A), lambda i: (i, 0))
    def feat(dtype):
        shape = jax.ShapeDtypeStruct((t // seq, WIDTH_A, seq), dtype)
        if seq >= tm:
            per_seq = seq // tm
            return shape, pl.BlockSpec((1, WIDTH_A, tm), lambda i: (i // per_seq, 0, i % per_seq))
        return shape, pl.BlockSpec((tm // seq, WIDTH_A, seq), lambda i: (i, 0, 0))
    outs = [tok(BF16), feat(kv_dtype) if k_t else tok(kv_dtype), feat(kv_dtype), tok(BF16), tok(BF16)]
    t_seq = (None, seq if k_t else None, seq, None, None)
    return pl.pallas_call(
        functools.partial(_mixin_kernel, t_seq=t_seq),
        out_shape=[o[0] for o in outs],
        grid=(t // tm,),
        in_specs=[
            pl.BlockSpec((tm, D_MODEL), lambda i: (i, 0)),
            pl.BlockSpec((1, N_MOD, D_MODEL), lambda i: (i * tm // rows_per_mod, 0, 0)),
            pl.BlockSpec((1, D_MODEL), lambda i: (0, 0)),
            pl.BlockSpec((D_MODEL, IN_WIDTH), lambda i: (0, 0), pipeline_mode=pl.Buffered(1)),
        ],
        out_specs=[o[1] for o in outs],
        compiler_params=pltpu.CompilerParams(
            dimension_semantics=("parallel",), vmem_limit_bytes=VMEM_LIMIT),
        name="mixin",
    )(x, mod, norm_g, w_in)


class _Seg(NamedTuple):
    k: jax.Array
    vt: jax.Array


def _head_scores(q, segs, biases, hh):
    lane = lax.broadcasted_iota(jnp.int32, (1, LANES), 1)
    qh = jnp.where((lane // HEAD_DIM_A) == hh, q, jnp.zeros_like(q))
    scores = []
    for seg, b in zip(segs, biases):
        s = _dot_nt(seg.k, qh)
        scores.append(s if b is None else s + b)
    return scores


def _head_probs(scores):
    m = scores[0].max(axis=0, keepdims=True)
    for s in scores[1:]:
        m = jnp.maximum(m, s.max(axis=0, keepdims=True))
    return [jnp.exp2(s - m).astype(BF16) for s in scores]


def _head_output(probs, segs, hh):
    row = lax.broadcasted_iota(jnp.int32, (LANES, 1), 0)
    ones_idx = ((hh + 1) % HEADS_PER_BLOCK) * HEAD_DIM_A
    head_rows = (row // HEAD_DIM_A) == hh
    ones_col = jnp.where(row == ones_idx, 1.0, 0.0)
    acc = None
    for p, seg in zip(probs, segs):
        vt_aug = jnp.where(head_rows, seg.vt.astype(F32), ones_col).astype(BF16)
        d = _dot(vt_aug, p)
        acc = d if acc is None else acc + d
    return jnp.where(head_rows, acc / acc[ones_idx:ones_idx + 1, :], 0.0)


def _attention_heads(q_ref, o_ref, seg_fn, bias_fn, pairs_per_iter, out_lag):
    def body(it, carry):
        pairs = [it * pairs_per_iter + sub for sub in range(pairs_per_iter)]
        cols = [pl.ds(pl.multiple_of(p * LANES, LANES), LANES) for p in pairs]
        heads = [(i, hh) for i in range(pairs_per_iter) for hh in range(HEADS_PER_BLOCK)]
        operands = {}

        def load(i):
            if i not in operands:
                operands[i] = (q_ref[:, cols[i]], seg_fn(pairs[i], cols[i]))
            return operands[i]

        def scores_of(t):
            i, hh = heads[t]
            q, segs = load(i)
            return _head_scores(q, segs, [bias_fn(pairs[i], hh, k) for k in range(len(segs))], hh)

        scores, probs, out = {}, {}, None
        for t in range(len(heads) + out_lag):
            if t < len(heads):
                scores[t] = scores_of(t)
            if PROBS_LAG <= t < len(heads) + PROBS_LAG:
                probs[t - PROBS_LAG] = _head_probs(scores.pop(t - PROBS_LAG))
            if t >= out_lag:
                i, hh = heads[t - out_lag]
                o = _head_output(probs.pop(t - out_lag), load(i)[1], hh)
                out = o if hh == 0 else out + o
                if hh == HEADS_PER_BLOCK - 1:
                    o_ref[:, cols[i]] = out.T.astype(o_ref.dtype)
        return carry

    lax.fori_loop(0, N_HEAD_BLOCKS // pairs_per_iter, body, 0)


def _keys_from_t(kt):
    return kt.T.astype(BF16)


def _ctx_attn_kernel(q_ref, kt_ref, vt_ref, o_ref):
    def seg_fn(p, cols):
        return [_Seg(_keys_from_t(kt_ref[0, cols, :]), vt_ref[0, cols, :])]

    _attention_heads(q_ref, o_ref, seg_fn, lambda p, hh, i: None, N_HEAD_BLOCKS, CTX_OUT_LAG)


def _ctx_attention(q, kt, vt, seq):
    t = q.shape[0]
    qspec = pl.BlockSpec((seq, WIDTH_A), lambda b: (b, 0))
    tspec = pl.BlockSpec((1, WIDTH_A, seq), lambda b: (b, 0, 0))
    return pl.pallas_call(
        _ctx_attn_kernel,
        out_shape=jax.ShapeDtypeStruct((t, WIDTH_A), BF16),
        grid=(t // seq,),
        in_specs=[qspec, tspec, tspec],
        out_specs=qspec,
        compiler_params=pltpu.CompilerParams(
            dimension_semantics=("parallel",), vmem_limit_bytes=VMEM_LIMIT),
        name="ctx_attn",
    )(q, kt, vt)


def _nbr_attn_kernel(q_ref, k0_ref, k1_ref, k2_ref, vt0_ref, vt1_ref, vt2_ref,
                     ckt_ref, cvt_ref, bias_ref, o_ref):
    local = ((k0_ref, vt0_ref), (k1_ref, vt1_ref), (k2_ref, vt2_ref))

    def run(windows):
        def seg_fn(p, cols):
            segs = [_Seg(local[w][0][:, cols], local[w][1][0, cols, :]) for w in windows]
            segs.append(_Seg(_keys_from_t(ckt_ref[0, cols, :]), cvt_ref[0, cols, :]))
            return segs

        def bias_fn(p, hh, i):
            if i >= len(windows):
                return None
            w = windows[i]
            return bias_ref[0, HEADS_PER_BLOCK * p + hh, w * Q_TOK:(w + 1) * Q_TOK, :]

        _attention_heads(q_ref, o_ref, seg_fn, bias_fn, PAIRS_PER_ITER, NBR_OUT_LAG)

    m = pl.program_id(0)
    last = pl.num_programs(0) - 1
    pl.when(m == 0)(lambda: run(tuple(range(WIN_BLOCKS - 1))))
    pl.when(m == last)(lambda: run(tuple(range(1, WIN_BLOCKS))))
    pl.when((m > 0) & (m < last))(lambda: run(tuple(range(WIN_BLOCKS))))


def _nbr_attention(q, k, vt, ckt, cvt, bias, n_tok):
    t = q.shape[0]
    batch = t // n_tok
    past = ckt.shape[2]
    qb = n_tok // Q_TOK
    assert qb > WIN_BLOCKS and NA_ROWS <= (WIN_BLOCKS - 1) * Q_ROWS
    def win(m):
        return jnp.clip(m - 1, 0, qb - WIN_BLOCKS)
    def variant(m):
        return jnp.minimum(m, 1) + jnp.maximum(m - (qb - 2), 0)
    qspec = pl.BlockSpec((Q_TOK, WIDTH_A), lambda m, b: (b * qb + m, 0))
    def kspec(w):
        return pl.BlockSpec((Q_TOK, WIDTH_A), lambda m, b: (b * qb + win(m) + w, 0))
    def vtspec(w):
        return pl.BlockSpec((1, WIDTH_A, Q_TOK), lambda m, b: (b, 0, win(m) + w))
    cspec = pl.BlockSpec((1, WIDTH_A, past), lambda m, b: (b, 0, 0))
    bspec = pl.BlockSpec((1, N_HEADS_A, WIN_BLOCKS * Q_TOK, Q_TOK),
                         lambda m, b: (variant(m), 0, 0, 0))
    return pl.pallas_call(
        _nbr_attn_kernel,
        out_shape=jax.ShapeDtypeStruct((t, WIDTH_A), BF16),
        grid=(qb, batch),
        in_specs=[qspec] + [kspec(w) for w in range(WIN_BLOCKS)]
                 + [vtspec(w) for w in range(WIN_BLOCKS)] + [cspec, cspec, bspec],
        out_specs=qspec,
        compiler_params=pltpu.CompilerParams(
            dimension_semantics=("parallel", "arbitrary"), vmem_limit_bytes=VMEM_LIMIT),
        name="nbr_attn",
    )(q, k, k, k, vt, vt, vt, ckt, cvt, bias)


def _bias_kernel(rpb_ref, o_ref, *, rows):
    h = pl.program_id(0)
    n_dr = 2 * NA_ROWS - 1
    n_dc = 2 * NA_COLS - 1
    kc = lax.broadcasted_iota(jnp.int32, (GRID_W, LANES), 0)
    ln = lax.broadcasted_iota(jnp.int32, (GRID_W, LANES), 1)
    qc = ln % GRID_W
    col_start = jnp.clip(qc - NA_COLS // 2, 0, GRID_W - NA_COLS)
    col_ok = (kc >= col_start) & (kc < col_start + NA_COLS)
    neg = jnp.full((GRID_W, LANES), NEG_INF, F32)
    lane8 = lax.broadcasted_iota(jnp.int32, (8, LANES), 1)
    toeplitz = []
    for dr in range(n_dr):
        r8 = jnp.zeros((8, LANES), F32)
        for dc in range(n_dc):
            r8 = jnp.where(lane8 == n_dc - 1 - dc, rpb_ref[h * (n_dr * n_dc) + dr * n_dc + dc], r8)
        r = jnp.concatenate([r8] * (GRID_W // 8), axis=0)
        lo = pltpu.roll(r, LANES - (NA_COLS - 1), 1, stride=1, stride_axis=0)
        hi = pltpu.roll(r, GRID_W - (NA_COLS - 1), 1, stride=1, stride_axis=0)
        t = jnp.where(ln < GRID_W, lo, hi)
        toeplitz.append(jnp.where(col_ok, t * LOG2E, neg))
    qb = rows // Q_ROWS
    kr = min(NA_ROWS, rows)
    for var, m in enumerate((0, 1, qb - 1)):
        wb = min(max(m - 1, 0), qb - WIN_BLOCKS)
        for t in range(WIN_BLOCKS * Q_ROWS):
            key_row = Q_ROWS * wb + t
            for ap in range(Q_ROWS // 2):
                halves = []
                for a in (2 * ap, 2 * ap + 1):
                    r = Q_ROWS * m + a
                    row_start = min(max(r - kr // 2, 0), rows - kr)
                    ok = row_start <= key_row < row_start + kr
                    halves.append(toeplitz[key_row - r + NA_ROWS - 1] if ok else neg)
                tile = jnp.where(ln < GRID_W, halves[0], halves[1])
                o_ref[var, 0, t * GRID_W:(t + 1) * GRID_W, ap * LANES:(ap + 1) * LANES] = tile


def _bias_table(rpb, rows):
    return pl.pallas_call(
        functools.partial(_bias_kernel, rows=rows),
        out_shape=jax.ShapeDtypeStruct((3, N_HEADS_A, WIN_BLOCKS * Q_TOK, Q_TOK), F32),
        grid=(N_HEADS_A,),
        in_specs=[pl.BlockSpec(memory_space=pltpu.SMEM)],
        out_specs=pl.BlockSpec((3, 1, WIN_BLOCKS * Q_TOK, Q_TOK), lambda h: (0, h, 0, 0)),
        compiler_params=pltpu.CompilerParams(
            dimension_semantics=("arbitrary",), vmem_limit_bytes=VMEM_LIMIT),
        name="nbr_bias",
    )(rpb.reshape(-1))


def _mixout_kernel(x_ref, a_ref, u_ref, vg_ref, mod_ref, gn_ref, ws_ref, bs_ref,
                   na_ref, nb_ref, wo_ref, o_ref, lhs_ref, ssq_ref):
    tm = x_ref.shape[0]
    gate = mod_ref[0, 5:6, :]
    ya = _rms(a_ref[...].astype(F32), na_ref[...]).astype(BF16)
    n_parts = N_GROUPS_B // MIXOUT_GROUPS_PER_DOT
    a_cols = D_MODEL // n_parts

    def attn_part(n):
        cols = slice(n * a_cols, (n + 1) * a_cols)
        o_ref[:, cols] = x_ref[:, cols] + gate[:, cols] * _dot(ya, wo_ref[:WIDTH_A, cols])

    def gated_tile(g, c):
        rows = slice(c * CHUNK, (c + 1) * CHUNK)
        cols = slice(g * GROUP_DIM_B, (g + 1) * GROUP_DIM_B)
        vn = _rms(_gelu_tanh(vg_ref[rows, cols].astype(F32)), gn_ref[:, cols]).astype(BF16)
        gm = _gelu_tanh(u_ref[rows, cols].astype(F32)) * (_dot(ws_ref[g], vn) + bs_ref[:, g:g + 1])
        lhs_ref[rows, cols] = (gm * nb_ref[:, cols]).astype(BF16)
        sq = jnp.broadcast_to(jnp.sum(gm * gm, axis=-1, keepdims=True), (CHUNK, LANES))
        ssq_ref[rows, :] = sq if g == 0 else ssq_ref[rows, :] + sq

    yb = None
    for n in range(n_parts):
        attn_part(n)
        for g in range(n * MIXOUT_GROUPS_PER_DOT, (n + 1) * MIXOUT_GROUPS_PER_DOT):
            for c in range(tm // CHUNK):
                gated_tile(g, c)
        ks = slice(n * MIXOUT_GROUPS_PER_DOT * GROUP_DIM_B, (n + 1) * MIXOUT_GROUPS_PER_DOT * GROUP_DIM_B)
        d = _dot(lhs_ref[:, ks], wo_ref[WIDTH_A + ks.start:WIDTH_A + ks.stop, :])
        yb = d if yb is None else yb + d
    row_scale = lax.rsqrt(ssq_ref[...] * (1.0 / WIDTH_B) + EPS)
    o_ref[...] += gate * (_lane_tile(row_scale, D_MODEL) * yb)


def _mixout(x, a, u, vg, mod, gmlp_norm, w_s, b_s_t, na_g, nb_g, w_out, *, rows_per_mod, tm=512):
    t = x.shape[0]
    tok = lambda w: pl.BlockSpec((tm, w), lambda i: (i, 0))
    full = lambda shape: pl.BlockSpec(shape, lambda i: (0,) * len(shape))
    return pl.pallas_call(
        _mixout_kernel,
        out_shape=jax.ShapeDtypeStruct((t, D_MODEL), F32),
        grid=(t // tm,),
        in_specs=[
            tok(D_MODEL), tok(WIDTH_A), tok(WIDTH_B), tok(WIDTH_B),
            pl.BlockSpec((1, N_MOD, D_MODEL), lambda i: (i * tm // rows_per_mod, 0, 0)),
            full((1, WIDTH_B)),
            full((N_GROUPS_B, CHUNK, CHUNK)),
            full((CHUNK, N_GROUPS_B)),
            full((1, WIDTH_A)),
            full((1, WIDTH_B)),
            pl.BlockSpec((MIX_WIDTH, D_MODEL), lambda i: (0, 0), pipeline_mode=pl.Buffered(1)),
        ],
        out_specs=tok(D_MODEL),
        scratch_shapes=[pltpu.VMEM((tm, WIDTH_B), BF16), pltpu.VMEM((tm, LANES), F32)],
        compiler_params=pltpu.CompilerParams(
            dimension_semantics=("parallel",), vmem_limit_bytes=VMEM_LIMIT),
        name="mixout",
    )(x, a, u, vg, mod, gmlp_norm, w_s, b_s_t, na_g, nb_g, w_out)


def kernel(x_prompt, x_sample, cache_k, cache_v, c, c_ctx, w_ada, b_ada, ffn1_norm, ffn1_w_gate, ffn1_w_up, ffn1_w_down, mix_norm, w_in, rpb, gmlp_norm, w_s, b_s, out_norm_a, out_norm_b, w_out, ffn2_norm, ffn2_w_gate, ffn2_w_up, ffn2_w_down, final_norm):
    batch, seq, _ = x_prompt.shape
    dec_batch, dec_seq, _ = x_sample.shape
    depth = w_ada.shape[0]
    x_ctx = x_prompt.reshape(batch * seq, D_MODEL)
    x_lat = x_sample.reshape(dec_batch * dec_seq, D_MODEL)
    final_g = final_norm.reshape(1, D_MODEL)

    cvec = jnp.concatenate([c_ctx[None, :], c], axis=0)
    mod_rows = -(-cvec.shape[0] // 8) * 8
    cvec = jnp.pad(cvec, ((0, mod_rows - cvec.shape[0]), (0, 0)))

    new_k, new_v = [], []
    for l in range(depth):
        last = l == depth - 1
        mod = _adaln(cvec, w_ada[l], b_ada[l][None, :]).reshape(mod_rows, N_MOD, D_MODEL)
        mods = (mod[0:1], mod[1:1 + dec_batch])
        rows_per_mod = (batch * seq, dec_seq)

        f1 = (ffn1_norm[l][None, :], ffn1_w_gate[l].astype(BF16), ffn1_w_up[l].astype(BF16),
              ffn1_w_down[l].astype(BF16))
        mix_g = mix_norm[l][None, :]
        w_s_l = w_s[l].astype(BF16)
        b_s_t = b_s[l].T
        gn = gmlp_norm[l][None, :]
        na_g = out_norm_a[l][None, :]
        nb_g = out_norm_b[l][None, :]
        bias = _bias_table(rpb[l], dec_seq // GRID_W)
        ckt = jnp.transpose(cache_k[:, l], (0, 2, 3, 1)).reshape(dec_batch, WIDTH_A, -1)
        cvt = jnp.transpose(cache_v[:, l], (0, 2, 3, 1)).reshape(dec_batch, WIDTH_A, -1)

        n_j = D_FF // FFN_TF
        n_ctx, n_lat = x_ctx.shape[0] // FFN_TM, x_lat.shape[0] // FFN_TM
        x_ctx, w_in_l, w_out_l = _ffn(
            x_ctx, mods[0], *f1, mod_base=0, rows_per_mod=rows_per_mod[0],
            side=(_cast_plan(w_in[l], n_ctx, n_j), _cast_plan(w_out[l], n_ctx, n_j)))
        x_lat, *f2_w = _ffn(
            x_lat, mods[1], *f1, mod_base=0, rows_per_mod=rows_per_mod[1],
            side=(_cast_plan_tiled(ffn2_w_gate[l], n_lat, n_j), _cast_plan_tiled(ffn2_w_up[l], n_lat, n_j),
                  _cast_plan_t(ffn2_w_down[l], n_lat, n_j)))
        f2 = (ffn2_norm[l][None, :], *f2_w)

        xs = []
        for path, (x, m, rpm) in enumerate(zip((x_ctx, x_lat), mods, rows_per_mod)):
            q, k, v, u, vg = _mixin(x, m, mix_g, w_in_l, rows_per_mod=rpm,
                                    seq=seq if path == 0 else dec_seq, k_t=path == 0,
                                    kv_dtype=F32 if path == 0 else BF16)
            if path == 0:
                a = _ctx_attention(q, k, v, seq)
                for store, kt in ((new_k, k), (new_v, v)):
                    kt = kt.reshape(batch, N_HEADS_A, HEAD_DIM_A, seq)
                    store.append(jnp.transpose(kt, (0, 3, 1, 2)))
            else:
                a = _nbr_attention(q, k, v, ckt, cvt, bias, dec_seq)
            x = _mixout(x, a, u, vg, m, gn, w_s_l, b_s_t, na_g, nb_g, w_out_l, rows_per_mod=rpm)
            x, = _ffn(x, m, *f2, mod_base=6, rows_per_mod=rpm,
                      final_g=final_g if last else None)
            xs.append(x)
        x_ctx, x_lat = xs

    if depth == 0:
        raise ValueError("depth must be positive")
    y_prompt = x_ctx.reshape(batch, seq, D_MODEL)
    y_sample = x_lat.reshape(dec_batch, dec_seq, D_MODEL)
    return (y_prompt, y_sample, jnp.stack(new_k, axis=1), jnp.stack(new_v, axis=1))
```

```python
import functools
import math
from typing import NamedTuple

import jax
import jax.numpy as jnp
from jax import lax
from jax.experimental import pallas as pl
from jax.experimental.pallas import tpu as pltpu

D_MODEL = 2048
N_HEADS_A = 16
HEAD_DIM_A = 64
WIDTH_A = N_HEADS_A * HEAD_DIM_A
GRID_W = 64
NA_ROWS = 8
NA_COLS = 16
N_GROUPS_B = 8
GROUP_DIM_B = 128
WIDTH_B = N_GROUPS_B * GROUP_DIM_B
CHUNK = 128
MIX_WIDTH = WIDTH_A + WIDTH_B
IN_WIDTH = 3 * WIDTH_A + 2 * WIDTH_B
D_FF = 5632
N_MOD = 9
EPS = 1e-6
NEG_INF = -1e30

LANES = 128
HEADS_PER_BLOCK = LANES // HEAD_DIM_A
N_HEAD_BLOCKS = N_HEADS_A // HEADS_PER_BLOCK
Q_ROWS = 4
Q_TOK = Q_ROWS * GRID_W
WIN_BLOCKS = 3
VMEM_LIMIT = 62 * 1024 * 1024
FFN_OUT_CHUNK = 512
FFN_TM = 1024
FFN_TF = 512
PAIRS_PER_ITER = 8
PROBS_LAG = 2
NBR_OUT_LAG = 3
CTX_OUT_LAG = 4
MIXOUT_GROUPS_PER_DOT = 4
LOG2E = math.log2(math.e)
Q_SCALE = HEAD_DIM_A ** -0.5 * LOG2E

F32 = jnp.float32
BF16 = jnp.bfloat16


def _dot(a, b):
    return jnp.dot(a, b, preferred_element_type=F32)


def _dot_nt(a, b):
    return lax.dot_general(a, b, (((1,), (1,)), ((), ())), preferred_element_type=F32)


def _rms(x, g):
    ms = jnp.mean(x * x, axis=-1, keepdims=True)
    return x * lax.rsqrt(ms + EPS) * g


def _silu(x):
    return x * jax.nn.sigmoid(x)


def _gelu_tanh(x):
    c = 0.7978845608028654
    return 0.5 * x * (1.0 + jnp.tanh(c * (x + 0.044715 * (x * x * x))))


def _lane_tile(v, width):
    return jnp.concatenate([v] * (width // LANES), axis=1)


def _adaln_kernel(c_ref, w_ref, b_ref, o_ref):
    s = _silu(c_ref[...]).astype(BF16)
    o_ref[...] = _dot(s, w_ref[...].astype(BF16)) + b_ref[...]


def _adaln(cvec, w_ada, b_ada, tn=1024):
    rows = cvec.shape[0]
    n = w_ada.shape[1]
    return pl.pallas_call(
        _adaln_kernel,
        out_shape=jax.ShapeDtypeStruct((rows, n), F32),
        grid=(n // tn,),
        in_specs=[
            pl.BlockSpec((rows, D_MODEL), lambda j: (0, 0)),
            pl.BlockSpec((D_MODEL, tn), lambda j: (0, j)),
            pl.BlockSpec((1, tn), lambda j: (0, j)),
        ],
        out_specs=pl.BlockSpec((rows, tn), lambda j: (0, j)),
        compiler_params=pltpu.CompilerParams(
            dimension_semantics=("arbitrary",), vmem_limit_bytes=VMEM_LIMIT),
        name="adaln",
    )(cvec, w_ada, b_ada)


def _ffn_kernel(*refs, mod_base, final, n_side):
    x_ref, mod_ref, g_ref, wg_ref, wu_ref, wd_ref = refs[:6]
    n_in = 7 if final else 6
    fg_ref = refs[6] if final else None
    side_in = refs[n_in:n_in + n_side]
    o_ref = refs[n_in + n_side]
    side_out = refs[n_in + n_side + 1:n_in + 2 * n_side + 1]
    h_ref, = refs[n_in + 2 * n_side + 1:]
    j = pl.program_id(1)

    def step(acc_ref, h):
        for src_ref, dst_ref in zip(side_in, side_out):
            if len(dst_ref.shape) == 3:
                dst_ref[0] = src_ref[...].astype(BF16)
            else:
                dst_ref[...] = src_ref[...].astype(BF16)
        wg = wg_ref[0] if len(wg_ref.shape) == 3 else wg_ref[...]
        wu = wu_ref[0] if len(wu_ref.shape) == 3 else wu_ref[...]
        a = (_silu(_dot(h, wg)) * _dot(h, wu)).astype(BF16)
        half_gate = 0.5 * mod_ref[0, mod_base + 2:mod_base + 3, :]
        for n in range(D_MODEL // FFN_OUT_CHUNK):
            cols = slice(n * FFN_OUT_CHUNK, (n + 1) * FFN_OUT_CHUNK)
            o_ref[:, cols] = acc_ref[:, cols] + half_gate[:, cols] * _dot(a, wd_ref[:, cols])

    @pl.when(j == 0)
    def _():
        x = x_ref[...]
        shift = mod_ref[0, mod_base:mod_base + 1, :]
        gain = g_ref[...] * (1.0 + mod_ref[0, mod_base + 1:mod_base + 2, :])
        ms = jnp.mean(x * x, axis=-1, keepdims=True)
        h = (x * lax.rsqrt(ms + EPS) * gain + shift).astype(BF16)
        h_ref[...] = h
        step(x_ref, h)

    last = pl.num_programs(1) - 1

    @pl.when((j > 0) & (j < last) if final else j > 0)
    def _():
        step(o_ref, h_ref[...])

    if final:
        @pl.when(j == last)
        def _():
            step(o_ref, h_ref[...])
            o_ref[...] = _rms(o_ref[...], fg_ref[...])


def _cast_plan(w, n_i, n_j):
    rows, cols = w.shape
    col_blocks = max(d for d in range(1, n_j + 1) if cols % d == 0 and (cols // d) % LANES == 0)
    assert rows % n_i == 0 and (rows // n_i) % 16 == 0
    block = (rows // n_i, cols // col_blocks)
    index_map = lambda i, j: (i, jnp.minimum(j, col_blocks - 1))
    return w, block, index_map, w.shape, block, index_map


def _cast_plan_t(w, n_i, n_j):
    rows, cols = w.shape
    assert rows % n_j == 0 and (rows // n_j) % 16 == 0 and cols % n_i == 0 and (cols // n_i) % LANES == 0
    block = (rows // n_j, cols // n_i)
    index_map = lambda i, j: (j, i)
    return w, block, index_map, w.shape, block, index_map


def _cast_plan_tiled(w, n_i, n_j):
    rows, cols = w.shape
    assert rows % n_i == 0 and (rows // n_i) % 16 == 0 and cols % n_j == 0 and (cols // n_j) % LANES == 0
    tile = cols // n_j
    return (w, (rows // n_i, tile), lambda i, j: (i, j),
            (n_j, rows, tile), (1, rows // n_i, tile), lambda i, j: (j, i, 0))


def _ffn_in_weight_spec(w, tf):
    if w.ndim == 3:
        return pl.BlockSpec((1, D_MODEL, tf), lambda i, j: (j, 0, 0))
    return pl.BlockSpec((D_MODEL, tf), lambda i, j: (0, j))


def _ffn(x, mod, norm_g, wg, wu, wd, *, mod_base, rows_per_mod, final_g=None, side=(),
         tm=FFN_TM, tf=FFN_TF):
    t = x.shape[0]
    final = final_g is not None
    in_specs = [
        pl.BlockSpec((tm, D_MODEL), lambda i, j: (i, 0)),
        pl.BlockSpec((1, N_MOD, D_MODEL), lambda i, j: (i * tm // rows_per_mod, 0, 0)),
        pl.BlockSpec((1, D_MODEL), lambda i, j: (0, 0)),
        _ffn_in_weight_spec(wg, tf),
        _ffn_in_weight_spec(wu, tf),
        pl.BlockSpec((tf, D_MODEL), lambda i, j: (j, 0)),
    ]
    args = [x, mod, norm_g, wg, wu, wd]
    if final:
        in_specs.append(pl.BlockSpec((1, D_MODEL), lambda i, j: (0, 0)))
        args.append(final_g)
    out_shape = [jax.ShapeDtypeStruct((t, D_MODEL), F32)]
    out_specs = [pl.BlockSpec((tm, D_MODEL), lambda i, j: (i, 0))]
    for w, in_block, in_map, o_shape, o_block, o_map in side:
        in_specs.append(pl.BlockSpec(in_block, in_map))
        args.append(w)
        out_shape.append(jax.ShapeDtypeStruct(o_shape, BF16))
        out_specs.append(pl.BlockSpec(o_block, o_map))
    return pl.pallas_call(
        functools.partial(_ffn_kernel, mod_base=mod_base, final=final, n_side=len(side)),
        out_shape=out_shape,
        grid=(t // tm, D_FF // tf),
        in_specs=in_specs,
        out_specs=out_specs,
        scratch_shapes=[pltpu.VMEM((tm, D_MODEL), BF16)],
        compiler_params=pltpu.CompilerParams(
            dimension_semantics=("arbitrary", "arbitrary"), vmem_limit_bytes=VMEM_LIMIT),
        name="ffn_final" if final else "ffn",
    )(*args)


def _mixin_kernel(*refs, t_seq, bias_rows):
    w_ref, sem = refs[-2:]
    refs = refs[:-2]
    if bias_rows:
        x_ref, mod_ref, g_ref, w_hbm_ref, rpb_ref, q_ref, k_ref, v_ref, u_ref, vg_ref, bias_ref = refs
    else:
        x_ref, mod_ref, g_ref, w_hbm_ref, q_ref, k_ref, v_ref, u_ref, vg_ref = refs
    outs = (q_ref, k_ref, v_ref, u_ref, vg_ref)

    def chunk_copy(idx):
        cols = slice(idx * WIDTH_A, (idx + 1) * WIDTH_A)
        return pltpu.make_async_copy(w_hbm_ref.at[:, cols], w_ref.at[:, cols], sem.at[idx])

    def body(first):
        if first:
            for idx in range(len(outs)):
                chunk_copy(idx).start()
        if bias_rows:
            _bias_tiles(rpb_ref, bias_ref, pl.program_id(0), bias_rows)
        shift = mod_ref[0, 3:4, :]
        scale = mod_ref[0, 4:5, :]
        h = (_rms(x_ref[...], g_ref[...]) * (1.0 + scale) + shift).astype(BF16)
        for idx, o_ref in enumerate(outs):
            if first:
                chunk_copy(idx).wait()
            p = _dot(h, w_ref[:, idx * WIDTH_A:(idx + 1) * WIDTH_A])
            if idx == 0:
                p = p * Q_SCALE
            seq = t_seq[idx]
            if seq is None:
                o_ref[...] = p.astype(o_ref.dtype)
            else:
                per_block = max(p.shape[0] // seq, 1)
                rows = p.shape[0] // per_block
                for e in range(per_block):
                    o_ref[e] = p[e * rows:(e + 1) * rows, :].T.astype(o_ref.dtype)

    pl.when(pl.program_id(0) == 0)(lambda: body(True))
    pl.when(pl.program_id(0) > 0)(lambda: body(False))


def _mixin(x, mod, norm_g, w_in, *, rows_per_mod, seq, k_t, kv_dtype, rpb=None, tm=512):
    t = x.shape[0]
    def tok(dtype):
        return jax.ShapeDtypeStruct((t, WIDTH_A), dtype), pl.BlockSpec((tm, WIDTH_A), lambda i: (i, 0))
    def feat(dtype):
        shape = jax.ShapeDtypeStruct((t // seq, WIDTH_A, seq), dtype)
        if seq >= tm:
            per_seq = seq // tm
            return shape, pl.BlockSpec((1, WIDTH_A, tm), lambda i: (i // per_seq, 0, i % per_seq))
        return shape, pl.BlockSpec((tm // seq, WIDTH_A, seq), lambda i: (i, 0, 0))
    outs = [tok(BF16), feat(kv_dtype) if k_t else tok(kv_dtype), feat(kv_dtype), tok(BF16), tok(BF16)]
    t_seq = (None, seq if k_t else None, seq, None, None)
    in_specs = [
        pl.BlockSpec((tm, D_MODEL), lambda i: (i, 0)),
        pl.BlockSpec((1, N_MOD, D_MODEL), lambda i: (i * tm // rows_per_mod, 0, 0)),
        pl.BlockSpec((1, D_MODEL), lambda i: (0, 0)),
        pl.BlockSpec(memory_space=pl.ANY),
    ]
    args = [x, mod, norm_g, w_in]
    bias_rows = None
    if rpb is not None:
        assert t // tm == N_HEADS_A
        bias_rows = seq // GRID_W
        in_specs.append(pl.BlockSpec(memory_space=pltpu.SMEM))
        args.append(rpb.reshape(-1))
        outs.append((jax.ShapeDtypeStruct((3, N_HEADS_A, WIN_BLOCKS * Q_TOK, Q_TOK), F32),
                     pl.BlockSpec((3, 1, WIN_BLOCKS * Q_TOK, Q_TOK), lambda i: (0, i, 0, 0))))
    return pl.pallas_call(
        functools.partial(_mixin_kernel, t_seq=t_seq, bias_rows=bias_rows),
        out_shape=[o[0] for o in outs],
        grid=(t // tm,),
        in_specs=in_specs,
        out_specs=[o[1] for o in outs],
        scratch_shapes=[pltpu.VMEM((D_MODEL, IN_WIDTH), BF16), pltpu.SemaphoreType.DMA((5,))],
        compiler_params=pltpu.CompilerParams(
            dimension_semantics=("arbitrary",), vmem_limit_bytes=VMEM_LIMIT),
        name="mixin",
    )(*args)


class _Seg(NamedTuple):
    k: jax.Array
    vt: jax.Array


def _head_scores(q, segs, biases, hh):
    lane = lax.broadcasted_iota(jnp.int32, (1, LANES), 1)
    qh = jnp.where((lane // HEAD_DIM_A) == hh, q, jnp.zeros_like(q))
    scores = []
    for seg, b in zip(segs, biases):
        s = _dot_nt(seg.k, qh)
        scores.append(s if b is None else s + b)
    return scores


def _head_probs(scores):
    m = scores[0].max(axis=0, keepdims=True)
    for s in scores[1:]:
        m = jnp.maximum(m, s.max(axis=0, keepdims=True))
    return [jnp.exp2(s - m).astype(BF16) for s in scores]


def _head_output(probs, segs, hh):
    row = lax.broadcasted_iota(jnp.int32, (LANES, 1), 0)
    ones_idx = ((hh + 1) % HEADS_PER_BLOCK) * HEAD_DIM_A
    head_rows = (row // HEAD_DIM_A) == hh
    ones_col = jnp.where(row == ones_idx, 1.0, 0.0)
    acc = None
    for p, seg in zip(probs, segs):
        vt_aug = jnp.where(head_rows, seg.vt.astype(F32), ones_col).astype(BF16)
        d = _dot(vt_aug, p)
        acc = d if acc is None else acc + d
    return jnp.where(head_rows, acc / acc[ones_idx:ones_idx + 1, :], 0.0)


def _attention_heads(q_ref, o_ref, seg_fn, bias_fn, pairs_per_iter, out_lag):
    def body(it, carry):
        pairs = [it * pairs_per_iter + sub for sub in range(pairs_per_iter)]
        cols = [pl.ds(pl.multiple_of(p * LANES, LANES), LANES) for p in pairs]
        heads = [(i, hh) for i in range(pairs_per_iter) for hh in range(HEADS_PER_BLOCK)]
        operands = {}

        def load(i):
            if i not in operands:
                operands[i] = (q_ref[:, cols[i]], seg_fn(pairs[i], cols[i]))
            return operands[i]

        def scores_of(t):
            i, hh = heads[t]
            q, segs = load(i)
            return _head_scores(q, segs, [bias_fn(pairs[i], hh, k) for k in range(len(segs))], hh)

        scores, probs, out = {}, {}, None
        for t in range(len(heads) + out_lag):
            if t < len(heads):
                scores[t] = scores_of(t)
            if PROBS_LAG <= t < len(heads) + PROBS_LAG:
                probs[t - PROBS_LAG] = _head_probs(scores.pop(t - PROBS_LAG))
            if t >= out_lag:
                i, hh = heads[t - out_lag]
                o = _head_output(probs.pop(t - out_lag), load(i)[1], hh)
                out = o if hh == 0 else out + o
                if hh == HEADS_PER_BLOCK - 1:
                    o_ref[:, cols[i]] = out.T.astype(o_ref.dtype)
        return carry

    lax.fori_loop(0, N_HEAD_BLOCKS // pairs_per_iter, body, 0)


def _keys_from_t(kt):
    return kt.T.astype(BF16)


def _ctx_attn_kernel(q_ref, kt_ref, vt_ref, o_ref):
    def seg_fn(p, cols):
        return [_Seg(_keys_from_t(kt_ref[0, cols, :]), vt_ref[0, cols, :])]

    _attention_heads(q_ref, o_ref, seg_fn, lambda p, hh, i: None, N_HEAD_BLOCKS, CTX_OUT_LAG)


def _ctx_attention(q, kt, vt, seq):
    t = q.shape[0]
    qspec = pl.BlockSpec((seq, WIDTH_A), lambda b: (b, 0))
    tspec = pl.BlockSpec((1, WIDTH_A, seq), lambda b: (b, 0, 0))
    return pl.pallas_call(
        _ctx_attn_kernel,
        out_shape=jax.ShapeDtypeStruct((t, WIDTH_A), BF16),
        grid=(t // seq,),
        in_specs=[qspec, tspec, tspec],
        out_specs=qspec,
        compiler_params=pltpu.CompilerParams(
            dimension_semantics=("parallel",), vmem_limit_bytes=VMEM_LIMIT),
        name="ctx_attn",
    )(q, kt, vt)


def _nbr_attn_kernel(q_ref, k0_ref, k1_ref, k2_ref, vt0_ref, vt1_ref, vt2_ref,
                     ckt_ref, cvt_ref, bias_ref, o_ref):
    local = ((k0_ref, vt0_ref), (k1_ref, vt1_ref), (k2_ref, vt2_ref))

    def run(windows):
        def seg_fn(p, cols):
            segs = [_Seg(local[w][0][:, cols], local[w][1][0, cols, :]) for w in windows]
            segs.append(_Seg(_keys_from_t(ckt_ref[0, cols, :]), cvt_ref[0, cols, :]))
            return segs

        def bias_fn(p, hh, i):
            if i >= len(windows):
                return None
            w = windows[i]
            return bias_ref[0, HEADS_PER_BLOCK * p + hh, w * Q_TOK:(w + 1) * Q_TOK, :]

        _attention_heads(q_ref, o_ref, seg_fn, bias_fn, PAIRS_PER_ITER, NBR_OUT_LAG)

    m = pl.program_id(0)
    last = pl.num_programs(0) - 1
    pl.when(m == 0)(lambda: run(tuple(range(WIN_BLOCKS - 1))))
    pl.when(m == last)(lambda: run(tuple(range(1, WIN_BLOCKS))))
    pl.when((m > 0) & (m < last))(lambda: run(tuple(range(WIN_BLOCKS))))


def _nbr_attention(q, k, vt, ckt, cvt, bias, n_tok):
    t = q.shape[0]
    batch = t // n_tok
    past = ckt.shape[2]
    qb = n_tok // Q_TOK
    assert qb > WIN_BLOCKS and NA_ROWS <= (WIN_BLOCKS - 1) * Q_ROWS
    def win(m):
        return jnp.clip(m - 1, 0, qb - WIN_BLOCKS)
    def variant(m):
        return jnp.minimum(m, 1) + jnp.maximum(m - (qb - 2), 0)
    qspec = pl.BlockSpec((Q_TOK, WIDTH_A), lambda m, b: (b * qb + m, 0))
    def kspec(w):
        return pl.BlockSpec((Q_TOK, WIDTH_A), lambda m, b: (b * qb + win(m) + w, 0))
    def vtspec(w):
        return pl.BlockSpec((1, WIDTH_A, Q_TOK), lambda m, b: (b, 0, win(m) + w))
    cspec = pl.BlockSpec((1, WIDTH_A, past), lambda m, b: (b, 0, 0))
    bspec = pl.BlockSpec((1, N_HEADS_A, WIN_BLOCKS * Q_TOK, Q_TOK),
                         lambda m, b: (variant(m), 0, 0, 0))
    return pl.pallas_call(
        _nbr_attn_kernel,
        out_shape=jax.ShapeDtypeStruct((t, WIDTH_A), BF16),
        grid=(qb, batch),
        in_specs=[qspec] + [kspec(w) for w in range(WIN_BLOCKS)]
                 + [vtspec(w) for w in range(WIN_BLOCKS)] + [cspec, cspec, bspec],
        out_specs=qspec,
        compiler_params=pltpu.CompilerParams(
            dimension_semantics=("parallel", "arbitrary"), vmem_limit_bytes=VMEM_LIMIT),
        name="nbr_attn",
    )(q, k, k, k, vt, vt, vt, ckt, cvt, bias)


def _bias_tiles(rpb_ref, o_ref, h, rows):
    n_dr = 2 * NA_ROWS - 1
    n_dc = 2 * NA_COLS - 1
    kc = lax.broadcasted_iota(jnp.int32, (GRID_W, LANES), 0)
    ln = lax.broadcasted_iota(jnp.int32, (GRID_W, LANES), 1)
    qc = ln % GRID_W
    col_start = jnp.clip(qc - NA_COLS // 2, 0, GRID_W - NA_COLS)
    col_ok = (kc >= col_start) & (kc < col_start + NA_COLS)
    neg = jnp.full((GRID_W, LANES), NEG_INF, F32)
    lane8 = lax.broadcasted_iota(jnp.int32, (8, LANES), 1)
    toeplitz = []
    for dr in range(n_dr):
        r8 = jnp.zeros((8, LANES), F32)
        for dc in range(n_dc):
            r8 = jnp.where(lane8 == n_dc - 1 - dc, rpb_ref[h * (n_dr * n_dc) + dr * n_dc + dc], r8)
        r = jnp.concatenate([r8] * (GRID_W // 8), axis=0)
        lo = pltpu.roll(r, LANES - (NA_COLS - 1), 1, stride=1, stride_axis=0)
        hi = pltpu.roll(r, GRID_W - (NA_COLS - 1), 1, stride=1, stride_axis=0)
        t = jnp.where(ln < GRID_W, lo, hi)
        toeplitz.append(jnp.where(col_ok, t * LOG2E, neg))
    qb = rows // Q_ROWS
    kr = min(NA_ROWS, rows)
    for var, m in enumerate((0, 1, qb - 1)):
        wb = min(max(m - 1, 0), qb - WIN_BLOCKS)
        for t in range(WIN_BLOCKS * Q_ROWS):
            key_row = Q_ROWS * wb + t
            for ap in range(Q_ROWS // 2):
                halves = []
                for a in (2 * ap, 2 * ap + 1):
                    r = Q_ROWS * m + a
                    row_start = min(max(r - kr // 2, 0), rows - kr)
                    ok = row_start <= key_row < row_start + kr
                    halves.append(toeplitz[key_row - r + NA_ROWS - 1] if ok else neg)
                tile = jnp.where(ln < GRID_W, halves[0], halves[1])
                o_ref[var, 0, t * GRID_W:(t + 1) * GRID_W, ap * LANES:(ap + 1) * LANES] = tile


def _mixout_kernel(x_ref, a_ref, u_ref, vg_ref, mod_ref, gn_ref, ws_ref, bs_ref,
                   na_ref, nb_ref, wo_ref, o_ref, lhs_ref, ssq_ref):
    tm = x_ref.shape[0]
    gate = mod_ref[0, 5:6, :]
    ya = _rms(a_ref[...].astype(F32), na_ref[...]).astype(BF16)
    n_parts = N_GROUPS_B // MIXOUT_GROUPS_PER_DOT
    a_cols = D_MODEL // n_parts

    def attn_part(n):
        cols = slice(n * a_cols, (n + 1) * a_cols)
        o_ref[:, cols] = x_ref[:, cols] + gate[:, cols] * _dot(ya, wo_ref[:WIDTH_A, cols])

    def gated_tile(g, c):
        rows = slice(c * CHUNK, (c + 1) * CHUNK)
        cols = slice(g * GROUP_DIM_B, (g + 1) * GROUP_DIM_B)
        vn = _rms(_gelu_tanh(vg_ref[rows, cols].astype(F32)), gn_ref[:, cols]).astype(BF16)
        gm = _gelu_tanh(u_ref[rows, cols].astype(F32)) * (_dot(ws_ref[g], vn) + bs_ref[:, g:g + 1])
        lhs_ref[rows, cols] = (gm * nb_ref[:, cols]).astype(BF16)
        sq = jnp.broadcast_to(jnp.sum(gm * gm, axis=-1, keepdims=True), (CHUNK, LANES))
        ssq_ref[rows, :] = sq if g == 0 else ssq_ref[rows, :] + sq

    yb = None
    for n in range(n_parts):
        attn_part(n)
        for g in range(n * MIXOUT_GROUPS_PER_DOT, (n + 1) * MIXOUT_GROUPS_PER_DOT):
            for c in range(tm // CHUNK):
                gated_tile(g, c)
        ks = slice(n * MIXOUT_GROUPS_PER_DOT * GROUP_DIM_B, (n + 1) * MIXOUT_GROUPS_PER_DOT * GROUP_DIM_B)
        d = _dot(lhs_ref[:, ks], wo_ref[WIDTH_A + ks.start:WIDTH_A + ks.stop, :])
        yb = d if yb is None else yb + d
    row_scale = lax.rsqrt(ssq_ref[...] * (1.0 / WIDTH_B) + EPS)
    o_ref[...] += gate * (_lane_tile(row_scale, D_MODEL) * yb)


def _mixout(x, a, u, vg, mod, gmlp_norm, w_s, b_s_t, na_g, nb_g, w_out, *, rows_per_mod, tm=512):
    t = x.shape[0]
    tok = lambda w: pl.BlockSpec((tm, w), lambda i: (i, 0))
    full = lambda shape: pl.BlockSpec(shape, lambda i: (0,) * len(shape))
    return pl.pallas_call(
        _mixout_kernel,
        out_shape=jax.ShapeDtypeStruct((t, D_MODEL), F32),
        grid=(t // tm,),
        in_specs=[
            tok(D_MODEL), tok(WIDTH_A), tok(WIDTH_B), tok(WIDTH_B),
            pl.BlockSpec((1, N_MOD, D_MODEL), lambda i: (i * tm // rows_per_mod, 0, 0)),
            full((1, WIDTH_B)),
            full((N_GROUPS_B, CHUNK, CHUNK)),
            full((CHUNK, N_GROUPS_B)),
            full((1, WIDTH_A)),
            full((1, WIDTH_B)),
            pl.BlockSpec((MIX_WIDTH, D_MODEL), lambda i: (0, 0), pipeline_mode=pl.Buffered(1)),
        ],
        out_specs=tok(D_MODEL),
        scratch_shapes=[pltpu.VMEM((tm, WIDTH_B), BF16), pltpu.VMEM((tm, LANES), F32)],
        compiler_params=pltpu.CompilerParams(
            dimension_semantics=("parallel",), vmem_limit_bytes=VMEM_LIMIT),
        name="mixout",
    )(x, a, u, vg, mod, gmlp_norm, w_s, b_s_t, na_g, nb_g, w_out)


def kernel(x_prompt, x_sample, cache_k, cache_v, c, c_ctx, w_ada, b_ada, ffn1_norm, ffn1_w_gate, ffn1_w_up, ffn1_w_down, mix_norm, w_in, rpb, gmlp_norm, w_s, b_s, out_norm_a, out_norm_b, w_out, ffn2_norm, ffn2_w_gate, ffn2_w_up, ffn2_w_down, final_norm):
    batch, seq, _ = x_prompt.shape
    dec_batch, dec_seq, _ = x_sample.shape
    depth = w_ada.shape[0]
    x_ctx = x_prompt.reshape(batch * seq, D_MODEL)
    x_lat = x_sample.reshape(dec_batch * dec_seq, D_MODEL)
    final_g = final_norm.reshape(1, D_MODEL)

    cvec = jnp.concatenate([c_ctx[None, :], c], axis=0)
    mod_rows = -(-cvec.shape[0] // 8) * 8
    cvec = jnp.pad(cvec, ((0, mod_rows - cvec.shape[0]), (0, 0)))

    new_k, new_v = [], []
    for l in range(depth):
        last = l == depth - 1
        mod = _adaln(cvec, w_ada[l], b_ada[l][None, :]).reshape(mod_rows, N_MOD, D_MODEL)
        mods = (mod[0:1], mod[1:1 + dec_batch])
        rows_per_mod = (batch * seq, dec_seq)

        f1 = (ffn1_norm[l][None, :], ffn1_w_gate[l].astype(BF16), ffn1_w_up[l].astype(BF16),
              ffn1_w_down[l].astype(BF16))
        mix_g = mix_norm[l][None, :]
        w_s_l = w_s[l].astype(BF16)
        b_s_t = b_s[l].T
        gn = gmlp_norm[l][None, :]
        na_g = out_norm_a[l][None, :]
        nb_g = out_norm_b[l][None, :]
        ckt = jnp.transpose(cache_k[:, l], (0, 2, 3, 1)).reshape(dec_batch, WIDTH_A, -1)
        cvt = jnp.transpose(cache_v[:, l], (0, 2, 3, 1)).reshape(dec_batch, WIDTH_A, -1)

        n_j = D_FF // FFN_TF
        n_ctx, n_lat = x_ctx.shape[0] // FFN_TM, x_lat.shape[0] // FFN_TM
        x_ctx, w_in_l, w_out_l = _ffn(
            x_ctx, mods[0], *f1, mod_base=0, rows_per_mod=rows_per_mod[0],
            side=(_cast_plan(w_in[l], n_ctx, n_j), _cast_plan(w_out[l], n_ctx, n_j)))
        x_lat, *f2_w = _ffn(
            x_lat, mods[1], *f1, mod_base=0, rows_per_mod=rows_per_mod[1],
            side=(_cast_plan_tiled(ffn2_w_gate[l], n_lat, n_j), _cast_plan_tiled(ffn2_w_up[l], n_lat, n_j),
                  _cast_plan_t(ffn2_w_down[l], n_lat, n_j)))
        f2 = (ffn2_norm[l][None, :], *f2_w)

        xs = []
        for path, (x, m, rpm) in enumerate(zip((x_ctx, x_lat), mods, rows_per_mod)):
            q, k, v, u, vg, *bias = _mixin(x, m, mix_g, w_in_l, rows_per_mod=rpm,
                                           seq=seq if path == 0 else dec_seq, k_t=path == 0,
                                           kv_dtype=F32 if path == 0 else BF16,
                                           rpb=None if path == 0 else rpb[l])
            if path == 0:
                a = _ctx_attention(q, k, v, seq)
                for store, kt in ((new_k, k), (new_v, v)):
                    kt = kt.reshape(batch, N_HEADS_A, HEAD_DIM_A, seq)
                    store.append(jnp.transpose(kt, (0, 3, 1, 2)))
            else:
                a = _nbr_attention(q, k, v, ckt, cvt, bias[0], dec_seq)
            x = _mixout(x, a, u, vg, m, gn, w_s_l, b_s_t, na_g, nb_g, w_out_l, rows_per_mod=rpm)
            x, = _ffn(x, m, *f2, mod_base=6, rows_per_mod=rpm,
                      final_g=final_g if last else None)
            xs.append(x)
        x_ctx, x_lat = xs

    if depth == 0:
        raise ValueError("depth must be positive")
    y_prompt = x_ctx.reshape(batch, seq, D_MODEL)
    y_sample = x_lat.reshape(dec_batch, dec_seq, D_MODEL)
    return (y_prompt, y_sample, jnp.stack(new_k, axis=1), jnp.stack(new_v, axis=1))
```

```python
import functools
import math
from typing import NamedTuple

import jax
import jax.numpy as jnp
from jax import lax
from jax.experimental import pallas as pl
from jax.experimental.pallas import tpu as pltpu

D_MODEL = 2048
N_HEADS_A = 16
HEAD_DIM_A = 64
WIDTH_A = N_HEADS_A * HEAD_DIM_A
GRID_W = 64
NA_ROWS = 8
NA_COLS = 16
N_GROUPS_B = 8
GROUP_DIM_B = 128
WIDTH_B = N_GROUPS_B * GROUP_DIM_B
CHUNK = 128
MIX_WIDTH = WIDTH_A + WIDTH_B
IN_WIDTH = 3 * WIDTH_A + 2 * WIDTH_B
D_FF = 5632
N_MOD = 9
EPS = 1e-6
NEG_INF = -1e30

LANES = 128
HEADS_PER_BLOCK = LANES // HEAD_DIM_A
N_HEAD_BLOCKS = N_HEADS_A // HEADS_PER_BLOCK
Q_ROWS = 4
Q_TOK = Q_ROWS * GRID_W
WIN_BLOCKS = 3
VMEM_LIMIT = 62 * 1024 * 1024
FFN_OUT_CHUNK = 512
FFN_TM = 1024
FFN_TF = 512
PAIRS_PER_ITER = 8
PROBS_LAG = 2
NBR_OUT_LAG = 3
CTX_OUT_LAG = 4
MIXOUT_GROUPS_PER_DOT = 4
LOG2E = math.log2(math.e)
Q_SCALE = HEAD_DIM_A ** -0.5 * LOG2E

F32 = jnp.float32
BF16 = jnp.bfloat16


def _dot(a, b):
    return jnp.dot(a, b, preferred_element_type=F32)


def _dot_nt(a, b):
    return lax.dot_general(a, b, (((1,), (1,)), ((), ())), preferred_element_type=F32)


def _rms(x, g):
    ms = jnp.mean(x * x, axis=-1, keepdims=True)
    return x * lax.rsqrt(ms + EPS) * g


def _silu(x):
    return x * jax.nn.sigmoid(x)


def _gelu_tanh(x):
    c = 0.7978845608028654
    return 0.5 * x * (1.0 + jnp.tanh(c * (x + 0.044715 * (x * x * x))))


def _lane_tile(v, width):
    return jnp.concatenate([v] * (width // LANES), axis=1)


def _adaln_kernel(c_ref, w_ref, b_ref, o_ref):
    s = _silu(c_ref[...]).astype(BF16)
    o_ref[...] = _dot(s, w_ref[...].astype(BF16)) + b_ref[...]


def _adaln(cvec, w_ada, b_ada, tn=1024):
    rows = cvec.shape[0]
    n = w_ada.shape[1]
    return pl.pallas_call(
        _adaln_kernel,
        out_shape=jax.ShapeDtypeStruct((rows, n), F32),
        grid=(n // tn,),
        in_specs=[
            pl.BlockSpec((rows, D_MODEL), lambda j: (0, 0)),
            pl.BlockSpec((D_MODEL, tn), lambda j: (0, j)),
            pl.BlockSpec((1, tn), lambda j: (0, j)),
        ],
        out_specs=pl.BlockSpec((rows, tn), lambda j: (0, j)),
        compiler_params=pltpu.CompilerParams(
            dimension_semantics=("arbitrary",), vmem_limit_bytes=VMEM_LIMIT),
        name="adaln",
    )(cvec, w_ada, b_ada)


def _ffn_kernel(*refs, mod_base, final, n_side):
    x_ref, mod_ref, g_ref, wg_ref, wu_ref, wd_ref = refs[:6]
    n_in = 7 if final else 6
    fg_ref = refs[6] if final else None
    side_in = refs[n_in:n_in + n_side]
    o_ref = refs[n_in + n_side]
    side_out = refs[n_in + n_side + 1:n_in + 2 * n_side + 1]
    h_ref, = refs[n_in + 2 * n_side + 1:]
    j = pl.program_id(1)

    def step(acc_ref, h):
        for src_ref, dst_ref in zip(side_in, side_out):
            if len(dst_ref.shape) == 3:
                dst_ref[0] = src_ref[...].astype(BF16)
            else:
                dst_ref[...] = src_ref[...].astype(BF16)
        wg = wg_ref[0] if len(wg_ref.shape) == 3 else wg_ref[...]
        wu = wu_ref[0] if len(wu_ref.shape) == 3 else wu_ref[...]
        a = (_silu(_dot(h, wg)) * _dot(h, wu)).astype(BF16)
        half_gate = 0.5 * mod_ref[0, mod_base + 2:mod_base + 3, :]
        for n in range(D_MODEL // FFN_OUT_CHUNK):
            cols = slice(n * FFN_OUT_CHUNK, (n + 1) * FFN_OUT_CHUNK)
            o_ref[:, cols] = acc_ref[:, cols] + half_gate[:, cols] * _dot(a, wd_ref[:, cols])

    @pl.when(j == 0)
    def _():
        x = x_ref[...]
        shift = mod_ref[0, mod_base:mod_base + 1, :]
        gain = g_ref[...] * (1.0 + mod_ref[0, mod_base + 1:mod_base + 2, :])
        ms = jnp.mean(x * x, axis=-1, keepdims=True)
        h = (x * lax.rsqrt(ms + EPS) * gain + shift).astype(BF16)
        h_ref[...] = h
        step(x_ref, h)

    last = pl.num_programs(1) - 1

    @pl.when((j > 0) & (j < last) if final else j > 0)
    def _():
        step(o_ref, h_ref[...])

    if final:
        @pl.when(j == last)
        def _():
            step(o_ref, h_ref[...])
            o_ref[...] = _rms(o_ref[...], fg_ref[...])


def _cast_plan(w, n_i, n_j):
    rows, cols = w.shape
    col_blocks = max(d for d in range(1, n_j + 1) if cols % d == 0 and (cols // d) % LANES == 0)
    assert rows % n_i == 0 and (rows // n_i) % 16 == 0
    block = (rows // n_i, cols // col_blocks)
    index_map = lambda i, j: (i, jnp.minimum(j, col_blocks - 1))
    return w, block, index_map, w.shape, block, index_map


def _cast_plan_t(w, n_i, n_j):
    rows, cols = w.shape
    assert rows % n_j == 0 and (rows // n_j) % 16 == 0 and cols % n_i == 0 and (cols // n_i) % LANES == 0
    block = (rows // n_j, cols // n_i)
    index_map = lambda i, j: (j, i)
    return w, block, index_map, w.shape, block, index_map


def _cast_plan_tiled(w, n_i, n_j):
    rows, cols = w.shape
    assert rows % n_i == 0 and (rows // n_i) % 16 == 0 and cols % n_j == 0 and (cols // n_j) % LANES == 0
    tile = cols // n_j
    return (w, (rows // n_i, tile), lambda i, j: (i, j),
            (n_j, rows, tile), (1, rows // n_i, tile), lambda i, j: (j, i, 0))


def _ffn_in_weight_spec(w, tf):
    if w.ndim == 3:
        return pl.BlockSpec((1, D_MODEL, tf), lambda i, j: (j, 0, 0))
    return pl.BlockSpec((D_MODEL, tf), lambda i, j: (0, j))


def _ffn(x, mod, norm_g, wg, wu, wd, *, mod_base, rows_per_mod, final_g=None, side=(),
         tm=FFN_TM, tf=FFN_TF):
    t = x.shape[0]
    final = final_g is not None
    in_specs = [
        pl.BlockSpec((tm, D_MODEL), lambda i, j: (i, 0)),
        pl.BlockSpec((1, N_MOD, D_MODEL), lambda i, j: (i * tm // rows_per_mod, 0, 0)),
        pl.BlockSpec((1, D_MODEL), lambda i, j: (0, 0)),
        _ffn_in_weight_spec(wg, tf),
        _ffn_in_weight_spec(wu, tf),
        pl.BlockSpec((tf, D_MODEL), lambda i, j: (j, 0)),
    ]
    args = [x, mod, norm_g, wg, wu, wd]
    if final:
        in_specs.append(pl.BlockSpec((1, D_MODEL), lambda i, j: (0, 0)))
        args.append(final_g)
    out_shape = [jax.ShapeDtypeStruct((t, D_MODEL), F32)]
    out_specs = [pl.BlockSpec((tm, D_MODEL), lambda i, j: (i, 0))]
    for w, in_block, in_map, o_shape, o_block, o_map in side:
        in_specs.append(pl.BlockSpec(in_block, in_map))
        args.append(w)
        out_shape.append(jax.ShapeDtypeStruct(o_shape, BF16))
        out_specs.append(pl.BlockSpec(o_block, o_map))
    return pl.pallas_call(
        functools.partial(_ffn_kernel, mod_base=mod_base, final=final, n_side=len(side)),
        out_shape=out_shape,
        grid=(t // tm, D_FF // tf),
        in_specs=in_specs,
        out_specs=out_specs,
        scratch_shapes=[pltpu.VMEM((tm, D_MODEL), BF16)],
        compiler_params=pltpu.CompilerParams(
            dimension_semantics=("arbitrary", "arbitrary"), vmem_limit_bytes=VMEM_LIMIT),
        name="ffn_final" if final else "ffn",
    )(*args)


def _mixin_kernel(*refs, t_seq, bias_rows):
    w_ref, sem = refs[-2:]
    refs = refs[:-2]
    if bias_rows:
        x_ref, mod_ref, g_ref, w_hbm_ref, rpb_ref, q_ref, k_ref, v_ref, u_ref, vg_ref, bias_ref = refs
    else:
        x_ref, mod_ref, g_ref, w_hbm_ref, q_ref, k_ref, v_ref, u_ref, vg_ref = refs
    outs = (q_ref, k_ref, v_ref, u_ref, vg_ref)

    def chunk_copy(idx):
        cols = slice(idx * WIDTH_A, (idx + 1) * WIDTH_A)
        return pltpu.make_async_copy(w_hbm_ref.at[:, cols], w_ref.at[:, cols], sem.at[idx])

    def body(first):
        if first:
            for idx in range(len(outs)):
                chunk_copy(idx).start()
        if bias_rows:
            _bias_tiles(rpb_ref, bias_ref, pl.program_id(0), bias_rows)
        shift = mod_ref[0, 3:4, :]
        scale = mod_ref[0, 4:5, :]
        h = (_rms(x_ref[...], g_ref[...]) * (1.0 + scale) + shift).astype(BF16)
        for idx, o_ref in enumerate(outs):
            if first:
                chunk_copy(idx).wait()
            p = _dot(h, w_ref[:, idx * WIDTH_A:(idx + 1) * WIDTH_A])
            if idx == 0:
                p = p * Q_SCALE
            seq = t_seq[idx]
            if seq is None:
                o_ref[...] = p.astype(o_ref.dtype)
            else:
                per_block = max(p.shape[0] // seq, 1)
                rows = p.shape[0] // per_block
                for e in range(per_block):
                    o_ref[e] = p[e * rows:(e + 1) * rows, :].T.astype(o_ref.dtype)

    pl.when(pl.program_id(0) == 0)(lambda: body(True))
    pl.when(pl.program_id(0) > 0)(lambda: body(False))


def _mixin(x, mod, norm_g, w_in, *, rows_per_mod, seq, k_t, kv_dtype, rpb=None, tm=512):
    t = x.shape[0]
    def tok(dtype):
        return jax.ShapeDtypeStruct((t, WIDTH_A), dtype), pl.BlockSpec((tm, WIDTH_A), lambda i: (i, 0))
    def feat(dtype):
        shape = jax.ShapeDtypeStruct((t // seq, WIDTH_A, seq), dtype)
        if seq >= tm:
            per_seq = seq // tm
            return shape, pl.BlockSpec((1, WIDTH_A, tm), lambda i: (i // per_seq, 0, i % per_seq))
        return shape, pl.BlockSpec((tm // seq, WIDTH_A, seq), lambda i: (i, 0, 0))
    outs = [tok(BF16), feat(kv_dtype) if k_t else tok(kv_dtype), feat(kv_dtype), tok(BF16), tok(BF16)]
    t_seq = (None, seq if k_t else None, seq, None, None)
    in_specs = [
        pl.BlockSpec((tm, D_MODEL), lambda i: (i, 0)),
        pl.BlockSpec((1, N_MOD, D_MODEL), lambda i: (i * tm // rows_per_mod, 0, 0)),
        pl.BlockSpec((1, D_MODEL), lambda i: (0, 0)),
        pl.BlockSpec(memory_space=pl.ANY),
    ]
    args = [x, mod, norm_g, w_in]
    bias_rows = None
    if rpb is not None:
        assert t // tm == N_HEADS_A
        bias_rows = seq // GRID_W
        in_specs.append(pl.BlockSpec(memory_space=pltpu.SMEM))
        args.append(rpb.reshape(-1))
        outs.append((jax.ShapeDtypeStruct((3, N_HEADS_A, WIN_BLOCKS * Q_TOK, Q_TOK), F32),
                     pl.BlockSpec((3, 1, WIN_BLOCKS * Q_TOK, Q_TOK), lambda i: (0, i, 0, 0))))
    return pl.pallas_call(
        functools.partial(_mixin_kernel, t_seq=t_seq, bias_rows=bias_rows),
        out_shape=[o[0] for o in outs],
        grid=(t // tm,),
        in_specs=in_specs,
        out_specs=[o[1] for o in outs],
        scratch_shapes=[pltpu.VMEM((D_MODEL, IN_WIDTH), BF16), pltpu.SemaphoreType.DMA((5,))],
        compiler_params=pltpu.CompilerParams(
            dimension_semantics=("arbitrary",), vmem_limit_bytes=VMEM_LIMIT),
        name="mixin",
    )(*args)


class _Seg(NamedTuple):
    k: jax.Array
    vt: jax.Array


def _head_scores(q, segs, biases, hh):
    lane = lax.broadcasted_iota(jnp.int32, (1, LANES), 1)
    qh = jnp.where((lane // HEAD_DIM_A) == hh, q, jnp.zeros_like(q))
    scores = []
    for seg, b in zip(segs, biases):
        s = _dot_nt(seg.k, qh)
        scores.append(s if b is None else s + b)
    return scores


def _head_probs(scores):
    m = scores[0].max(axis=0, keepdims=True)
    for s in scores[1:]:
        m = jnp.maximum(m, s.max(axis=0, keepdims=True))
    return [jnp.exp2(s - m).astype(BF16) for s in scores]


def _head_output(probs, segs, hh):
    row = lax.broadcasted_iota(jnp.int32, (LANES, 1), 0)
    ones_idx = ((hh + 1) % HEADS_PER_BLOCK) * HEAD_DIM_A
    head_rows = (row // HEAD_DIM_A) == hh
    ones_col = jnp.where(row == ones_idx, 1.0, 0.0)
    acc = None
    for p, seg in zip(probs, segs):
        vt_aug = jnp.where(head_rows, seg.vt.astype(F32), ones_col).astype(BF16)
        d = _dot(vt_aug, p)
        acc = d if acc is None else acc + d
    return jnp.where(head_rows, acc / acc[ones_idx:ones_idx + 1, :], 0.0)


def _attention_heads(q_ref, o_ref, seg_fn, bias_fn, pairs_per_iter, out_lag):
    def body(it, carry):
        pairs = [it * pairs_per_iter + sub for sub in range(pairs_per_iter)]
        cols = [pl.ds(pl.multiple_of(p * LANES, LANES), LANES) for p in pairs]
        heads = [(i, hh) for i in range(pairs_per_iter) for hh in range(HEADS_PER_BLOCK)]
        operands = {}

        def load(i):
            if i not in operands:
                operands[i] = (q_ref[:, cols[i]], seg_fn(pairs[i], cols[i]))
            return operands[i]

        def scores_of(t):
            i, hh = heads[t]
            q, segs = load(i)
            return _head_scores(q, segs, [bias_fn(pairs[i], hh, k) for k in range(len(segs))], hh)

        scores, probs, out = {}, {}, None
        for t in range(len(heads) + out_lag):
            if t < len(heads):
                scores[t] = scores_of(t)
            if PROBS_LAG <= t < len(heads) + PROBS_LAG:
                probs[t - PROBS_LAG] = _head_probs(scores.pop(t - PROBS_LAG))
            if t >= out_lag:
                i, hh = heads[t - out_lag]
                o = _head_output(probs.pop(t - out_lag), load(i)[1], hh)
                out = o if hh == 0 else out + o
                if hh == HEADS_PER_BLOCK - 1:
                    o_ref[:, cols[i]] = out.T.astype(o_ref.dtype)
        return carry

    lax.fori_loop(0, N_HEAD_BLOCKS // pairs_per_iter, body, 0)


def _keys_from_t(kt):
    return kt.T.astype(BF16)


def _ctx_attn_kernel(q_ref, kt_ref, vt_ref, o_ref):
    def seg_fn(p, cols):
        return [_Seg(_keys_from_t(kt_ref[0, cols, :]), vt_ref[0, cols, :])]

    _attention_heads(q_ref, o_ref, seg_fn, lambda p, hh, i: None, N_HEAD_BLOCKS, CTX_OUT_LAG)


def _ctx_attention(q, kt, vt, seq):
    t = q.shape[0]
    qspec = pl.BlockSpec((seq, WIDTH_A), lambda b: (b, 0))
    tspec = pl.BlockSpec((1, WIDTH_A, seq), lambda b: (b, 0, 0))
    return pl.pallas_call(
        _ctx_attn_kernel,
        out_shape=jax.ShapeDtypeStruct((t, WIDTH_A), BF16),
        grid=(t // seq,),
        in_specs=[qspec, tspec, tspec],
        out_specs=qspec,
        compiler_params=pltpu.CompilerParams(
            dimension_semantics=("parallel",), vmem_limit_bytes=VMEM_LIMIT),
        name="ctx_attn",
    )(q, kt, vt)


def _nbr_attn_kernel(q_ref, k0_ref, k1_ref, k2_ref, vt0_ref, vt1_ref, vt2_ref,
                     ckt_ref, cvt_ref, bias_ref, o_ref):
    local = ((k0_ref, vt0_ref), (k1_ref, vt1_ref), (k2_ref, vt2_ref))

    def run(windows):
        def seg_fn(p, cols):
            segs = [_Seg(local[w][0][:, cols], local[w][1][0, cols, :]) for w in windows]
            segs.append(_Seg(_keys_from_t(ckt_ref[0, cols, :]), cvt_ref[0, cols, :]))
            return segs

        def bias_fn(p, hh, i):
            if i >= len(windows):
                return None
            w = windows[i]
            return bias_ref[0, HEADS_PER_BLOCK * p + hh, w * Q_TOK:(w + 1) * Q_TOK, :]

        _attention_heads(q_ref, o_ref, seg_fn, bias_fn, PAIRS_PER_ITER, NBR_OUT_LAG)

    m = pl.program_id(0)
    last = pl.num_programs(0) - 1
    pl.when(m == 0)(lambda: run(tuple(range(WIN_BLOCKS - 1))))
    pl.when(m == last)(lambda: run(tuple(range(1, WIN_BLOCKS))))
    pl.when((m > 0) & (m < last))(lambda: run(tuple(range(WIN_BLOCKS))))


def _nbr_attention(q, k, vt, ckt, cvt, bias, n_tok):
    t = q.shape[0]
    batch = t // n_tok
    past = ckt.shape[2]
    qb = n_tok // Q_TOK
    assert qb > WIN_BLOCKS and NA_ROWS <= (WIN_BLOCKS - 1) * Q_ROWS
    def win(m):
        return jnp.clip(m - 1, 0, qb - WIN_BLOCKS)
    def variant(m):
        return jnp.minimum(m, 1) + jnp.maximum(m - (qb - 2), 0)
    qspec = pl.BlockSpec((Q_TOK, WIDTH_A), lambda m, b: (b * qb + m, 0))
    def kspec(w):
        return pl.BlockSpec((Q_TOK, WIDTH_A), lambda m, b: (b * qb + win(m) + w, 0))
    def vtspec(w):
        return pl.BlockSpec((1, WIDTH_A, Q_TOK), lambda m, b: (b, 0, win(m) + w))
    cspec = pl.BlockSpec((1, WIDTH_A, past), lambda m, b: (b, 0, 0))
    bspec = pl.BlockSpec((1, N_HEADS_A, WIN_BLOCKS * Q_TOK, Q_TOK),
                         lambda m, b: (variant(m), 0, 0, 0))
    return pl.pallas_call(
        _nbr_attn_kernel,
        out_shape=jax.ShapeDtypeStruct((t, WIDTH_A), BF16),
        grid=(qb, batch),
        in_specs=[qspec] + [kspec(w) for w in range(WIN_BLOCKS)]
                 + [vtspec(w) for w in range(WIN_BLOCKS)] + [cspec, cspec, bspec],
        out_specs=qspec,
        compiler_params=pltpu.CompilerParams(
            dimension_semantics=("parallel", "arbitrary"), vmem_limit_bytes=VMEM_LIMIT),
        name="nbr_attn",
    )(q, k, k, k, vt, vt, vt, ckt, cvt, bias)


def _bias_tiles(rpb_ref, o_ref, h, rows):
    n_dr = 2 * NA_ROWS - 1
    n_dc = 2 * NA_COLS - 1
    kc = lax.broadcasted_iota(jnp.int32, (GRID_W, LANES), 0)
    ln = lax.broadcasted_iota(jnp.int32, (GRID_W, LANES), 1)
    qc = ln % GRID_W
    col_start = jnp.clip(qc - NA_COLS // 2, 0, GRID_W - NA_COLS)
    col_ok = (kc >= col_start) & (kc < col_start + NA_COLS)
    neg = jnp.full((GRID_W, LANES), NEG_INF, F32)
    lane8 = lax.broadcasted_iota(jnp.int32, (8, LANES), 1)
    toeplitz = []
    for dr in range(n_dr):
        r8 = jnp.zeros((8, LANES), F32)
        for dc in range(n_dc):
            r8 = jnp.where(lane8 == n_dc - 1 - dc, rpb_ref[h * (n_dr * n_dc) + dr * n_dc + dc], r8)
        r = jnp.concatenate([r8] * (GRID_W // 8), axis=0)
        lo = pltpu.roll(r, LANES - (NA_COLS - 1), 1, stride=1, stride_axis=0)
        hi = pltpu.roll(r, GRID_W - (NA_COLS - 1), 1, stride=1, stride_axis=0)
        t = jnp.where(ln < GRID_W, lo, hi)
        toeplitz.append(jnp.where(col_ok, t * LOG2E, neg))
    qb = rows // Q_ROWS
    kr = min(NA_ROWS, rows)
    for var, m in enumerate((0, 1, qb - 1)):
        wb = min(max(m - 1, 0), qb - WIN_BLOCKS)
        for t in range(WIN_BLOCKS * Q_ROWS):
            key_row = Q_ROWS * wb + t
            for ap in range(Q_ROWS // 2):
                halves = []
                for a in (2 * ap, 2 * ap + 1):
                    r = Q_ROWS * m + a
                    row_start = min(max(r - kr // 2, 0), rows - kr)
                    ok = row_start <= key_row < row_start + kr
                    halves.append(toeplitz[key_row - r + NA_ROWS - 1] if ok else neg)
                tile = jnp.where(ln < GRID_W, halves[0], halves[1])
                o_ref[var, 0, t * GRID_W:(t + 1) * GRID_W, ap * LANES:(ap + 1) * LANES] = tile


def _mixout_kernel(x_ref, a_ref, u_ref, vg_ref, mod_ref, gn_ref, ws_ref, bs_ref,
                   na_ref, nb_ref, wo_hbm_ref, o_ref, lhs_ref, ssq_ref, wo_ref, sem):
    tm = x_ref.shape[0]
    n_parts = N_GROUPS_B // MIXOUT_GROUPS_PER_DOT
    a_cols = D_MODEL // n_parts
    k_rows = MIXOUT_GROUPS_PER_DOT * GROUP_DIM_B
    regions = []
    for n in range(n_parts):
        regions.append((slice(0, WIDTH_A), slice(n * a_cols, (n + 1) * a_cols)))
        regions.append((slice(WIDTH_A + n * k_rows, WIDTH_A + (n + 1) * k_rows), slice(0, D_MODEL)))

    def region_copy(k):
        rows, cols = regions[k]
        return pltpu.make_async_copy(wo_hbm_ref.at[rows, cols], wo_ref.at[rows, cols], sem.at[k])

    def gated_tile(g, c):
        rows = slice(c * CHUNK, (c + 1) * CHUNK)
        cols = slice(g * GROUP_DIM_B, (g + 1) * GROUP_DIM_B)
        vn = _rms(_gelu_tanh(vg_ref[rows, cols].astype(F32)), gn_ref[:, cols]).astype(BF16)
        gm = _gelu_tanh(u_ref[rows, cols].astype(F32)) * (_dot(ws_ref[g], vn) + bs_ref[:, g:g + 1])
        lhs_ref[rows, cols] = (gm * nb_ref[:, cols]).astype(BF16)
        sq = jnp.broadcast_to(jnp.sum(gm * gm, axis=-1, keepdims=True), (CHUNK, LANES))
        ssq_ref[rows, :] = sq if g == 0 else ssq_ref[rows, :] + sq

    def body(first):
        if first:
            for k in range(len(regions)):
                region_copy(k).start()
        gate = mod_ref[0, 5:6, :]
        ya = _rms(a_ref[...].astype(F32), na_ref[...]).astype(BF16)
        yb = None
        for n in range(n_parts):
            rows, cols = regions[2 * n]
            if first:
                region_copy(2 * n).wait()
            o_ref[:, cols] = x_ref[:, cols] + gate[:, cols] * _dot(ya, wo_ref[rows, cols])
            for g in range(n * MIXOUT_GROUPS_PER_DOT, (n + 1) * MIXOUT_GROUPS_PER_DOT):
                for c in range(tm // CHUNK):
                    gated_tile(g, c)
            if first:
                region_copy(2 * n + 1).wait()
            d = _dot(lhs_ref[:, n * k_rows:(n + 1) * k_rows], wo_ref[regions[2 * n + 1][0], :])
            yb = d if yb is None else yb + d
        row_scale = lax.rsqrt(ssq_ref[...] * (1.0 / WIDTH_B) + EPS)
        o_ref[...] += gate * (_lane_tile(row_scale, D_MODEL) * yb)

    pl.when(pl.program_id(0) == 0)(lambda: body(True))
    pl.when(pl.program_id(0) > 0)(lambda: body(False))


def _mixout(x, a, u, vg, mod, gmlp_norm, w_s, b_s_t, na_g, nb_g, w_out, *, rows_per_mod, tm=512):
    t = x.shape[0]
    tok = lambda w: pl.BlockSpec((tm, w), lambda i: (i, 0))
    full = lambda shape: pl.BlockSpec(shape, lambda i: (0,) * len(shape))
    return pl.pallas_call(
        _mixout_kernel,
        out_shape=jax.ShapeDtypeStruct((t, D_MODEL), F32),
        grid=(t // tm,),
        in_specs=[
            tok(D_MODEL), tok(WIDTH_A), tok(WIDTH_B), tok(WIDTH_B),
            pl.BlockSpec((1, N_MOD, D_MODEL), lambda i: (i * tm // rows_per_mod, 0, 0)),
            full((1, WIDTH_B)),
            full((N_GROUPS_B, CHUNK, CHUNK)),
            full((CHUNK, N_GROUPS_B)),
            full((1, WIDTH_A)),
            full((1, WIDTH_B)),
            pl.BlockSpec(memory_space=pl.ANY),
        ],
        out_specs=tok(D_MODEL),
        scratch_shapes=[pltpu.VMEM((tm, WIDTH_B), BF16), pltpu.VMEM((tm, LANES), F32),
                        pltpu.VMEM((MIX_WIDTH, D_MODEL), BF16),
                        pltpu.SemaphoreType.DMA((2 * (N_GROUPS_B // MIXOUT_GROUPS_PER_DOT),))],
        compiler_params=pltpu.CompilerParams(
            dimension_semantics=("arbitrary",), vmem_limit_bytes=VMEM_LIMIT),
        name="mixout",
    )(x, a, u, vg, mod, gmlp_norm, w_s, b_s_t, na_g, nb_g, w_out)


def kernel(x_prompt, x_sample, cache_k, cache_v, c, c_ctx, w_ada, b_ada, ffn1_norm, ffn1_w_gate, ffn1_w_up, ffn1_w_down, mix_norm, w_in, rpb, gmlp_norm, w_s, b_s, out_norm_a, out_norm_b, w_out, ffn2_norm, ffn2_w_gate, ffn2_w_up, ffn2_w_down, final_norm):
    batch, seq, _ = x_prompt.shape
    dec_batch, dec_seq, _ = x_sample.shape
    depth = w_ada.shape[0]
    x_ctx = x_prompt.reshape(batch * seq, D_MODEL)
    x_lat = x_sample.reshape(dec_batch * dec_seq, D_MODEL)
    final_g = final_norm.reshape(1, D_MODEL)

    cvec = jnp.concatenate([c_ctx[None, :], c], axis=0)
    mod_rows = -(-cvec.shape[0] // 8) * 8
    cvec = jnp.pad(cvec, ((0, mod_rows - cvec.shape[0]), (0, 0)))

    new_k, new_v = [], []
    for l in range(depth):
        last = l == depth - 1
        mod = _adaln(cvec, w_ada[l], b_ada[l][None, :]).reshape(mod_rows, N_MOD, D_MODEL)
        mods = (mod[0:1], mod[1:1 + dec_batch])
        rows_per_mod = (batch * seq, dec_seq)

        f1 = (ffn1_norm[l][None, :], ffn1_w_gate[l].astype(BF16), ffn1_w_up[l].astype(BF16),
              ffn1_w_down[l].astype(BF16))
        mix_g = mix_norm[l][None, :]
        w_s_l = w_s[l].astype(BF16)
        b_s_t = b_s[l].T
        gn = gmlp_norm[l][None, :]
        na_g = out_norm_a[l][None, :]
        nb_g = out_norm_b[l][None, :]
        ckt = jnp.transpose(cache_k[:, l], (0, 2, 3, 1)).reshape(dec_batch, WIDTH_A, -1)
        cvt = jnp.transpose(cache_v[:, l], (0, 2, 3, 1)).reshape(dec_batch, WIDTH_A, -1)

        n_j = D_FF // FFN_TF
        n_ctx, n_lat = x_ctx.shape[0] // FFN_TM, x_lat.shape[0] // FFN_TM
        x_ctx, w_in_l, w_out_l = _ffn(
            x_ctx, mods[0], *f1, mod_base=0, rows_per_mod=rows_per_mod[0],
            side=(_cast_plan(w_in[l], n_ctx, n_j), _cast_plan(w_out[l], n_ctx, n_j)))
        x_lat, *f2_w = _ffn(
            x_lat, mods[1], *f1, mod_base=0, rows_per_mod=rows_per_mod[1],
            side=(_cast_plan_tiled(ffn2_w_gate[l], n_lat, n_j), _cast_plan_tiled(ffn2_w_up[l], n_lat, n_j),
                  _cast_plan_t(ffn2_w_down[l], n_lat, n_j)))
        f2 = (ffn2_norm[l][None, :], *f2_w)

        xs = []
        for path, (x, m, rpm) in enumerate(zip((x_ctx, x_lat), mods, rows_per_mod)):
            q, k, v, u, vg, *bias = _mixin(x, m, mix_g, w_in_l, rows_per_mod=rpm,
                                           seq=seq if path == 0 else dec_seq, k_t=path == 0,
                                           kv_dtype=F32 if path == 0 else BF16,
                                           rpb=None if path == 0 else rpb[l])
            if path == 0:
                a = _ctx_attention(q, k, v, seq)
                for store, kt in ((new_k, k), (new_v, v)):
                    kt = kt.reshape(batch, N_HEADS_A, HEAD_DIM_A, seq)
                    store.append(jnp.transpose(kt, (0, 3, 1, 2)))
            else:
                a = _nbr_attention(q, k, v, ckt, cvt, bias[0], dec_seq)
            x = _mixout(x, a, u, vg, m, gn, w_s_l, b_s_t, na_g, nb_g, w_out_l, rows_per_mod=rpm)
            x, = _ffn(x, m, *f2, mod_base=6, rows_per_mod=rpm,
                      final_g=final_g if last else None)
            xs.append(x)
        x_ctx, x_lat = xs

    if depth == 0:
        raise ValueError("depth must be positive")
    y_prompt = x_ctx.reshape(batch, seq, D_MODEL)
    y_sample = x_lat.reshape(dec_batch, dec_seq, D_MODEL)
    return (y_prompt, y_sample, jnp.stack(new_k, axis=1), jnp.stack(new_v, axis=1))
```
